```python
import math
import jax
import jax.numpy as jnp
from jax import lax
import numpy as np

D_MODEL = 1024
BATCH = 32
SEQ = 256
DEPTH = 2
DEC_BATCH = 8
DEC_SEQ = 2048
PAST_LEN = 256

GRID_W = 64
ROPE_THETA = 10000.0
EPS = 1e-6
Q_BLOCK = 128
MLA_HEADS = 4
MLA_NOPE = 64
MLA_ROPE = 32
MLA_V = 64
MLA_Q_RANK = 256
MLA_KV_RANK = 128
ML_HEADS = 4
ML_DH = 128
ML_WIDTH = ML_HEADS * ML_DH
ML_CHUNK = 128
GQA_HEADS = 4
GQA_KV_HEADS = 2
GQA_DH = 64
D_FF = 2816

A_WIDTH = MLA_HEADS * MLA_V
C_WIDTH = GQA_HEADS * GQA_DH
MIX_WIDTH = A_WIDTH + ML_WIDTH + C_WIDTH
IN_SIZES = (MLA_Q_RANK, MLA_KV_RANK, MLA_ROPE, ML_WIDTH, ML_WIDTH, ML_WIDTH, 4 * ML_HEADS,
            GQA_HEADS * GQA_DH, GQA_KV_HEADS * GQA_DH, GQA_KV_HEADS * GQA_DH)
IN_COLS = MLA_Q_RANK + MLA_KV_RANK + MLA_ROPE + 3 * ML_WIDTH + 4 * ML_HEADS + (GQA_HEADS + 2 * GQA_KV_HEADS) * GQA_DH
MLA_SCALE = (MLA_NOPE + MLA_ROPE) ** -0.5
GQA_SCALE = GQA_DH ** -0.5

kernel_name = 'hybrid_mla_mlstm_gqa_diffusion_step'


def rmsnorm(x, g):
    xf = x.astype(jnp.float32)
    xf = xf * lax.rsqrt(jnp.mean(xf * xf, axis=-1, keepdims=True) + EPS)
    return xf.astype(x.dtype) * g


def grid_positions(n_tokens):
    rows = n_tokens // GRID_W
    r, c = jnp.meshgrid(jnp.arange(rows, dtype=jnp.int32), jnp.arange(GRID_W, dtype=jnp.int32), indexing='ij')
    return r.reshape(-1), c.reshape(-1)


def rope_1d(x, pos):
    d = x.shape[-1]
    inv = ROPE_THETA ** (-jnp.arange(0, d, 2, dtype=jnp.float32) / d)
    ang = pos.astype(jnp.float32)[:, None] * inv[None, :]
    cos = jnp.cos(ang)[:, None, :].astype(x.dtype)
    sin = jnp.sin(ang)[:, None, :].astype(x.dtype)
    x1, x2 = x[..., : d // 2], x[..., d // 2:]
    return jnp.concatenate([x1 * cos - x2 * sin, x1 * sin + x2 * cos], axis=-1)


def rope_2d(x, rows, cols):
    h = x.shape[-1] // 2
    return jnp.concatenate([rope_1d(x[..., :h], rows), rope_1d(x[..., h:], cols)], axis=-1)


def dwconv_centred(x, w, b):
    k = w.shape[0]
    p = k // 2
    t = x.shape[1]
    xp = jnp.pad(x, ((0, 0), (p, p), (0, 0)))
    return sum(xp[:, j:j + t] * w[j] for j in range(k)) + b


def modulation(cvec, lp):
    m = jax.nn.silu(cvec) @ lp['w_ada'] + lp['b_ada']
    return jnp.split(m[:, None, :], 6, axis=-1)


def adaln_in(x, g, shift, scale):
    return rmsnorm(x, g) * (1.0 + scale) + shift


def split_in(z):
    idx = [int(v) for v in np.cumsum(IN_SIZES)[:-1]]
    return jnp.split(z, idx, axis=-1)


def block_attention(q, k, v, scale):
    b, tq, h, dk = q.shape
    hk = k.shape[2]
    g = h // hk
    nb = tq // Q_BLOCK
    qb = q.reshape(b, nb, Q_BLOCK, hk, g, dk).transpose(1, 0, 2, 3, 4, 5)

    def one(qblk):
        s = jnp.einsum('bqhgd,bkhd->bhgqk', qblk, k).astype(jnp.float32) * scale
        p = jax.nn.softmax(s, axis=-1).astype(v.dtype)
        return jnp.einsum('bhgqk,bkhd->bqhgd', p, v)

    o = lax.map(one, qb)
    return o.transpose(1, 0, 2, 3, 4, 5).reshape(b, tq, h, v.shape[-1])


def mla_queries(cq, lp):
    b, t, _ = cq.shape
    q = (rmsnorm(cq, lp['g_mla_q']) @ lp['w_mla_uq']).reshape(b, t, MLA_HEADS, MLA_NOPE + MLA_ROPE)
    return q[..., :MLA_NOPE], q[..., MLA_NOPE:]


def mla_kv(ckv, lp):
    b, t, _ = ckv.shape
    kv = (ckv @ lp['w_mla_ukv']).reshape(b, t, MLA_HEADS, MLA_NOPE + MLA_V)
    return kv[..., :MLA_NOPE], kv[..., MLA_NOPE:]


def mla_keys(k_nope, k_rope):
    kr = jnp.broadcast_to(k_rope[:, :, None, :], k_nope.shape[:-1] + (MLA_ROPE,))
    return jnp.concatenate([k_nope, kr], axis=-1)


def gqa_qkv(q_g, k_g, v_g, lp):
    b, t, _ = q_g.shape
    q = rmsnorm(q_g.reshape(b, t, GQA_HEADS, GQA_DH), lp['g_gqa_q'])
    k = rmsnorm(k_g.reshape(b, t, GQA_KV_HEADS, GQA_DH), lp['g_gqa_k'])
    v = v_g.reshape(b, t, GQA_KV_HEADS, GQA_DH)
    return q, k, v


def mlstm_chunkwise(q, k, v, i_pre, f_pre, c0, n0, m0):
    b, t, h, d = q.shape
    lc = ML_CHUNK
    nc = t // lc

    def to_chunks(a):
        a = a.astype(jnp.float32).reshape((b, nc, lc, h) + a.shape[3:])
        return jnp.swapaxes(jnp.moveaxis(a, 1, 0), 2, 3)

    li_all = i_pre
    lf_all = jax.nn.log_sigmoid(f_pre.astype(jnp.float32))
    xs = (to_chunks(q), to_chunks(k), to_chunks(v), to_chunks(li_all), to_chunks(lf_all))
    tri = jnp.tril(jnp.ones((lc, lc), dtype=bool))

    def step(carry, xc):
        cm, nm, mm = carry
        qc, kc, vc, li, lf = xc
        bc = jnp.cumsum(lf, axis=-1)
        dlog = bc[..., :, None] - bc[..., None, :] + li[..., None, :]
        dlog = jnp.where(tri, dlog, -jnp.inf)
        g = bc + mm[..., None]
        m_t = jnp.maximum(g, jnp.max(dlog, axis=-1))
        w = jnp.exp(dlog - m_t[..., None])
        inter = jnp.exp(g - m_t)
        s = jnp.einsum('bhtd,bhsd->bhts', qc, kc) * w
        num = jnp.einsum('bhts,bhsd->bhtd', s, vc) + inter[..., None] * jnp.einsum('bhtd,bhde->bhte', qc, cm)
        den = jnp.sum(s, axis=-1) + inter * jnp.einsum('bhtd,bhd->bht', qc, nm)
        hc = num / jnp.maximum(jnp.abs(den), jnp.exp(-m_t))[..., None]
        b_last = bc[..., -1]
        m_new = m_t[..., -1]
        wk = jnp.exp(b_last[..., None] - bc + li - m_new[..., None])
        decay = jnp.exp(b_last + mm - m_new)
        c_new = decay[..., None, None] * cm + jnp.einsum('bhs,bhsd,bhse->bhde', wk, kc, vc)
        n_new = decay[..., None] * nm + jnp.einsum('bhs,bhsd->bhd', wk, kc)
        return (c_new, n_new, m_new), hc

    init = (c0.astype(jnp.float32), n0.astype(jnp.float32), m0.astype(jnp.float32))
    (cf, nf, mf), hs = lax.scan(step, init, xs)
    hs = hs.transpose(1, 0, 3, 2, 4).reshape(b, t, h, d)
    return hs, (cf, nf, mf)


def mlstm_mixer(u, v_ml, o_ml, gates, lp, c0, n0, m0):
    b, t, _ = u.shape
    uc = jax.nn.silu(dwconv_centred(u, lp['w_ml_conv'], lp['b_ml_conv'])).reshape(b, t, ML_HEADS, ML_DH)
    q = jnp.einsum('bthd,hde->bthe', uc, lp['w_ml_q'])
    k = jnp.einsum('bthd,hde->bthe', uc, lp['w_ml_k']) * (ML_DH ** -0.5)
    v = v_ml.reshape(b, t, ML_HEADS, ML_DH)
    g = (gates + lp['b_ml_gates']).astype(jnp.float32).reshape(b, t, 4, ML_HEADS)
    h_f, (cf, nf, mf) = mlstm_chunkwise(q, k, v, g[:, :, 0], g[:, :, 1], c0[:, 0], n0[:, 0], m0[:, 0])
    rev = lambda a: jnp.flip(a, axis=1)
    h_b, (cb, nb, mb) = mlstm_chunkwise(rev(q), rev(k), rev(v), rev(g[:, :, 2]), rev(g[:, :, 3]),
                                        c0[:, 1], n0[:, 1], m0[:, 1])
    hsum = (h_f + rev(h_b)).astype(u.dtype)
    hn = rmsnorm(hsum, lp['g_ml_out'].reshape(ML_HEADS, ML_DH)).reshape(b, t, ML_WIDTH)
    out = hn * jax.nn.sigmoid(o_ml)
    return out, (jnp.stack([cf, cb], axis=1), jnp.stack([nf, nb], axis=1), jnp.stack([mf, mb], axis=1))


def merge_heads(o_a, o_b, o_c, lp):
    b, t = o_b.shape[:2]
    o = jnp.concatenate([o_a.reshape(b, t, A_WIDTH), o_b, o_c.reshape(b, t, C_WIDTH)], axis=-1)
    return o @ lp['w_out']


def mixer_context(y, lp):
    b, t, _ = y.shape
    cq, ckv_raw, krope, u, v_ml, o_ml, gates, q_g, k_g, v_g = split_in(y @ lp['w_in'])
    ckv = rmsnorm(ckv_raw, lp['g_mla_kv'])
    q_nope, q_rope = mla_queries(cq, lp)
    k_nope, v_a = mla_kv(ckv, lp)
    o_a = block_attention(jnp.concatenate([q_nope, q_rope], axis=-1), mla_keys(k_nope, krope), v_a, MLA_SCALE)
    c0 = jnp.zeros((b, 2, ML_HEADS, ML_DH, ML_DH), y.dtype)
    n0 = jnp.zeros((b, 2, ML_HEADS, ML_DH), y.dtype)
    m0 = jnp.zeros((b, 2, ML_HEADS), y.dtype)
    o_b, (cs, ns, ms) = mlstm_mixer(u, v_ml, o_ml, gates, lp, c0, n0, m0)
    q_c, k_c, v_c = gqa_qkv(q_g, k_g, v_g, lp)
    o_c = block_attention(q_c, k_c, v_c, GQA_SCALE)
    return merge_heads(o_a, o_b, o_c, lp), (ckv, krope, k_c, v_c, cs, ns, ms)


def mixer_latent(y, lp, ctx, rows, cols):
    ckv_ctx, krope_ctx, k_ctx, v_ctx, c0, n0, m0 = ctx
    cq, ckv_raw, krope, u, v_ml, o_ml, gates, q_g, k_g, v_g = split_in(y @ lp['w_in'])
    q_nope, q_rope = mla_queries(cq, lp)
    q_a = jnp.concatenate([q_nope, rope_2d(q_rope, rows, cols)], axis=-1)
    k_nope, v_lat = mla_kv(rmsnorm(ckv_raw, lp['g_mla_kv']), lp)
    kr = rope_2d(krope[:, :, None, :], rows, cols)[:, :, 0]
    k_nope_c, v_ctx_a = mla_kv(ckv_ctx, lp)
    k_a = jnp.concatenate([mla_keys(k_nope, kr), mla_keys(k_nope_c, krope_ctx)], axis=1)
    v_a = jnp.concatenate([v_lat, v_ctx_a], axis=1)
    o_a = block_attention(q_a, k_a, v_a, MLA_SCALE)
    o_b, _ = mlstm_mixer(u, v_ml, o_ml, gates, lp, c0, n0, m0)
    q_c, k_c, v_c = gqa_qkv(q_g, k_g, v_g, lp)
    q_c = rope_2d(q_c, rows, cols)
    k_c = rope_2d(k_c, rows, cols)
    o_c = block_attention(q_c, jnp.concatenate([k_c, k_ctx], axis=1), jnp.concatenate([v_c, v_ctx], axis=1), GQA_SCALE)
    return merge_heads(o_a, o_b, o_c, lp)


def conv_ffn(y, lp):
    a, g = jnp.split(y @ lp['w_ff_up'], 2, axis=-1)
    g = dwconv_centred(g, lp['w_ff_conv'], lp['b_ff_conv'])
    return (jax.nn.silu(g) * a) @ lp['w_ff_down']


def setup_inputs(seed: int = 0) -> dict:
    key = jax.random.key(seed)
    ks = iter(jax.random.split(key, 40))

    def nrm(shape, scale=1.0):
        return jax.random.normal(next(ks), shape, jnp.float32) * scale

    L = DEPTH
    gate_base = jnp.repeat(jnp.array([0.0, 3.0, 0.0, 3.0], jnp.float32), ML_HEADS)
    return {
        'x_prompt': nrm((BATCH, SEQ, D_MODEL)),
        'x_sample': nrm((DEC_BATCH, DEC_SEQ, D_MODEL)),
        'cache_mla_ckv': nrm((DEC_BATCH, L, PAST_LEN, MLA_KV_RANK)),
        'cache_mla_krope': nrm((DEC_BATCH, L, PAST_LEN, MLA_ROPE)),
        'cache_gqa_k': nrm((DEC_BATCH, L, PAST_LEN, GQA_KV_HEADS, GQA_DH)),
        'cache_gqa_v': nrm((DEC_BATCH, L, PAST_LEN, GQA_KV_HEADS, GQA_DH)),
        'state_mlstm_C': nrm((DEC_BATCH, L, 2, ML_HEADS, ML_DH, ML_DH), 0.1),
        'state_mlstm_n': nrm((DEC_BATCH, L, 2, ML_HEADS, ML_DH), 0.1),
        'state_mlstm_m': nrm((DEC_BATCH, L, 2, ML_HEADS)),
        'c': nrm((DEC_BATCH, D_MODEL)),
        'c_ctx': nrm((D_MODEL,)),
        'w_ada': nrm((L, D_MODEL, 6 * D_MODEL), 0.5 * D_MODEL ** -0.5),
        'b_ada': nrm((L, 6 * D_MODEL), 0.01),
        'g_norm1': 1.0 + nrm((L, D_MODEL), 0.01),
        'g_norm2': 1.0 + nrm((L, D_MODEL), 0.01),
        'w_in': nrm((L, D_MODEL, IN_COLS), D_MODEL ** -0.5),
        'g_mla_q': 1.0 + nrm((L, MLA_Q_RANK), 0.01),
        'w_mla_uq': nrm((L, MLA_Q_RANK, MLA_HEADS * (MLA_NOPE + MLA_ROPE)), MLA_Q_RANK ** -0.5),
        'g_mla_kv': 1.0 + nrm((L, MLA_KV_RANK), 0.01),
        'w_mla_ukv': nrm((L, MLA_KV_RANK, MLA_HEADS * (MLA_NOPE + MLA_V)), MLA_KV_RANK ** -0.5),
        'w_ml_conv': nrm((L, 3, ML_WIDTH), 0.5),
        'b_ml_conv': nrm((L, ML_WIDTH), 0.01),
        'w_ml_q': nrm((L, ML_HEADS, ML_DH, ML_DH), ML_DH ** -0.5),
        'w_ml_k': nrm((L, ML_HEADS, ML_DH, ML_DH), ML_DH ** -0.5),
        'b_ml_gates': gate_base + nrm((L, 4 * ML_HEADS), 0.1),
        'g_ml_out': 1.0 + nrm((L, ML_WIDTH), 0.01),
        'g_gqa_q': 1.0 + nrm((L, GQA_DH), 0.01),
        'g_gqa_k': 1.0 + nrm((L, GQA_DH), 0.01),
        'w_out': nrm((L, MIX_WIDTH, D_MODEL), MIX_WIDTH ** -0.5),
        'w_ff_up': nrm((L, D_MODEL, 2 * D_FF), D_MODEL ** -0.5),
        'w_ff_conv': nrm((L, 3, D_FF), 0.5),
        'b_ff_conv': nrm((L, D_FF), 0.01),
        'w_ff_down': nrm((L, D_FF, D_MODEL), D_FF ** -0.5),
        'g_final': 1.0 + nrm((D_MODEL,), 0.01),
    }


def reference(x_prompt, x_sample, cache_mla_ckv, cache_mla_krope, cache_gqa_k, cache_gqa_v,
              state_mlstm_C, state_mlstm_n, state_mlstm_m, c, c_ctx,
              w_ada, b_ada, g_norm1, g_norm2, w_in, g_mla_q, w_mla_uq, g_mla_kv, w_mla_ukv,
              w_ml_conv, b_ml_conv, w_ml_q, w_ml_k, b_ml_gates, g_ml_out, g_gqa_q, g_gqa_k,
              w_out, w_ff_up, w_ff_conv, b_ff_conv, w_ff_down, g_final):
    params = {
        'w_ada': w_ada, 'b_ada': b_ada, 'g_norm1': g_norm1, 'g_norm2': g_norm2, 'w_in': w_in,
        'g_mla_q': g_mla_q, 'w_mla_uq': w_mla_uq, 'g_mla_kv': g_mla_kv, 'w_mla_ukv': w_mla_ukv,
        'w_ml_conv': w_ml_conv, 'b_ml_conv': b_ml_conv, 'w_ml_q': w_ml_q, 'w_ml_k': w_ml_k,
        'b_ml_gates': b_ml_gates, 'g_ml_out': g_ml_out, 'g_gqa_q': g_gqa_q, 'g_gqa_k': g_gqa_k,
        'w_out': w_out, 'w_ff_up': w_ff_up, 'w_ff_conv': w_ff_conv, 'b_ff_conv': b_ff_conv,
        'w_ff_down': w_ff_down,
    }
    rows, cols = grid_positions(x_sample.shape[1])
    xp = x_prompt
    xs = x_sample
    new_state = [[] for _ in range(7)]
    for l in range(DEPTH):
        lp = {name: arr[l] for name, arr in params.items()}
        sh1, sc1, gt1, sh2, sc2, gt2 = modulation(c_ctx[None, :], lp)
        out, ctx_t = mixer_context(adaln_in(xp, lp['g_norm1'], sh1, sc1), lp)
        xp = xp + gt1 * out
        xp = xp + gt2 * conv_ffn(adaln_in(xp, lp['g_norm2'], sh2, sc2), lp)
        for acc, tns in zip(new_state, ctx_t):
            acc.append(tns)
        cache_l = (cache_mla_ckv[:, l], cache_mla_krope[:, l], cache_gqa_k[:, l], cache_gqa_v[:, l],
                   state_mlstm_C[:, l], state_mlstm_n[:, l], state_mlstm_m[:, l])
        sh1, sc1, gt1, sh2, sc2, gt2 = modulation(c, lp)
        xs = xs + gt1 * mixer_latent(adaln_in(xs, lp['g_norm1'], sh1, sc1), lp, cache_l, rows, cols)
        xs = xs + gt2 * conv_ffn(adaln_in(xs, lp['g_norm2'], sh2, sc2), lp)
    y_prompt = rmsnorm(xp, g_final)
    y_sample = rmsnorm(xs, g_final)
    new_mla_ckv = jnp.stack(new_state[0], axis=1)
    new_mla_krope = jnp.stack(new_state[1], axis=1)
    new_gqa_k = jnp.stack(new_state[2], axis=1)
    new_gqa_v = jnp.stack(new_state[3], axis=1)
    new_mlstm_C = jnp.stack(new_state[4], axis=1)
    new_mlstm_n = jnp.stack(new_state[5], axis=1)
    new_mlstm_m = jnp.stack(new_state[6], axis=1)
    return (y_prompt, y_sample, new_mla_ckv, new_mla_krope, new_gqa_k, new_gqa_v, new_mlstm_C, new_mlstm_n, new_mlstm_m)
```

```python
import functools

import jax
import jax.numpy as jnp
from jax import lax
from jax.experimental import pallas as pl
from jax.experimental.pallas import tpu as pltpu

F32 = jnp.float32
BF16 = jnp.bfloat16

D_MODEL = 1024
GRID_W = 64
ROPE_THETA = 10000.0
EPS = 1e-6
MLA_HEADS = 4
MLA_NOPE = 64
MLA_ROPE = 32
MLA_V = 64
MLA_Q_RANK = 256
MLA_KV_RANK = 128
ML_HEADS = 4
ML_DH = 128
ML_WIDTH = ML_HEADS * ML_DH
ML_CHUNK = 128
GQA_HEADS = 4
GQA_KV_HEADS = 2
GQA_DH = 64
D_FF = 2816
MLA_SCALE = (MLA_NOPE + MLA_ROPE) ** -0.5
GQA_SCALE = GQA_DH ** -0.5
ML_K_SCALE = ML_DH ** -0.5

LANES = 128
HALO = 16
VMEM_LIMIT = 52 * 1024 * 1024

C_CQ, C_CKV, C_KR, C_U, C_V, C_O, C_G, C_QG, C_KG, C_VG, IN_COLS_P = (
    0, 256, 384, 512, 1024, 1536, 2048, 2176, 2432, 2560, 2688)
FF_CHUNK = 256
TM = 512
TQ = 512


def _cparams(sem):
    return pltpu.CompilerParams(dimension_semantics=sem, vmem_limit_bytes=VMEM_LIMIT)


def _dot(a, b):
    return jnp.dot(a, b, preferred_element_type=F32)


def _dot_nt(a, b):
    return lax.dot_general(a, b, (((1,), (1,)), ((), ())), preferred_element_type=F32)


def _dot_tn(a, b):
    return lax.dot_general(a, b, (((0,), (0,)), ((), ())), preferred_element_type=F32)


def _rms(x, g):
    return (x * lax.rsqrt(jnp.mean(x * x, axis=-1, keepdims=True) + EPS)) * g


def _silu(x):
    return x * jax.nn.sigmoid(x)


def _adaln(x, g, shift, scale):
    return _rms(x, g) * (1.0 + scale) + shift


def _split2(x):
    hi = x.astype(BF16)
    lo = (x - hi.astype(F32)).astype(BF16)
    return hi, lo


def _split3(x):
    h1 = x.astype(BF16)
    r1 = x - h1.astype(F32)
    h2 = r1.astype(BF16)
    h3 = (r1 - h2.astype(F32)).astype(BF16)
    return h1, h2, h3


def _rope(x, cos, sa, sb, half):
    w = x.shape[-1]
    return x * cos + pltpu.roll(x, w - half, 1) * sa + pltpu.roll(x, half, 1) * sb


def _group_rms(x, gsum, g):
    hi, lo = _split2(x * x)
    ss = _dot(hi, gsum) + _dot(lo, gsum)
    return (x * lax.rsqrt(ss * (1.0 / GQA_DH) + EPS)) * g


def _mod_kernel(c_ref, w_ref, b_ref, o_ref):
    a = _silu(c_ref[...]).astype(BF16)
    o_ref[...] = _dot(a, w_ref[...].astype(BF16)) + b_ref[...]


def _modulation(cvec, w_ada, b_ada):
    nl = w_ada.shape[0]
    tn = 1536
    return pl.pallas_call(
        _mod_kernel,
        grid=(nl, 6 * D_MODEL // tn),
        in_specs=[
            pl.BlockSpec((16, D_MODEL), lambda l, j: (0, 0)),
            pl.BlockSpec((None, D_MODEL, tn), lambda l, j: (l, 0, j)),
            pl.BlockSpec((None, 1, tn), lambda l, j: (l, 0, j)),
        ],
        out_specs=pl.BlockSpec((None, 16, tn), lambda l, j: (l, 0, j)),
        out_shape=jax.ShapeDtypeStruct((nl, 16, 6 * D_MODEL), F32),
        compiler_params=_cparams(("arbitrary", "arbitrary")),
        name="modulation",
    )(cvec, w_ada, b_ada.reshape(nl, 1, 6 * D_MODEL))


def _in_proj_kernel(is_ctx, x_ref, mod_ref, g1_ref, win_ref, gq_ref, wuq_ref, gkv_ref, wukv_ref,
                    bg_ref, ggq_ref, ggk_ref, gsum_ref, *rest):
    if is_ctx:
        (qm_ref, km_ref, vm_ref, u_ref, vml_ref, oml_ref, gt_ref, qg_ref, kg_ref, vg_ref,
         ckv_ref, kr_ref) = rest
    else:
        (cm_ref, sam_ref, sbm_ref, cg_ref, sag_ref, sbg_ref,
         qm_ref, km_ref, vm_ref, u_ref, vml_ref, oml_ref, gt_ref, qg_ref, kg_ref, vg_ref) = rest

    y = _adaln(x_ref[...], g1_ref[...], mod_ref[0:1, :], mod_ref[1:2, :]).astype(BF16)

    def proj(c0, width):
        return _dot(y, win_ref[:, c0:c0 + width])

    cqn = _rms(proj(C_CQ, MLA_Q_RANK), gq_ref[...]).astype(BF16)
    qz = _dot(cqn, wuq_ref[...])
    for h in range(MLA_HEADS):
        blk = qz[:, LANES * h:LANES * (h + 1)]
        if not is_ctx:
            blk = _rope(blk, cm_ref[...], sam_ref[...], sbm_ref[...], 8)
        qm_ref[:, LANES * h:LANES * (h + 1)] = blk.astype(qm_ref.dtype)

    ckvn = _rms(proj(C_CKV, MLA_KV_RANK), gkv_ref[...])
    kr = proj(C_KR, LANES)
    if is_ctx:
        ckv_ref[...] = ckvn
        kr_ref[...] = kr
    else:
        kr = _rope(kr, cm_ref[...], sam_ref[...], sbm_ref[...], 8)
    kvz = _dot(ckvn.astype(BF16), wukv_ref[...])
    for h in range(MLA_HEADS):
        km_ref[:, LANES * h:LANES * (h + 1)] = (kvz[:, LANES * h:LANES * (h + 1)] + kr).astype(km_ref.dtype)
    vm_ref[...] = kvz[:, MLA_HEADS * LANES:].astype(vm_ref.dtype)

    u_ref[...] = proj(C_U, ML_WIDTH).astype(u_ref.dtype)
    vml_ref[...] = proj(C_V, ML_WIDTH).astype(vml_ref.dtype)
    oml_ref[...] = proj(C_O, ML_WIDTH).astype(oml_ref.dtype)
    gt_ref[...] = proj(C_G, LANES) + bg_ref[...]

    qg = _group_rms(proj(C_QG, GQA_HEADS * GQA_DH), gsum_ref[...], ggq_ref[...])
    kg = _group_rms(proj(C_KG, LANES), gsum_ref[0:LANES, 0:LANES], ggk_ref[...])
    if not is_ctx:
        qg = jnp.concatenate(
            [_rope(qg[:, 0:LANES], cg_ref[...], sag_ref[...], sbg_ref[...], 16),
             _rope(qg[:, LANES:], cg_ref[...], sag_ref[...], sbg_ref[...], 16)], axis=1)
        kg = _rope(kg, cg_ref[...], sag_ref[...], sbg_ref[...], 16)
    qg_ref[...] = qg.astype(qg_ref.dtype)
    kg_ref[...] = kg.astype(kg_ref.dtype)
    vg_ref[...] = proj(C_VG, LANES).astype(vg_ref.dtype)


def _in_proj(x2, mod, seq, is_ctx, wl, tabs):
    m = x2.shape[0]
    tm = TM
    nt = m // tm
    if is_ctx:
        mod_map = lambda i: (0, 0, 0)
    else:
        mod_map = lambda i: (1 + (i * tm) // seq, 0, 0)
    const = lambda i: (0, 0)
    row = lambda i: (i, 0)
    in_specs = [
        pl.BlockSpec((tm, D_MODEL), row),
        pl.BlockSpec((None, 6, D_MODEL), mod_map),
        pl.BlockSpec((1, D_MODEL), const),
        pl.BlockSpec((D_MODEL, IN_COLS_P), const),
        pl.BlockSpec((1, MLA_Q_RANK), const),
        pl.BlockSpec((MLA_Q_RANK, MLA_HEADS * LANES), const),
        pl.BlockSpec((1, MLA_KV_RANK), const),
        pl.BlockSpec((MLA_KV_RANK, MLA_HEADS * LANES + MLA_HEADS * MLA_V), const),
        pl.BlockSpec((1, LANES), const),
        pl.BlockSpec((1, GQA_HEADS * GQA_DH), const),
        pl.BlockSpec((1, LANES), const),
        pl.BlockSpec((2 * LANES, 2 * LANES), const),
    ]
    args = [x2, mod, wl["g_norm1"], wl["w_in"], wl["g_mla_q"], wl["w_uq"], wl["g_mla_kv"], wl["w_ukv"],
            wl["b_gates"], wl["g_gqa_q"], wl["g_gqa_k"], wl["gsum"]]
    if not is_ctx:
        tpos = seq // tm
        in_specs += [pl.BlockSpec((tm, LANES), lambda i: (i % tpos, 0))] * 6
        args += list(tabs)
    act = F32 if is_ctx else BF16
    outs = [
        (MLA_HEADS * LANES, BF16),
        (MLA_HEADS * LANES, BF16),
        (MLA_HEADS * MLA_V, BF16),
        (ML_WIDTH, F32),
        (ML_WIDTH, BF16),
        (ML_WIDTH, F32),
        (LANES, F32),
        (GQA_HEADS * GQA_DH, BF16),
        (LANES, act),
        (LANES, act),
    ]
    if is_ctx:
        outs += [(MLA_KV_RANK, F32), (LANES, F32)]
    return pl.pallas_call(
        functools.partial(_in_proj_kernel, is_ctx),
        grid=(nt,),
        in_specs=in_specs,
        out_specs=[pl.BlockSpec((tm, w), row) for w, _ in outs],
        out_shape=[jax.ShapeDtypeStruct((m, w), dt) for w, dt in outs],
        compiler_params=_cparams(("parallel",)),
        name="in_proj_ctx" if is_ctx else "in_proj_lat",
    )(*args)


def _cache_kv_kernel(ckv_ref, kr_ref, wukv_ref, k_ref, v_ref):
    kvz = _dot(ckv_ref[...].astype(BF16), wukv_ref[...])
    kr = kr_ref[...]
    for h in range(MLA_HEADS):
        k_ref[:, LANES * h:LANES * (h + 1)] = (kvz[:, LANES * h:LANES * (h + 1)] + kr).astype(k_ref.dtype)
    v_ref[...] = kvz[:, MLA_HEADS * LANES:].astype(v_ref.dtype)


def _cache_kv(ckv2, kr2, w_ukv):
    m = ckv2.shape[0]
    tm = 256
    row = lambda i: (i, 0)
    return pl.pallas_call(
        _cache_kv_kernel,
        grid=(m // tm,),
        in_specs=[pl.BlockSpec((tm, MLA_KV_RANK), row), pl.BlockSpec((tm, LANES), row),
                  pl.BlockSpec(w_ukv.shape, lambda i: (0, 0))],
        out_specs=[pl.BlockSpec((tm, MLA_HEADS * LANES), row), pl.BlockSpec((tm, MLA_HEADS * MLA_V), row)],
        out_shape=[jax.ShapeDtypeStruct((m, MLA_HEADS * LANES), BF16),
                   jax.ShapeDtypeStruct((m, MLA_HEADS * MLA_V), BF16)],
        compiler_params=_cparams(("parallel",)),
        name="cache_kv",
    )(ckv2, kr2, w_ukv)


def _attn_kernel(is_mla, nseg, scale, q_ref, *refs):
    kv = refs[:2 * nseg]
    o_ref = refs[2 * nseg]
    tq = q_ref.shape[0]
    lo = lax.broadcasted_iota(jnp.int32, (tq, LANES), 1) < (LANES // 2)
    heads = []
    for h in range(4):
        if is_mla:
            qh = q_ref[:, LANES * h:LANES * (h + 1)]
            ksl = slice(LANES * h, LANES * (h + 1))
            vsl = slice(LANES * (h // 2), LANES * (h // 2 + 1))
        else:
            blk = q_ref[:, LANES * (h % 2):LANES * (h % 2 + 1)].astype(F32)
            qh = jnp.where(lo if h < 2 else jnp.logical_not(lo), blk, 0.0).astype(BF16)
            ksl = slice(0, LANES)
            vsl = slice(0, LANES)
        ss = [_dot_nt(qh, kv[2 * j][:, ksl].astype(BF16)) * scale for j in range(nseg)]
        mx = jnp.max(ss[0], axis=-1, keepdims=True)
        for s in ss[1:]:
            mx = jnp.maximum(mx, jnp.max(s, axis=-1, keepdims=True))
        den = None
        acc = None
        for j, s in enumerate(ss):
            e = jnp.exp(s - mx)
            d = jnp.sum(e, axis=-1, keepdims=True)
            a = _dot(e.astype(BF16), kv[2 * j + 1][:, vsl].astype(BF16))
            den = d if den is None else den + d
            acc = a if acc is None else acc + a
        heads.append(acc / den)
    if is_mla:
        o_ref[:, 0:LANES] = jnp.where(lo, heads[0], heads[1]).astype(o_ref.dtype)
        o_ref[:, LANES:] = jnp.where(lo, heads[2], heads[3]).astype(o_ref.dtype)
    else:
        half = LANES // 2
        o_ref[:, 0:LANES] = jnp.where(lo, heads[0], pltpu.roll(heads[1], half, 1)).astype(o_ref.dtype)
        o_ref[:, LANES:] = jnp.where(lo, pltpu.roll(heads[2], half, 1), heads[3]).astype(o_ref.dtype)


def _attention(is_mla, scale, q, segs):
    b, tq_all, wq = q.shape
    tq = min(TQ, tq_all)
    in_specs = [pl.BlockSpec((None, tq, wq), lambda i, j: (i, j, 0))]
    args = [q]
    for k, v in segs:
        in_specs.append(pl.BlockSpec((None,) + k.shape[1:], lambda i, j: (i, 0, 0)))
        in_specs.append(pl.BlockSpec((None,) + v.shape[1:], lambda i, j: (i, 0, 0)))
        args += [k, v]
    return pl.pallas_call(
        functools.partial(_attn_kernel, is_mla, len(segs), scale),
        grid=(b, tq_all // tq),
        in_specs=in_specs,
        out_specs=pl.BlockSpec((None, tq, 2 * LANES), lambda i, j: (i, j, 0)),
        out_shape=jax.ShapeDtypeStruct((b, tq_all, 2 * LANES), BF16),
        compiler_params=_cparams(("parallel", "parallel")),
        name=("attn_mla" if is_mla else "attn_gqa") + ("_lat" if len(segs) > 1 else "_ctx"),
    )(*args)


def _log_sigmoid(x):
    return jnp.minimum(x, 0.0) - jnp.log(1.0 + jnp.exp(-jnp.abs(x)))


def _mlstm_kernel(has_init, nc, u_ref, v_ref, o_ref, gt_ref, hm_ref, cw_ref, cb_ref, wq_ref, wk_ref,
                  go_ref, *rest):
    if has_init:
        c0_ref, n0_ref, m0_ref, out_ref, q_s, k_s, hf_s, hb_s, bcf_s, af_s, bcb_s, ab_s = rest
    else:
        out_ref, cfin_ref, sfin_ref, q_s, k_s, hf_s, hb_s, bcf_s, af_s, bcb_s, ab_s = rest
    t = nc * ML_CHUNK
    sq = (ML_CHUNK, ML_CHUNK)

    u = u_ref[...]
    rows = lax.broadcasted_iota(jnp.int32, (t, LANES), 0)
    up = jnp.where(rows == 0, 0.0, pltpu.roll(u, 1, 0))
    un = jnp.where(rows == t - 1, 0.0, pltpu.roll(u, t - 1, 0))
    uc = _silu(cw_ref[0:1, :] * up + cw_ref[1:2, :] * u + cw_ref[2:3, :] * un + cb_ref[...]).astype(BF16)
    q_s[...] = _dot(uc, wq_ref[...])
    k_s[...] = _dot(uc, wk_ref[...]) * ML_K_SCALE

    r_i = lax.broadcasted_iota(jnp.int32, sq, 0)
    c_i = lax.broadcasted_iota(jnp.int32, sq, 1)
    lower = c_i <= r_i
    upper = c_i >= r_i
    tri_lo = jnp.where(lower, 1.0, 0.0).astype(BF16)
    tri_up = jnp.where(upper, 1.0, 0.0).astype(BF16)
    m_if, m_ff, m_ib, m_fb = (hm_ref[k:k + 1, :] for k in range(4))

    def pick(x, mask):
        return jnp.broadcast_to(jnp.sum(x * mask, axis=1, keepdims=True), sq)

    for c in range(nc):
        sl = slice(c * ML_CHUNK, (c + 1) * ML_CHUNK)
        g = gt_ref[sl, :]
        lf = _log_sigmoid(g)
        h1, h2, h3 = _split3(lf)
        cs_f = _dot(tri_lo, h1) + _dot(tri_lo, h2) + _dot(tri_lo, h3)
        cs_b = _dot(tri_up, h1) + _dot(tri_up, h2) + _dot(tri_up, h3)
        bc_f = pick(cs_f, m_ff)
        bc_b = pick(cs_b, m_fb)
        bcf_s[sl, :] = bc_f
        bcb_s[sl, :] = bc_b
        af_s[sl, :] = pick(g, m_if) - bc_f
        ab_s[sl, :] = pick(g, m_ib) - bc_b

    def chunk(r0, bc_s, a_s, tri, last, cm, nm, mm):
        sl = pl.ds(r0, ML_CHUNK)
        q = q_s[sl, :]
        k = k_s[sl, :]
        v = v_ref[sl, :]
        bc = bc_s[sl, :]
        a = a_s[sl, :]
        g = bc + mm
        dlog = jnp.where(tri, bc + a.T, -jnp.inf)
        m_t = jnp.maximum(g, jnp.max(dlog, axis=1, keepdims=True))
        w = jnp.exp(dlog - m_t)
        inter = jnp.exp(g - m_t)
        qb = q.astype(BF16)
        s = _dot_nt(qb, k.astype(BF16)) * w
        num = _dot(s.astype(BF16), v) + inter * _dot(qb, cm.astype(BF16))
        den = jnp.sum(s, axis=1, keepdims=True) + inter * jnp.sum(q * nm, axis=1, keepdims=True)
        hc = num / jnp.maximum(jnp.abs(den), jnp.exp(-m_t))
        b_last = jnp.broadcast_to(bc[last:last + 1, :], sq)
        m_new = jnp.broadcast_to(m_t[last:last + 1, :], sq)
        kw = k * jnp.exp(b_last + a - m_new)
        decay = jnp.exp(b_last + mm - m_new)
        c_new = decay * cm + _dot_tn(kw.astype(BF16), v)
        n_new = decay[0:1, :] * nm + jnp.sum(kw, axis=0, keepdims=True)
        return hc, c_new, n_new, m_new

    def step(i, carry):
        cf, nf, mf, cb, nb, mb = carry
        rf = i * ML_CHUNK
        rb = (nc - 1 - i) * ML_CHUNK
        if not isinstance(i, int):
            rf = pl.multiple_of(rf, ML_CHUNK)
            rb = pl.multiple_of(rb, ML_CHUNK)
        hf, cf, nf, mf = chunk(rf, bcf_s, af_s, lower, ML_CHUNK - 1, cf, nf, mf)
        hb, cb, nb, mb = chunk(rb, bcb_s, ab_s, upper, 0, cb, nb, mb)
        hf_s[pl.ds(rf, ML_CHUNK), :] = hf
        hb_s[pl.ds(rb, ML_CHUNK), :] = hb
        return cf, nf, mf, cb, nb, mb

    if has_init:
        init = (c0_ref[0], n0_ref[0:1, :], jnp.broadcast_to(m0_ref[0:1, :], sq),
                c0_ref[1], n0_ref[1:2, :], jnp.broadcast_to(m0_ref[1:2, :], sq))
    else:
        init = (jnp.zeros(sq, F32), jnp.zeros((1, LANES), F32), jnp.zeros(sq, F32),
                jnp.zeros(sq, F32), jnp.zeros((1, LANES), F32), jnp.zeros(sq, F32))
    if nc <= 2:
        carry = init
        for i in range(nc):
            carry = step(i, carry)
    else:
        carry = lax.fori_loop(0, nc, step, init)

    hn = _rms(hf_s[...] + hb_s[...], go_ref[...])
    out_ref[...] = (hn * jax.nn.sigmoid(o_ref[...])).astype(out_ref.dtype)

    if not has_init:
        cf, nf, mf, cb, nb, mb = carry
        cfin_ref[0] = cf
        cfin_ref[1] = cb
        sfin_ref[...] = jnp.concatenate(
            [nf, nb, mf[0:1, :], mb[0:1, :], jnp.zeros((4, LANES), F32)], axis=0)


def _mlstm(u, v_ml, o_ml, gates, wl, init):
    b, t, _ = u.shape
    nc = t // ML_CHUNK
    has_init = init is not None
    head_blk = lambda i, h: (i, 0, h)
    in_specs = [
        pl.BlockSpec((None, t, LANES), head_blk),
        pl.BlockSpec((None, t, LANES), head_blk),
        pl.BlockSpec((None, t, LANES), head_blk),
        pl.BlockSpec((None, t, LANES), lambda i, h: (i, 0, 0)),
        pl.BlockSpec((None, 8, LANES), lambda i, h: (h, 0, 0)),
        pl.BlockSpec((3, LANES), lambda i, h: (0, h)),
        pl.BlockSpec((1, LANES), lambda i, h: (0, h)),
        pl.BlockSpec((None, ML_DH, ML_DH), lambda i, h: (h, 0, 0)),
        pl.BlockSpec((None, ML_DH, ML_DH), lambda i, h: (h, 0, 0)),
        pl.BlockSpec((1, LANES), lambda i, h: (0, h)),
    ]
    args = [u, v_ml, o_ml, gates, wl["hmask"], wl["w_ml_conv"], wl["b_ml_conv"], wl["w_ml_q"], wl["w_ml_k"],
            wl["g_ml_out"]]
    out_specs = [pl.BlockSpec((None, t, LANES), head_blk)]
    out_shape = [jax.ShapeDtypeStruct((b, t, ML_WIDTH), BF16)]
    if has_init:
        c0, n0, m0 = init
        in_specs += [
            pl.BlockSpec((None, 2, None, ML_DH, ML_DH), lambda i, h: (i, 0, h, 0, 0)),
            pl.BlockSpec((None, None, 8, LANES), lambda i, h: (i, h, 0, 0)),
            pl.BlockSpec((None, None, 8, LANES), lambda i, h: (i, h, 0, 0)),
        ]
        args += [c0, n0, m0]
    else:
        out_specs += [
            pl.BlockSpec((None, 2, None, ML_DH, ML_DH), lambda i, h: (i, 0, h, 0, 0)),
            pl.BlockSpec((None, None, 8, LANES), lambda i, h: (i, h, 0, 0)),
        ]
        out_shape += [
            jax.ShapeDtypeStruct((b, 2, ML_HEADS, ML_DH, ML_DH), F32),
            jax.ShapeDtypeStruct((b, ML_HEADS, 8, LANES), F32),
        ]
    return pl.pallas_call(
        functools.partial(_mlstm_kernel, has_init, nc),
        grid=(b, ML_HEADS),
        in_specs=in_specs,
        out_specs=out_specs,
        out_shape=out_shape,
        scratch_shapes=[pltpu.VMEM((t, LANES), F32)] * 8,
        compiler_params=_cparams(("parallel", "parallel")),
        name="mlstm_lat" if has_init else "mlstm_ctx",
    )(*args)


def _out_proj_kernel(x_ref, mod_ref, oa_ref, ob_ref, oc_ref, w_ref, y_ref):
    a_w = MLA_HEADS * MLA_V
    mix = (_dot(oa_ref[...], w_ref[0:a_w, :]) + _dot(ob_ref[...], w_ref[a_w:a_w + ML_WIDTH, :])
           + _dot(oc_ref[...], w_ref[a_w + ML_WIDTH:, :]))
    y_ref[...] = x_ref[...] + mod_ref[2:3, :] * mix


def _out_proj(x2, mod, seq, is_ctx, o_a, o_b, o_c, w_out):
    m = x2.shape[0]
    tm = TM
    mod_map = (lambda i: (0, 0, 0)) if is_ctx else (lambda i: (1 + (i * tm) // seq, 0, 0))
    row = lambda i: (i, 0)
    return pl.pallas_call(
        _out_proj_kernel,
        grid=(m // tm,),
        in_specs=[
            pl.BlockSpec((tm, D_MODEL), row),
            pl.BlockSpec((None, 6, D_MODEL), mod_map),
            pl.BlockSpec((tm, o_a.shape[1]), row),
            pl.BlockSpec((tm, o_b.shape[1]), row),
            pl.BlockSpec((tm, o_c.shape[1]), row),
            pl.BlockSpec(w_out.shape, lambda i: (0, 0)),
        ],
        out_specs=pl.BlockSpec((tm, D_MODEL), row),
        out_shape=jax.ShapeDtypeStruct((m, D_MODEL), F32),
        compiler_params=_cparams(("parallel",)),
        name="out_proj",
    )(x2, mod, o_a, o_b, o_c, w_out)


def _ffn_kernel(seq, tm, final, x_ref, xp_ref, xn_ref, mod_ref, g2_ref, wup_ref, cw_ref, cb_ref, wdn_ref,
                gf_ref, y_ref, yext_s, gext_s):
    i = pl.program_id(0)
    g2 = g2_ref[...]
    shift = mod_ref[3:4, :]
    scale = mod_ref[4:5, :]
    x = x_ref[...]
    keep_p = jnp.where((i * tm) % seq == 0, 0.0, 1.0)
    keep_n = jnp.where(((i + 1) * tm) % seq == 0, 0.0, 1.0)
    yext_s[0:HALO, :] = (_adaln(xp_ref[...], g2, shift, scale) * keep_p).astype(BF16)
    yext_s[HALO:HALO + tm, :] = _adaln(x, g2, shift, scale).astype(BF16)
    yext_s[HALO + tm:, :] = (_adaln(xn_ref[...], g2, shift, scale) * keep_n).astype(BF16)

    acc = jnp.zeros((tm, D_MODEL), F32)
    for c in range(D_FF // FF_CHUNK):
        c0 = c * FF_CHUNK
        a = _dot(yext_s[HALO:HALO + tm, :], wup_ref[:, c0:c0 + FF_CHUNK])
        gext_s[...] = _dot(yext_s[...], wup_ref[:, D_FF + c0:D_FF + c0 + FF_CHUNK])
        g = (cw_ref[0:1, c0:c0 + FF_CHUNK] * gext_s[HALO - 1:HALO - 1 + tm, :]
             + cw_ref[1:2, c0:c0 + FF_CHUNK] * gext_s[HALO:HALO + tm, :]
             + cw_ref[2:3, c0:c0 + FF_CHUNK] * gext_s[HALO + 1:HALO + 1 + tm, :]
             + cb_ref[:, c0:c0 + FF_CHUNK])
        acc = acc + _dot((_silu(g) * a).astype(BF16), wdn_ref[c0:c0 + FF_CHUNK, :])
    y = x + mod_ref[5:6, :] * acc
    if final:
        y = _rms(y, gf_ref[...])
    y_ref[...] = y


def _ffn(x2, mod, seq, is_ctx, wl, g_final, final):
    m = x2.shape[0]
    tm = min(TM, seq)
    nb = tm // HALO
    last_blk = m // HALO - 1
    mod_map = (lambda i: (0, 0, 0)) if is_ctx else (lambda i: (1 + (i * tm) // seq, 0, 0))
    const = lambda i: (0, 0)
    row = lambda i: (i, 0)
    return pl.pallas_call(
        functools.partial(_ffn_kernel, seq, tm, final),
        grid=(m // tm,),
        in_specs=[
            pl.BlockSpec((tm, D_MODEL), row),
            pl.BlockSpec((HALO, D_MODEL), lambda i: (jnp.maximum(i * nb - 1, 0), 0)),
            pl.BlockSpec((HALO, D_MODEL), lambda i: (jnp.minimum((i + 1) * nb, last_blk), 0)),
            pl.BlockSpec((None, 6, D_MODEL), mod_map),
            pl.BlockSpec((1, D_MODEL), const),
            pl.BlockSpec((D_MODEL, 2 * D_FF), const),
            pl.BlockSpec((3, D_FF), const),
            pl.BlockSpec((1, D_FF), const),
            pl.BlockSpec((D_FF, D_MODEL), const),
            pl.BlockSpec((1, D_MODEL), const),
        ],
        out_specs=pl.BlockSpec((tm, D_MODEL), row),
        out_shape=jax.ShapeDtypeStruct((m, D_MODEL), F32),
        scratch_shapes=[pltpu.VMEM((tm + 2 * HALO, D_MODEL), BF16),
                        pltpu.VMEM((tm + 2 * HALO, FF_CHUNK), F32)],
        compiler_params=_cparams(("parallel",)),
        name="ffn_ctx" if is_ctx else "ffn_lat",
    )(x2, x2, x2, mod, wl["g_norm2"], wl["w_ff_up"], wl["w_ff_conv"], wl["b_ff_conv"], wl["w_ff_down"],
      g_final)


def _rope_tables(t):
    pos = jnp.arange(t, dtype=jnp.int32)
    rows = (pos // GRID_W).astype(F32)
    cols = (pos % GRID_W).astype(F32)

    def group(p, d):
        inv = ROPE_THETA ** (-jnp.arange(0, d, 2, dtype=F32) / d)
        ang = p[:, None] * inv[None, :]
        cs, sn, z = jnp.cos(ang), jnp.sin(ang), jnp.zeros_like(ang)
        return (jnp.concatenate([cs, cs], 1), jnp.concatenate([-sn, z], 1), jnp.concatenate([z, sn], 1))

    gr, gc = group(rows, MLA_ROPE // 2), group(cols, MLA_ROPE // 2)
    ones = jnp.ones((t, MLA_NOPE), F32)
    zeros = jnp.zeros((t, MLA_NOPE), F32)
    pad1 = jnp.ones((t, LANES - MLA_NOPE - MLA_ROPE), F32)
    pad0 = jnp.zeros((t, LANES - MLA_NOPE - MLA_ROPE), F32)
    mla = (jnp.concatenate([ones, gr[0], gc[0], pad1], 1),
           jnp.concatenate([zeros, gr[1], gc[1], pad0], 1),
           jnp.concatenate([zeros, gr[2], gc[2], pad0], 1))
    gr, gc = group(rows, GQA_DH // 2), group(cols, GQA_DH // 2)
    gqa = tuple(jnp.concatenate([gr[k], gc[k], gr[k], gc[k]], 1) for k in range(3))
    return mla + gqa


def _prep_layer(l, p):
    w_in = p["w_in"][l]
    pad_cols = lambda a, lo, hi: jnp.pad(a, ((0, 0), (lo, hi)))
    qg = w_in[:, 1968:2224].reshape(D_MODEL, GQA_HEADS, GQA_DH)[:, jnp.array([0, 2, 1, 3])].reshape(D_MODEL, -1)
    w_in_p = jnp.concatenate([
        w_in[:, 0:384],
        pad_cols(w_in[:, 384:416], MLA_NOPE, LANES - MLA_NOPE - MLA_ROPE),
        w_in[:, 416:1952],
        pad_cols(w_in[:, 1952:1968], 0, LANES - 4 * ML_HEADS),
        qg,
        w_in[:, 2224:2480],
    ], axis=1).astype(BF16)
    w_uq = p["w_mla_uq"][l].reshape(MLA_Q_RANK, MLA_HEADS, MLA_NOPE + MLA_ROPE)
    w_uq = jnp.pad(w_uq, ((0, 0), (0, 0), (0, LANES - MLA_NOPE - MLA_ROPE))).reshape(MLA_Q_RANK, -1).astype(BF16)
    w_ukv = p["w_mla_ukv"][l].reshape(MLA_KV_RANK, MLA_HEADS, MLA_NOPE + MLA_V)
    w_uk = jnp.pad(w_ukv[:, :, :MLA_NOPE], ((0, 0), (0, 0), (0, LANES - MLA_NOPE))).reshape(MLA_KV_RANK, -1)
    w_uv = w_ukv[:, :, MLA_NOPE:].reshape(MLA_KV_RANK, -1)
    w_ukv_p = jnp.concatenate([w_uk, w_uv], axis=1).astype(BF16)
    grp = jnp.arange(2 * LANES) // GQA_DH
    heads = jnp.arange(ML_HEADS)
    hmask = jnp.zeros((ML_HEADS, 8, LANES), F32)
    for k in range(4):
        hmask = hmask.at[heads, k, k * ML_HEADS + heads].set(1.0)
    return {
        "g_norm1": p["g_norm1"][l][None, :],
        "g_norm2": p["g_norm2"][l][None, :],
        "w_in": w_in_p,
        "g_mla_q": p["g_mla_q"][l][None, :],
        "w_uq": w_uq,
        "g_mla_kv": p["g_mla_kv"][l][None, :],
        "w_ukv": w_ukv_p,
        "b_gates": jnp.pad(p["b_ml_gates"][l], (0, LANES - 4 * ML_HEADS))[None, :],
        "g_gqa_q": jnp.tile(p["g_gqa_q"][l], GQA_HEADS)[None, :],
        "g_gqa_k": jnp.tile(p["g_gqa_k"][l], GQA_KV_HEADS)[None, :],
        "gsum": (grp[:, None] == grp[None, :]).astype(BF16),
        "hmask": hmask,
        "w_ml_conv": p["w_ml_conv"][l],
        "b_ml_conv": p["b_ml_conv"][l][None, :],
        "w_ml_q": p["w_ml_q"][l].astype(BF16),
        "w_ml_k": p["w_ml_k"][l].astype(BF16),
        "g_ml_out": p["g_ml_out"][l][None, :],
        "w_out": p["w_out"][l].astype(BF16),
        "w_ff_up": p["w_ff_up"][l].astype(BF16),
        "w_ff_conv": p["w_ff_conv"][l],
        "b_ff_conv": p["b_ff_conv"][l][None, :],
        "w_ff_down": p["w_ff_down"][l].astype(BF16),
    }


def _layer(x2, mod, seq, is_ctx, wl, tabs, cache, g_final, final):
    m = x2.shape[0]
    b = m // seq
    r3 = lambda a: a.reshape(b, seq, a.shape[-1])
    outs = _in_proj(x2, mod, seq, is_ctx, wl, tabs)
    q_m, k_m, v_m, u, v_ml, o_ml, gates, q_g, k_g, v_g = outs[:10]
    if is_ctx:
        segs_a = [(r3(k_m), r3(v_m))]
        segs_c = [(r3(k_g), r3(v_g))]
        init = None
    else:
        ckv_c, kr_c, kg_c, vg_c, c0, n0, m0 = cache
        tc = ckv_c.shape[1]
        kr_pad = jnp.pad(kr_c, ((0, 0), (0, 0), (MLA_NOPE, LANES - MLA_NOPE - MLA_ROPE)))
        kc, vc = _cache_kv(ckv_c.reshape(b * tc, -1), kr_pad.reshape(b * tc, LANES), wl["w_ukv"])
        segs_a = [(r3(k_m), r3(v_m)), (kc.reshape(b, tc, -1), vc.reshape(b, tc, -1))]
        segs_c = [(r3(k_g), r3(v_g)), (kg_c.reshape(b, tc, -1), vg_c.reshape(b, tc, -1))]
        n0p = jnp.pad(jnp.swapaxes(n0, 1, 2), ((0, 0), (0, 0), (0, 6), (0, 0)))
        m0p = jnp.pad(jnp.broadcast_to(jnp.swapaxes(m0, 1, 2)[..., None], m0.shape[:1] + (ML_HEADS, 2, LANES)),
                      ((0, 0), (0, 0), (0, 6), (0, 0)))
        init = (c0, n0p, m0p)
    o_a = _attention(True, MLA_SCALE, r3(q_m), segs_a)
    o_c = _attention(False, GQA_SCALE, r3(q_g), segs_c)
    ml = _mlstm(r3(u), r3(v_ml), r3(o_ml), r3(gates), wl, init)
    x2 = _out_proj(x2, mod, seq, is_ctx, o_a.reshape(m, -1), ml[0].reshape(m, -1), o_c.reshape(m, -1), wl["w_out"])
    x2 = _ffn(x2, mod, seq, is_ctx, wl, g_final, final)
    state = None
    if is_ctx:
        ckv_n, kr_raw = outs[10:12]
        sfin = ml[2]
        state = (
            r3(ckv_n),
            r3(kr_raw)[:, :, MLA_NOPE:MLA_NOPE + MLA_ROPE],
            k_g.reshape(b, seq, GQA_KV_HEADS, GQA_DH),
            v_g.reshape(b, seq, GQA_KV_HEADS, GQA_DH),
            ml[1],
            jnp.swapaxes(sfin[:, :, 0:2, :], 1, 2),
            jnp.swapaxes(sfin[:, :, 2:4, 0], 1, 2),
        )
    return x2, state


def kernel(x_prompt, x_sample, cache_mla_ckv, cache_mla_krope, cache_gqa_k, cache_gqa_v, state_mlstm_C, state_mlstm_n, state_mlstm_m, c, c_ctx, w_ada, b_ada, g_norm1, g_norm2, w_in, g_mla_q, w_mla_uq, g_mla_kv, w_mla_ukv, w_ml_conv, b_ml_conv, w_ml_q, w_ml_k, b_ml_gates, g_ml_out, g_gqa_q, g_gqa_k, w_out, w_ff_up, w_ff_conv, b_ff_conv, w_ff_down, g_final):
    params = {
        "g_norm1": g_norm1, "g_norm2": g_norm2, "w_in": w_in, "g_mla_q": g_mla_q, "w_mla_uq": w_mla_uq,
        "g_mla_kv": g_mla_kv, "w_mla_ukv": w_mla_ukv, "w_ml_conv": w_ml_conv, "b_ml_conv": b_ml_conv,
        "w_ml_q": w_ml_q, "w_ml_k": w_ml_k, "b_ml_gates": b_ml_gates, "g_ml_out": g_ml_out,
        "g_gqa_q": g_gqa_q, "g_gqa_k": g_gqa_k, "w_out": w_out, "w_ff_up": w_ff_up, "w_ff_conv": w_ff_conv,
        "b_ff_conv": b_ff_conv, "w_ff_down": w_ff_down,
    }
    depth = w_in.shape[0]
    bp, sp, _ = x_prompt.shape
    bs, ss, _ = x_sample.shape
    assert bs + 1 <= 16 and sp % ML_CHUNK == 0 and ss % TM == 0 and (bp * sp) % TM == 0

    cvec = jnp.concatenate([c_ctx[None, :], c, jnp.zeros((16 - 1 - bs, D_MODEL), F32)], axis=0)
    mod_all = _modulation(cvec, w_ada, b_ada).reshape(depth, 16, 6, D_MODEL)
    tabs = _rope_tables(ss)
    gf = g_final[None, :]

    xp = x_prompt.reshape(bp * sp, D_MODEL)
    xs = x_sample.reshape(bs * ss, D_MODEL)
    states = []
    for l in range(depth):
        wl = _prep_layer(l, params)
        final = l == depth - 1
        xp, st = _layer(xp, mod_all[l], sp, True, wl, None, None, gf, final)
        states.append(st)
        cache = (cache_mla_ckv[:, l], cache_mla_krope[:, l], cache_gqa_k[:, l], cache_gqa_v[:, l],
                 state_mlstm_C[:, l], state_mlstm_n[:, l], state_mlstm_m[:, l])
        xs, _ = _layer(xs, mod_all[l], ss, False, wl, tabs, cache, gf, final)
    new_state = tuple(jnp.stack([st[k] for st in states], axis=1) for k in range(7))
    return (xp.reshape(bp, sp, D_MODEL), xs.reshape(bs, ss, D_MODEL)) + new_state
```

```python
import functools

import jax
import jax.numpy as jnp
from jax import lax
from jax.experimental import pallas as pl
from jax.experimental.pallas import tpu as pltpu

F32 = jnp.float32
BF16 = jnp.bfloat16

D_MODEL = 1024
GRID_W = 64
ROPE_THETA = 10000.0
EPS = 1e-6
MLA_HEADS = 4
MLA_NOPE = 64
MLA_ROPE = 32
MLA_V = 64
MLA_Q_RANK = 256
MLA_KV_RANK = 128
ML_HEADS = 4
ML_DH = 128
ML_WIDTH = ML_HEADS * ML_DH
ML_CHUNK = 128
GQA_HEADS = 4
GQA_KV_HEADS = 2
GQA_DH = 64
D_FF = 2816
MLA_SCALE = (MLA_NOPE + MLA_ROPE) ** -0.5
GQA_SCALE = GQA_DH ** -0.5
ML_K_SCALE = ML_DH ** -0.5
LOG2E = 1.4426950408889634

LANES = 128
HALO = 16
VMEM_LIMIT = 52 * 1024 * 1024

C_CQ, C_CKV, C_KR, C_U, C_V, C_O, C_G, C_QG, C_KG, C_VG, IN_COLS_P = (
    0, 256, 384, 512, 1024, 1536, 2048, 2176, 2432, 2560, 2688)
FF_CHUNK = 256
TM = 512
TQ = 512


def _cparams(sem):
    return pltpu.CompilerParams(dimension_semantics=sem, vmem_limit_bytes=VMEM_LIMIT)


def _dot(a, b):
    return jnp.dot(a, b, preferred_element_type=F32)


def _dot_nt(a, b):
    return lax.dot_general(a, b, (((1,), (1,)), ((), ())), preferred_element_type=F32)


def _dot_tn(a, b):
    return lax.dot_general(a, b, (((0,), (0,)), ((), ())), preferred_element_type=F32)


def _rms(x, g):
    return (x * lax.rsqrt(jnp.mean(x * x, axis=-1, keepdims=True) + EPS)) * g


def _silu(x):
    return x * jax.nn.sigmoid(x)


def _adaln(x, g, shift, scale):
    return _rms(x, g) * (1.0 + scale) + shift


def _split2(x):
    hi = x.astype(BF16)
    lo = (x - hi.astype(F32)).astype(BF16)
    return hi, lo


def _split3(x):
    h1 = x.astype(BF16)
    r1 = x - h1.astype(F32)
    h2 = r1.astype(BF16)
    h3 = (r1 - h2.astype(F32)).astype(BF16)
    return h1, h2, h3


def _rope(x, cos, sa, sb, half):
    w = x.shape[-1]
    return x * cos + pltpu.roll(x, w - half, 1) * sa + pltpu.roll(x, half, 1) * sb


def _group_rms(x, gsum, g):
    hi, lo = _split2(x * x)
    ss = _dot(hi, gsum) + _dot(lo, gsum)
    return (x * lax.rsqrt(ss * (1.0 / GQA_DH) + EPS)) * g


def _mod_kernel(c_ref, w_ref, b_ref, o_ref):
    a = _silu(c_ref[...]).astype(BF16)
    o_ref[...] = _dot(a, w_ref[...].astype(BF16)) + b_ref[...]


def _modulation(cvec, w_ada, b_ada):
    nl = w_ada.shape[0]
    tn = 1536
    return pl.pallas_call(
        _mod_kernel,
        grid=(nl, 6 * D_MODEL // tn),
        in_specs=[
            pl.BlockSpec((16, D_MODEL), lambda l, j: (0, 0)),
            pl.BlockSpec((None, D_MODEL, tn), lambda l, j: (l, 0, j)),
            pl.BlockSpec((None, 1, tn), lambda l, j: (l, 0, j)),
        ],
        out_specs=pl.BlockSpec((None, 16, tn), lambda l, j: (l, 0, j)),
        out_shape=jax.ShapeDtypeStruct((nl, 16, 6 * D_MODEL), F32),
        compiler_params=_cparams(("arbitrary", "arbitrary")),
        name="modulation",
    )(cvec, w_ada, b_ada.reshape(nl, 1, 6 * D_MODEL))


def _in_proj_kernel(is_ctx, x_ref, mod_ref, g1_ref, win_ref, gq_ref, wuq_ref, gkv_ref, wukv_ref,
                    bg_ref, ggq_ref, ggk_ref, gsum_ref, *rest):
    if is_ctx:
        (qm_ref, km_ref, vm_ref, u_ref, vml_ref, oml_ref, gt_ref, qg_ref, kg_ref, vg_ref,
         ckv_ref, kr_ref) = rest
    else:
        (cm_ref, sam_ref, sbm_ref, cg_ref, sag_ref, sbg_ref,
         qm_ref, km_ref, vm_ref, u_ref, vml_ref, oml_ref, gt_ref, qg_ref, kg_ref, vg_ref) = rest

    y = _adaln(x_ref[...], g1_ref[...], mod_ref[0:1, :], mod_ref[1:2, :]).astype(BF16)

    def proj(c0, width):
        return _dot(y, win_ref[:, c0:c0 + width])

    cqn = _rms(proj(C_CQ, MLA_Q_RANK), gq_ref[...]).astype(BF16)
    qz = _dot(cqn, wuq_ref[...])
    for h in range(MLA_HEADS):
        blk = qz[:, LANES * h:LANES * (h + 1)]
        if not is_ctx:
            blk = _rope(blk, cm_ref[...], sam_ref[...], sbm_ref[...], 8)
        qm_ref[:, LANES * h:LANES * (h + 1)] = (blk * (MLA_SCALE * LOG2E)).astype(qm_ref.dtype)

    ckvn = _rms(proj(C_CKV, MLA_KV_RANK), gkv_ref[...])
    kr = proj(C_KR, LANES)
    if is_ctx:
        ckv_ref[...] = ckvn
        kr_ref[...] = kr
    else:
        kr = _rope(kr, cm_ref[...], sam_ref[...], sbm_ref[...], 8)
    kvz = _dot(ckvn.astype(BF16), wukv_ref[...])
    for h in range(MLA_HEADS):
        km_ref[:, LANES * h:LANES * (h + 1)] = (kvz[:, LANES * h:LANES * (h + 1)] + kr).astype(km_ref.dtype)
    vm_ref[...] = kvz[:, MLA_HEADS * LANES:].astype(vm_ref.dtype)

    u_ref[...] = proj(C_U, ML_WIDTH).astype(u_ref.dtype)
    vml_ref[...] = proj(C_V, ML_WIDTH).astype(vml_ref.dtype)
    oml_ref[...] = proj(C_O, ML_WIDTH).astype(oml_ref.dtype)
    gt_ref[...] = proj(C_G, LANES) + bg_ref[...]

    qg = _group_rms(proj(C_QG, GQA_HEADS * GQA_DH), gsum_ref[...], ggq_ref[...])
    kg = _group_rms(proj(C_KG, LANES), gsum_ref[0:LANES, 0:LANES], ggk_ref[...])
    if not is_ctx:
        qg = jnp.concatenate(
            [_rope(qg[:, 0:LANES], cg_ref[...], sag_ref[...], sbg_ref[...], 16),
             _rope(qg[:, LANES:], cg_ref[...], sag_ref[...], sbg_ref[...], 16)], axis=1)
        kg = _rope(kg, cg_ref[...], sag_ref[...], sbg_ref[...], 16)
    qg_ref[...] = (qg * (GQA_SCALE * LOG2E)).astype(qg_ref.dtype)
    kg_ref[...] = kg.astype(kg_ref.dtype)
    vg_ref[...] = proj(C_VG, LANES).astype(vg_ref.dtype)


def _in_proj(x2, mod, seq, is_ctx, wl, tabs):
    m = x2.shape[0]
    tm = TM
    nt = m // tm
    if is_ctx:
        mod_map = lambda i: (0, 0, 0)
    else:
        mod_map = lambda i: (1 + (i * tm) // seq, 0, 0)
    const = lambda i: (0, 0)
    row = lambda i: (i, 0)
    in_specs = [
        pl.BlockSpec((tm, D_MODEL), row),
        pl.BlockSpec((None, 6, D_MODEL), mod_map),
        pl.BlockSpec((1, D_MODEL), const),
        pl.BlockSpec((D_MODEL, IN_COLS_P), const),
        pl.BlockSpec((1, MLA_Q_RANK), const),
        pl.BlockSpec((MLA_Q_RANK, MLA_HEADS * LANES), const),
        pl.BlockSpec((1, MLA_KV_RANK), const),
        pl.BlockSpec((MLA_KV_RANK, MLA_HEADS * LANES + MLA_HEADS * MLA_V), const),
        pl.BlockSpec((1, LANES), const),
        pl.BlockSpec((1, GQA_HEADS * GQA_DH), const),
        pl.BlockSpec((1, LANES), const),
        pl.BlockSpec((2 * LANES, 2 * LANES), const),
    ]
    args = [x2, mod, wl["g_norm1"], wl["w_in"], wl["g_mla_q"], wl["w_uq"], wl["g_mla_kv"], wl["w_ukv"],
            wl["b_gates"], wl["g_gqa_q"], wl["g_gqa_k"], wl["gsum"]]
    if not is_ctx:
        tpos = seq // tm
        in_specs += [pl.BlockSpec((tm, LANES), lambda i: (i % tpos, 0))] * 6
        args += list(tabs)
    act = F32 if is_ctx else BF16
    outs = [
        (MLA_HEADS * LANES, BF16),
        (MLA_HEADS * LANES, BF16),
        (MLA_HEADS * MLA_V, BF16),
        (ML_WIDTH, F32),
        (ML_WIDTH, BF16),
        (ML_WIDTH, F32),
        (LANES, F32),
        (GQA_HEADS * GQA_DH, BF16),
        (LANES, act),
        (LANES, act),
    ]
    if is_ctx:
        outs += [(MLA_KV_RANK, F32), (LANES, F32)]
    return pl.pallas_call(
        functools.partial(_in_proj_kernel, is_ctx),
        grid=(nt,),
        in_specs=in_specs,
        out_specs=[pl.BlockSpec((tm, w), row) for w, _ in outs],
        out_shape=[jax.ShapeDtypeStruct((m, w), dt) for w, dt in outs],
        compiler_params=_cparams(("parallel",)),
        name="in_proj_ctx" if is_ctx else "in_proj_lat",
    )(*args)


def _cache_kv_kernel(ckv_ref, kr_ref, wukv_ref, k_ref, v_ref):
    kvz = _dot(ckv_ref[...].astype(BF16), wukv_ref[...])
    kr = kr_ref[...]
    for h in range(MLA_HEADS):
        k_ref[:, LANES * h:LANES * (h + 1)] = (kvz[:, LANES * h:LANES * (h + 1)] + kr).astype(k_ref.dtype)
    v_ref[...] = kvz[:, MLA_HEADS * LANES:].astype(v_ref.dtype)


def _cache_kv(ckv2, kr2, w_ukv):
    m = ckv2.shape[0]
    tm = 256
    row = lambda i: (i, 0)
    return pl.pallas_call(
        _cache_kv_kernel,
        grid=(m // tm,),
        in_specs=[pl.BlockSpec((tm, MLA_KV_RANK), row), pl.BlockSpec((tm, LANES), row),
                  pl.BlockSpec(w_ukv.shape, lambda i: (0, 0))],
        out_specs=[pl.BlockSpec((tm, MLA_HEADS * LANES), row), pl.BlockSpec((tm, MLA_HEADS * MLA_V), row)],
        out_shape=[jax.ShapeDtypeStruct((m, MLA_HEADS * LANES), BF16),
                   jax.ShapeDtypeStruct((m, MLA_HEADS * MLA_V), BF16)],
        compiler_params=_cparams(("parallel",)),
        name="cache_kv",
    )(ckv2, kr2, w_ukv)


def _attn_kernel(is_mla, nseg, q_ref, *refs):
    kv = refs[:2 * nseg]
    o_ref = refs[2 * nseg]
    tq = q_ref.shape[0]
    lo = lax.broadcasted_iota(jnp.int32, (tq, LANES), 1) < (LANES // 2)
    heads = []
    for h in range(4):
        if is_mla:
            qh = q_ref[:, LANES * h:LANES * (h + 1)]
            ksl = slice(LANES * h, LANES * (h + 1))
            vsl = slice(LANES * (h // 2), LANES * (h // 2 + 1))
        else:
            blk = q_ref[:, LANES * (h % 2):LANES * (h % 2 + 1)].astype(F32)
            qh = jnp.where(lo if h < 2 else jnp.logical_not(lo), blk, 0.0).astype(BF16)
            ksl = slice(0, LANES)
            vsl = slice(0, LANES)
        ss = [_dot_nt(qh, kv[2 * j][:, ksl].astype(BF16)) for j in range(nseg)]
        mx = jnp.max(ss[0], axis=-1, keepdims=True)
        for s in ss[1:]:
            mx = jnp.maximum(mx, jnp.max(s, axis=-1, keepdims=True))
        den = None
        acc = None
        for j, s in enumerate(ss):
            e = jnp.exp2(s - mx)
            d = jnp.sum(e, axis=-1, keepdims=True)
            a = _dot(e.astype(BF16), kv[2 * j + 1][:, vsl].astype(BF16))
            den = d if den is None else den + d
            acc = a if acc is None else acc + a
        heads.append(acc / den)
    if is_mla:
        o_ref[:, 0:LANES] = jnp.where(lo, heads[0], heads[1]).astype(o_ref.dtype)
        o_ref[:, LANES:] = jnp.where(lo, heads[2], heads[3]).astype(o_ref.dtype)
    else:
        half = LANES // 2
        o_ref[:, 0:LANES] = jnp.where(lo, heads[0], pltpu.roll(heads[1], half, 1)).astype(o_ref.dtype)
        o_ref[:, LANES:] = jnp.where(lo, pltpu.roll(heads[2], half, 1), heads[3]).astype(o_ref.dtype)


def _attention(is_mla, q, segs):
    b, tq_all, wq = q.shape
    tq = min(TQ, tq_all)
    in_specs = [pl.BlockSpec((None, tq, wq), lambda i, j: (i, j, 0))]
    args = [q]
    for k, v in segs:
        in_specs.append(pl.BlockSpec((None,) + k.shape[1:], lambda i, j: (i, 0, 0)))
        in_specs.append(pl.BlockSpec((None,) + v.shape[1:], lambda i, j: (i, 0, 0)))
        args += [k, v]
    return pl.pallas_call(
        functools.partial(_attn_kernel, is_mla, len(segs)),
        grid=(b, tq_all // tq),
        in_specs=in_specs,
        out_specs=pl.BlockSpec((None, tq, 2 * LANES), lambda i, j: (i, j, 0)),
        out_shape=jax.ShapeDtypeStruct((b, tq_all, 2 * LANES), BF16),
        compiler_params=_cparams(("parallel", "parallel")),
        name=("attn_mla" if is_mla else "attn_gqa") + ("_lat" if len(segs) > 1 else "_ctx"),
    )(*args)


def _log_sigmoid(x):
    return jnp.minimum(x, 0.0) - jnp.log(1.0 + jnp.exp(-jnp.abs(x)))


def _mlstm_kernel(has_init, nc, u_ref, v_ref, o_ref, gt_ref, hm_ref, cw_ref, cb_ref, wq_ref, wk_ref,
                  go_ref, *rest):
    if has_init:
        c0_ref, n0_ref, m0_ref, out_ref, q_s, k_s, hf_s, hb_s, bcf_s, af_s, bcb_s, ab_s = rest
    else:
        out_ref, cfin_ref, sfin_ref, q_s, k_s, hf_s, hb_s, bcf_s, af_s, bcb_s, ab_s = rest
    t = nc * ML_CHUNK
    sq = (ML_CHUNK, ML_CHUNK)

    u = u_ref[...]
    rows = lax.broadcasted_iota(jnp.int32, (t, LANES), 0)
    up = jnp.where(rows == 0, 0.0, pltpu.roll(u, 1, 0))
    un = jnp.where(rows == t - 1, 0.0, pltpu.roll(u, t - 1, 0))
    uc = _silu(cw_ref[0:1, :] * up + cw_ref[1:2, :] * u + cw_ref[2:3, :] * un + cb_ref[...]).astype(BF16)
    q_s[...] = _dot(uc, wq_ref[...])
    k_s[...] = _dot(uc, wk_ref[...]) * ML_K_SCALE

    r_i = lax.broadcasted_iota(jnp.int32, sq, 0)
    c_i = lax.broadcasted_iota(jnp.int32, sq, 1)
    lower = c_i <= r_i
    upper = c_i >= r_i
    tri_lo = jnp.where(lower, 1.0, 0.0).astype(BF16)
    tri_up = jnp.where(upper, 1.0, 0.0).astype(BF16)
    m_if, m_ff, m_ib, m_fb = (hm_ref[k:k + 1, :] for k in range(4))

    def pick(x, mask):
        return jnp.broadcast_to(jnp.sum(x * mask, axis=1, keepdims=True), sq)

    for c in range(nc):
        sl = slice(c * ML_CHUNK, (c + 1) * ML_CHUNK)
        g = gt_ref[sl, :]
        lf = _log_sigmoid(g)
        h1, h2, h3 = _split3(lf)
        cs_f = _dot(tri_lo, h1) + _dot(tri_lo, h2) + _dot(tri_lo, h3)
        cs_b = _dot(tri_up, h1) + _dot(tri_up, h2) + _dot(tri_up, h3)
        bc_f = pick(cs_f, m_ff)
        bc_b = pick(cs_b, m_fb)
        bcf_s[sl, :] = bc_f
        bcb_s[sl, :] = bc_b
        af_s[sl, :] = pick(g, m_if) - bc_f
        ab_s[sl, :] = pick(g, m_ib) - bc_b

    def chunk(r0, bc_s, a_s, tri, last, cm, nm, mm):
        sl = pl.ds(r0, ML_CHUNK)
        q = q_s[sl, :]
        k = k_s[sl, :]
        v = v_ref[sl, :]
        bc = bc_s[sl, :]
        a = a_s[sl, :]
        g = bc + mm
        dlog = jnp.where(tri, bc + a.T, -jnp.inf)
        m_t = jnp.maximum(g, jnp.max(dlog, axis=1, keepdims=True))
        w = jnp.exp(dlog - m_t)
        inter = jnp.exp(g - m_t)
        qb = q.astype(BF16)
        s = _dot_nt(qb, k.astype(BF16)) * w
        num = _dot(s.astype(BF16), v) + inter * _dot(qb, cm.astype(BF16))
        den = jnp.sum(s, axis=1, keepdims=True) + inter * jnp.sum(q * nm, axis=1, keepdims=True)
        hc = num / jnp.maximum(jnp.abs(den), jnp.exp(-m_t))
        b_last = jnp.broadcast_to(bc[last:last + 1, :], sq)
        m_new = jnp.broadcast_to(m_t[last:last + 1, :], sq)
        kw = k * jnp.exp(b_last + a - m_new)
        decay = jnp.exp(b_last + mm - m_new)
        c_new = decay * cm + _dot_tn(kw.astype(BF16), v)
        n_new = decay[0:1, :] * nm + jnp.sum(kw, axis=0, keepdims=True)
        return hc, c_new, n_new, m_new

    def step(i, carry):
        cf, nf, mf, cb, nb, mb = carry
        rf = i * ML_CHUNK
        rb = (nc - 1 - i) * ML_CHUNK
        if not isinstance(i, int):
            rf = pl.multiple_of(rf, ML_CHUNK)
            rb = pl.multiple_of(rb, ML_CHUNK)
        hf, cf, nf, mf = chunk(rf, bcf_s, af_s, lower, ML_CHUNK - 1, cf, nf, mf)
        hb, cb, nb, mb = chunk(rb, bcb_s, ab_s, upper, 0, cb, nb, mb)
        hf_s[pl.ds(rf, ML_CHUNK), :] = hf
        hb_s[pl.ds(rb, ML_CHUNK), :] = hb
        return cf, nf, mf, cb, nb, mb

    if has_init:
        init = (c0_ref[0], n0_ref[0:1, :], jnp.broadcast_to(m0_ref[0:1, :], sq),
                c0_ref[1], n0_ref[1:2, :], jnp.broadcast_to(m0_ref[1:2, :], sq))
    else:
        init = (jnp.zeros(sq, F32), jnp.zeros((1, LANES), F32), jnp.zeros(sq, F32),
                jnp.zeros(sq, F32), jnp.zeros((1, LANES), F32), jnp.zeros(sq, F32))
    if nc <= 2:
        carry = init
        for i in range(nc):
            carry = step(i, carry)
    else:
        carry = lax.fori_loop(0, nc, step, init)

    hn = _rms(hf_s[...] + hb_s[...], go_ref[...])
    out_ref[...] = (hn * jax.nn.sigmoid(o_ref[...])).astype(out_ref.dtype)

    if not has_init:
        cf, nf, mf, cb, nb, mb = carry
        cfin_ref[0] = cf
        cfin_ref[1] = cb
        sfin_ref[...] = jnp.concatenate(
            [nf, nb, mf[0:1, :], mb[0:1, :], jnp.zeros((4, LANES), F32)], axis=0)


def _mlstm(u, v_ml, o_ml, gates, wl, init):
    b, t, _ = u.shape
    nc = t // ML_CHUNK
    has_init = init is not None
    head_blk = lambda i, h: (i, 0, h)
    in_specs = [
        pl.BlockSpec((None, t, LANES), head_blk),
        pl.BlockSpec((None, t, LANES), head_blk),
        pl.BlockSpec((None, t, LANES), head_blk),
        pl.BlockSpec((None, t, LANES), lambda i, h: (i, 0, 0)),
        pl.BlockSpec((None, 8, LANES), lambda i, h: (h, 0, 0)),
        pl.BlockSpec((3, LANES), lambda i, h: (0, h)),
        pl.BlockSpec((1, LANES), lambda i, h: (0, h)),
        pl.BlockSpec((None, ML_DH, ML_DH), lambda i, h: (h, 0, 0)),
        pl.BlockSpec((None, ML_DH, ML_DH), lambda i, h: (h, 0, 0)),
        pl.BlockSpec((1, LANES), lambda i, h: (0, h)),
    ]
    args = [u, v_ml, o_ml, gates, wl["hmask"], wl["w_ml_conv"], wl["b_ml_conv"], wl["w_ml_q"], wl["w_ml_k"],
            wl["g_ml_out"]]
    out_specs = [pl.BlockSpec((None, t, LANES), head_blk)]
    out_shape = [jax.ShapeDtypeStruct((b, t, ML_WIDTH), BF16)]
    if has_init:
        c0, n0, m0 = init
        in_specs += [
            pl.BlockSpec((None, 2, None, ML_DH, ML_DH), lambda i, h: (i, 0, h, 0, 0)),
            pl.BlockSpec((None, None, 8, LANES), lambda i, h: (i, h, 0, 0)),
            pl.BlockSpec((None, None, 8, LANES), lambda i, h: (i, h, 0, 0)),
        ]
        args += [c0, n0, m0]
    else:
        out_specs += [
            pl.BlockSpec((None, 2, None, ML_DH, ML_DH), lambda i, h: (i, 0, h, 0, 0)),
            pl.BlockSpec((None, None, 8, LANES), lambda i, h: (i, h, 0, 0)),
        ]
        out_shape += [
            jax.ShapeDtypeStruct((b, 2, ML_HEADS, ML_DH, ML_DH), F32),
            jax.ShapeDtypeStruct((b, ML_HEADS, 8, LANES), F32),
        ]
    return pl.pallas_call(
        functools.partial(_mlstm_kernel, has_init, nc),
        grid=(b, ML_HEADS),
        in_specs=in_specs,
        out_specs=out_specs,
        out_shape=out_shape,
        scratch_shapes=[pltpu.VMEM((t, LANES), F32)] * 8,
        compiler_params=_cparams(("parallel", "parallel")),
        name="mlstm_lat" if has_init else "mlstm_ctx",
    )(*args)


def _out_proj_kernel(x_ref, mod_ref, oa_ref, ob_ref, oc_ref, w_ref, y_ref):
    a_w = MLA_HEADS * MLA_V
    mix = (_dot(oa_ref[...], w_ref[0:a_w, :]) + _dot(ob_ref[...], w_ref[a_w:a_w + ML_WIDTH, :])
           + _dot(oc_ref[...], w_ref[a_w + ML_WIDTH:, :]))
    y_ref[...] = x_ref[...] + mod_ref[2:3, :] * mix


def _out_proj(x2, mod, seq, is_ctx, o_a, o_b, o_c, w_out):
    m = x2.shape[0]
    tm = TM
    mod_map = (lambda i: (0, 0, 0)) if is_ctx else (lambda i: (1 + (i * tm) // seq, 0, 0))
    row = lambda i: (i, 0)
    return pl.pallas_call(
        _out_proj_kernel,
        grid=(m // tm,),
        in_specs=[
            pl.BlockSpec((tm, D_MODEL), row),
            pl.BlockSpec((None, 6, D_MODEL), mod_map),
            pl.BlockSpec((tm, o_a.shape[1]), row),
            pl.BlockSpec((tm, o_b.shape[1]), row),
            pl.BlockSpec((tm, o_c.shape[1]), row),
            pl.BlockSpec(w_out.shape, lambda i: (0, 0)),
        ],
        out_specs=pl.BlockSpec((tm, D_MODEL), row),
        out_shape=jax.ShapeDtypeStruct((m, D_MODEL), F32),
        compiler_params=_cparams(("parallel",)),
        name="out_proj",
    )(x2, mod, o_a, o_b, o_c, w_out)


def _ffn_kernel(seq, tm, final, x_ref, xp_ref, xn_ref, mod_ref, g2_ref, wup_ref, cw_ref, cb_ref, wdn_ref,
                gf_ref, y_ref, yext_s, gext_s, h_s):
    i = pl.program_id(0)
    g2 = g2_ref[...]
    shift = mod_ref[3:4, :]
    scale = mod_ref[4:5, :]
    x = x_ref[...]
    keep_p = jnp.where((i * tm) % seq == 0, 0.0, 1.0)
    keep_n = jnp.where(((i + 1) * tm) % seq == 0, 0.0, 1.0)
    yext_s[0:HALO, :] = (_adaln(xp_ref[...], g2, shift, scale) * keep_p).astype(BF16)
    yext_s[HALO:HALO + tm, :] = _adaln(x, g2, shift, scale).astype(BF16)
    yext_s[HALO + tm:, :] = (_adaln(xn_ref[...], g2, shift, scale) * keep_n).astype(BF16)

    for c in range(D_FF // FF_CHUNK):
        c0 = c * FF_CHUNK
        a = _dot(yext_s[HALO:HALO + tm, :], wup_ref[:, c0:c0 + FF_CHUNK])
        gext_s[...] = _dot(yext_s[...], wup_ref[:, D_FF + c0:D_FF + c0 + FF_CHUNK])
        g = (cw_ref[0:1, c0:c0 + FF_CHUNK] * gext_s[HALO - 1:HALO - 1 + tm, :]
             + cw_ref[1:2, c0:c0 + FF_CHUNK] * gext_s[HALO:HALO + tm, :]
             + cw_ref[2:3, c0:c0 + FF_CHUNK] * gext_s[HALO + 1:HALO + 1 + tm, :]
             + cb_ref[:, c0:c0 + FF_CHUNK])
        h_s[:, c0:c0 + FF_CHUNK] = (_silu(g) * a).astype(BF16)
    y = x + mod_ref[5:6, :] * _dot(h_s[...], wdn_ref[...])
    if final:
        y = _rms(y, gf_ref[...])
    y_ref[...] = y


def _ffn(x2, mod, seq, is_ctx, wl, g_final, final):
    m = x2.shape[0]
    tm = min(TM, seq)
    nb = tm // HALO
    last_blk = m // HALO - 1
    mod_map = (lambda i: (0, 0, 0)) if is_ctx else (lambda i: (1 + (i * tm) // seq, 0, 0))
    const = lambda i: (0, 0)
    row = lambda i: (i, 0)
    return pl.pallas_call(
        functools.partial(_ffn_kernel, seq, tm, final),
        grid=(m // tm,),
        in_specs=[
            pl.BlockSpec((tm, D_MODEL), row),
            pl.BlockSpec((HALO, D_MODEL), lambda i: (jnp.maximum(i * nb - 1, 0), 0)),
            pl.BlockSpec((HALO, D_MODEL), lambda i: (jnp.minimum((i + 1) * nb, last_blk), 0)),
            pl.BlockSpec((None, 6, D_MODEL), mod_map),
            pl.BlockSpec((1, D_MODEL), const),
            pl.BlockSpec((D_MODEL, 2 * D_FF), const),
            pl.BlockSpec((3, D_FF), const),
            pl.BlockSpec((1, D_FF), const),
            pl.BlockSpec((D_FF, D_MODEL), const),
            pl.BlockSpec((1, D_MODEL), const),
        ],
        out_specs=pl.BlockSpec((tm, D_MODEL), row),
        out_shape=jax.ShapeDtypeStruct((m, D_MODEL), F32),
        scratch_shapes=[pltpu.VMEM((tm + 2 * HALO, D_MODEL), BF16),
                        pltpu.VMEM((tm + 2 * HALO, FF_CHUNK), F32),
                        pltpu.VMEM((tm, D_FF), BF16)],
        compiler_params=_cparams(("parallel",)),
        name="ffn_ctx" if is_ctx else "ffn_lat",
    )(x2, x2, x2, mod, wl["g_norm2"], wl["w_ff_up"], wl["w_ff_conv"], wl["b_ff_conv"], wl["w_ff_down"],
      g_final)


def _rope_tables(t):
    pos = jnp.arange(t, dtype=jnp.int32)
    rows = (pos // GRID_W).astype(F32)
    cols = (pos % GRID_W).astype(F32)

    def group(p, d):
        inv = ROPE_THETA ** (-jnp.arange(0, d, 2, dtype=F32) / d)
        ang = p[:, None] * inv[None, :]
        cs, sn, z = jnp.cos(ang), jnp.sin(ang), jnp.zeros_like(ang)
        return (jnp.concatenate([cs, cs], 1), jnp.concatenate([-sn, z], 1), jnp.concatenate([z, sn], 1))

    gr, gc = group(rows, MLA_ROPE // 2), group(cols, MLA_ROPE // 2)
    ones = jnp.ones((t, MLA_NOPE), F32)
    zeros = jnp.zeros((t, MLA_NOPE), F32)
    pad1 = jnp.ones((t, LANES - MLA_NOPE - MLA_ROPE), F32)
    pad0 = jnp.zeros((t, LANES - MLA_NOPE - MLA_ROPE), F32)
    mla = (jnp.concatenate([ones, gr[0], gc[0], pad1], 1),
           jnp.concatenate([zeros, gr[1], gc[1], pad0], 1),
           jnp.concatenate([zeros, gr[2], gc[2], pad0], 1))
    gr, gc = group(rows, GQA_DH // 2), group(cols, GQA_DH // 2)
    gqa = tuple(jnp.concatenate([gr[k], gc[k], gr[k], gc[k]], 1) for k in range(3))
    return mla + gqa


def _prep_layer(l, p):
    w_in = p["w_in"][l]
    pad_cols = lambda a, lo, hi: jnp.pad(a, ((0, 0), (lo, hi)))
    qg = w_in[:, 1968:2224].reshape(D_MODEL, GQA_HEADS, GQA_DH)[:, jnp.array([0, 2, 1, 3])].reshape(D_MODEL, -1)
    w_in_p = jnp.concatenate([
        w_in[:, 0:384],
        pad_cols(w_in[:, 384:416], MLA_NOPE, LANES - MLA_NOPE - MLA_ROPE),
        w_in[:, 416:1952],
        pad_cols(w_in[:, 1952:1968], 0, LANES - 4 * ML_HEADS),
        qg,
        w_in[:, 2224:2480],
    ], axis=1).astype(BF16)
    w_uq = p["w_mla_uq"][l].reshape(MLA_Q_RANK, MLA_HEADS, MLA_NOPE + MLA_ROPE)
    w_uq = jnp.pad(w_uq, ((0, 0), (0, 0), (0, LANES - MLA_NOPE - MLA_ROPE))).reshape(MLA_Q_RANK, -1).astype(BF16)
    w_ukv = p["w_mla_ukv"][l].reshape(MLA_KV_RANK, MLA_HEADS, MLA_NOPE + MLA_V)
    w_uk = jnp.pad(w_ukv[:, :, :MLA_NOPE], ((0, 0), (0, 0), (0, LANES - MLA_NOPE))).reshape(MLA_KV_RANK, -1)
    w_uv = w_ukv[:, :, MLA_NOPE:].reshape(MLA_KV_RANK, -1)
    w_ukv_p = jnp.concatenate([w_uk, w_uv], axis=1).astype(BF16)
    grp = jnp.arange(2 * LANES) // GQA_DH
    heads = jnp.arange(ML_HEADS)
    hmask = jnp.zeros((ML_HEADS, 8, LANES), F32)
    for k in range(4):
        hmask = hmask.at[heads, k, k * ML_HEADS + heads].set(1.0)
    return {
        "g_norm1": p["g_norm1"][l][None, :],
        "g_norm2": p["g_norm2"][l][None, :],
        "w_in": w_in_p,
        "g_mla_q": p["g_mla_q"][l][None, :],
        "w_uq": w_uq,
        "g_mla_kv": p["g_mla_kv"][l][None, :],
        "w_ukv": w_ukv_p,
        "b_gates": jnp.pad(p["b_ml_gates"][l], (0, LANES - 4 * ML_HEADS))[None, :],
        "g_gqa_q": jnp.tile(p["g_gqa_q"][l], GQA_HEADS)[None, :],
        "g_gqa_k": jnp.tile(p["g_gqa_k"][l], GQA_KV_HEADS)[None, :],
        "gsum": (grp[:, None] == grp[None, :]).astype(BF16),
        "hmask": hmask,
        "w_ml_conv": p["w_ml_conv"][l],
        "b_ml_conv": p["b_ml_conv"][l][None, :],
        "w_ml_q": p["w_ml_q"][l].astype(BF16),
        "w_ml_k": p["w_ml_k"][l].astype(BF16),
        "g_ml_out": p["g_ml_out"][l][None, :],
        "w_out": p["w_out"][l].astype(BF16),
        "w_ff_up": p["w_ff_up"][l].astype(BF16),
        "w_ff_conv": p["w_ff_conv"][l],
        "b_ff_conv": p["b_ff_conv"][l][None, :],
        "w_ff_down": p["w_ff_down"][l].astype(BF16),
    }


def _layer(x2, mod, seq, is_ctx, wl, tabs, cache, g_final, final):
    m = x2.shape[0]
    b = m // seq
    r3 = lambda a: a.reshape(b, seq, a.shape[-1])
    outs = _in_proj(x2, mod, seq, is_ctx, wl, tabs)
    q_m, k_m, v_m, u, v_ml, o_ml, gates, q_g, k_g, v_g = outs[:10]
    if is_ctx:
        segs_a = [(r3(k_m), r3(v_m))]
        segs_c = [(r3(k_g), r3(v_g))]
        init = None
    else:
        ckv_c, kr_c, kg_c, vg_c, c0, n0, m0 = cache
        tc = ckv_c.shape[1]
        kr_pad = jnp.pad(kr_c, ((0, 0), (0, 0), (MLA_NOPE, LANES - MLA_NOPE - MLA_ROPE)))
        kc, vc = _cache_kv(ckv_c.reshape(b * tc, -1), kr_pad.reshape(b * tc, LANES), wl["w_ukv"])
        segs_a = [(r3(k_m), r3(v_m)), (kc.reshape(b, tc, -1), vc.reshape(b, tc, -1))]
        segs_c = [(r3(k_g), r3(v_g)), (kg_c.reshape(b, tc, -1), vg_c.reshape(b, tc, -1))]
        n0p = jnp.pad(jnp.swapaxes(n0, 1, 2), ((0, 0), (0, 0), (0, 6), (0, 0)))
        m0p = jnp.pad(jnp.broadcast_to(jnp.swapaxes(m0, 1, 2)[..., None], m0.shape[:1] + (ML_HEADS, 2, LANES)),
                      ((0, 0), (0, 0), (0, 6), (0, 0)))
        init = (c0, n0p, m0p)
    o_a = _attention(True, r3(q_m), segs_a)
    o_c = _attention(False, r3(q_g), segs_c)
    ml = _mlstm(r3(u), r3(v_ml), r3(o_ml), r3(gates), wl, init)
    x2 = _out_proj(x2, mod, seq, is_ctx, o_a.reshape(m, -1), ml[0].reshape(m, -1), o_c.reshape(m, -1), wl["w_out"])
    x2 = _ffn(x2, mod, seq, is_ctx, wl, g_final, final)
    state = None
    if is_ctx:
        ckv_n, kr_raw = outs[10:12]
        sfin = ml[2]
        state = (
            r3(ckv_n),
            r3(kr_raw)[:, :, MLA_NOPE:MLA_NOPE + MLA_ROPE],
            k_g.reshape(b, seq, GQA_KV_HEADS, GQA_DH),
            v_g.reshape(b, seq, GQA_KV_HEADS, GQA_DH),
            ml[1],
            jnp.swapaxes(sfin[:, :, 0:2, :], 1, 2),
            jnp.swapaxes(sfin[:, :, 2:4, 0], 1, 2),
        )
    return x2, state


def kernel(x_prompt, x_sample, cache_mla_ckv, cache_mla_krope, cache_gqa_k, cache_gqa_v, state_mlstm_C, state_mlstm_n, state_mlstm_m, c, c_ctx, w_ada, b_ada, g_norm1, g_norm2, w_in, g_mla_q, w_mla_uq, g_mla_kv, w_mla_ukv, w_ml_conv, b_ml_conv, w_ml_q, w_ml_k, b_ml_gates, g_ml_out, g_gqa_q, g_gqa_k, w_out, w_ff_up, w_ff_conv, b_ff_conv, w_ff_down, g_final):
    params = {
        "g_norm1": g_norm1, "g_norm2": g_norm2, "w_in": w_in, "g_mla_q": g_mla_q, "w_mla_uq": w_mla_uq,
        "g_mla_kv": g_mla_kv, "w_mla_ukv": w_mla_ukv, "w_ml_conv": w_ml_conv, "b_ml_conv": b_ml_conv,
        "w_ml_q": w_ml_q, "w_ml_k": w_ml_k, "b_ml_gates": b_ml_gates, "g_ml_out": g_ml_out,
        "g_gqa_q": g_gqa_q, "g_gqa_k": g_gqa_k, "w_out": w_out, "w_ff_up": w_ff_up, "w_ff_conv": w_ff_conv,
        "b_ff_conv": b_ff_conv, "w_ff_down": w_ff_down,
    }
    depth = w_in.shape[0]
    bp, sp, _ = x_prompt.shape
    bs, ss, _ = x_sample.shape
    assert bs + 1 <= 16 and sp % ML_CHUNK == 0 and ss % TM == 0 and (bp * sp) % TM == 0

    cvec = jnp.concatenate([c_ctx[None, :], c, jnp.zeros((16 - 1 - bs, D_MODEL), F32)], axis=0)
    mod_all = _modulation(cvec, w_ada, b_ada).reshape(depth, 16, 6, D_MODEL)
    tabs = _rope_tables(ss)
    gf = g_final[None, :]

    xp = x_prompt.reshape(bp * sp, D_MODEL)
    xs = x_sample.reshape(bs * ss, D_MODEL)
    states = []
    for l in range(depth):
        wl = _prep_layer(l, params)
        final = l == depth - 1
        xp, st = _layer(xp, mod_all[l], sp, True, wl, None, None, gf, final)
        states.append(st)
        cache = (cache_mla_ckv[:, l], cache_mla_krope[:, l], cache_gqa_k[:, l], cache_gqa_v[:, l],
                 state_mlstm_C[:, l], state_mlstm_n[:, l], state_mlstm_m[:, l])
        xs, _ = _layer(xs, mod_all[l], ss, False, wl, tabs, cache, gf, final)
    new_state = tuple(jnp.stack([st[k] for st in states], axis=1) for k in range(7))
    return (xp.reshape(bp, sp, D_MODEL), xs.reshape(bs, ss, D_MODEL)) + new_state
```

```python
import functools

import jax
import jax.numpy as jnp
from jax import lax
from jax.experimental import pallas as pl
from jax.experimental.pallas import tpu as pltpu

F32 = jnp.float32
BF16 = jnp.bfloat16

D_MODEL = 1024
GRID_W = 64
ROPE_THETA = 10000.0
EPS = 1e-6
MLA_HEADS = 4
MLA_NOPE = 64
MLA_ROPE = 32
MLA_V = 64
MLA_Q_RANK = 256
MLA_KV_RANK = 128
ML_HEADS = 4
ML_DH = 128
ML_WIDTH = ML_HEADS * ML_DH
ML_CHUNK = 128
GQA_HEADS = 4
GQA_KV_HEADS = 2
GQA_DH = 64
D_FF = 2816
MLA_SCALE = (MLA_NOPE + MLA_ROPE) ** -0.5
GQA_SCALE = GQA_DH ** -0.5
ML_K_SCALE = ML_DH ** -0.5
LOG2E = 1.4426950408889634

LANES = 128
HALO = 16
VMEM_LIMIT = 52 * 1024 * 1024

C_CQ, C_CKV, C_KR, C_U, C_V, C_O, C_G, C_QG, C_KG, C_VG, IN_COLS_P = (
    0, 256, 384, 512, 1024, 1536, 2048, 2304, 2560, 2688, 2816)
GATE_COLS = 2 * LANES
FF_CHUNK = 256
TM = 512
TQ = 512


def _cparams(sem):
    return pltpu.CompilerParams(dimension_semantics=sem, vmem_limit_bytes=VMEM_LIMIT)


def _dot(a, b):
    return jnp.dot(a, b, preferred_element_type=F32)


def _dot_nt(a, b):
    return lax.dot_general(a, b, (((1,), (1,)), ((), ())), preferred_element_type=F32)


def _dot_tn(a, b):
    return lax.dot_general(a, b, (((0,), (0,)), ((), ())), preferred_element_type=F32)


def _rms(x, g):
    return (x * lax.rsqrt(jnp.mean(x * x, axis=-1, keepdims=True) + EPS)) * g


def _silu(x):
    return x * jax.nn.sigmoid(x)


def _adaln(x, g, shift, scale):
    return _rms(x, g) * (1.0 + scale) + shift


def _split2(x):
    hi = x.astype(BF16)
    lo = (x - hi.astype(F32)).astype(BF16)
    return hi, lo


def _split3(x):
    h1 = x.astype(BF16)
    r1 = x - h1.astype(F32)
    h2 = r1.astype(BF16)
    h3 = (r1 - h2.astype(F32)).astype(BF16)
    return h1, h2, h3


def _rope(x, cos, sa, sb, half):
    w = x.shape[-1]
    return x * cos + pltpu.roll(x, w - half, 1) * sa + pltpu.roll(x, half, 1) * sb


def _group_rms(x, gsum, g):
    hi, lo = _split2(x * x)
    ss = _dot(hi, gsum) + _dot(lo, gsum)
    return (x * lax.rsqrt(ss * (1.0 / GQA_DH) + EPS)) * g


def _mod_kernel(c_ref, w_ref, b_ref, o_ref):
    a = _silu(c_ref[...]).astype(BF16)
    o_ref[...] = _dot(a, w_ref[...].astype(BF16)) + b_ref[...]


def _modulation(cvec, w_ada, b_ada):
    nl = w_ada.shape[0]
    tn = 1536
    return pl.pallas_call(
        _mod_kernel,
        grid=(nl, 6 * D_MODEL // tn),
        in_specs=[
            pl.BlockSpec((16, D_MODEL), lambda l, j: (0, 0)),
            pl.BlockSpec((None, D_MODEL, tn), lambda l, j: (l, 0, j)),
            pl.BlockSpec((None, 1, tn), lambda l, j: (l, 0, j)),
        ],
        out_specs=pl.BlockSpec((None, 16, tn), lambda l, j: (l, 0, j)),
        out_shape=jax.ShapeDtypeStruct((nl, 16, 6 * D_MODEL), F32),
        compiler_params=_cparams(("arbitrary", "arbitrary")),
        name="modulation",
    )(cvec, w_ada, b_ada.reshape(nl, 1, 6 * D_MODEL))


def _in_proj_kernel(is_ctx, x_ref, mod_ref, g1_ref, win_ref, gq_ref, wuq_ref, gkv_ref, wukv_ref,
                    bg_ref, ggq_ref, ggk_ref, gsum_ref, *rest):
    if is_ctx:
        (qm_ref, km_ref, vm_ref, u_ref, vml_ref, oml_ref, gt_ref, qg_ref, kg_ref, vg_ref,
         ckv_ref, kr_ref) = rest
    else:
        (cm_ref, sam_ref, sbm_ref, cg_ref, sag_ref, sbg_ref,
         qm_ref, km_ref, vm_ref, u_ref, vml_ref, oml_ref, gt_ref, qg_ref, kg_ref, vg_ref) = rest

    y = _adaln(x_ref[...], g1_ref[...], mod_ref[0:1, :], mod_ref[1:2, :]).astype(BF16)

    def proj(c0, width):
        return _dot(y, win_ref[:, c0:c0 + width])

    cqn = _rms(proj(C_CQ, MLA_Q_RANK), gq_ref[...]).astype(BF16)
    qz = _dot(cqn, wuq_ref[...])
    for h in range(MLA_HEADS):
        blk = qz[:, LANES * h:LANES * (h + 1)]
        if not is_ctx:
            blk = _rope(blk, cm_ref[...], sam_ref[...], sbm_ref[...], 8)
        qm_ref[:, LANES * h:LANES * (h + 1)] = (blk * (MLA_SCALE * LOG2E)).astype(qm_ref.dtype)

    ckvn = _rms(proj(C_CKV, MLA_KV_RANK), gkv_ref[...])
    kr = proj(C_KR, LANES)
    if is_ctx:
        ckv_ref[...] = ckvn
        kr_ref[...] = kr
    else:
        kr = _rope(kr, cm_ref[...], sam_ref[...], sbm_ref[...], 8)
    kvz = _dot(ckvn.astype(BF16), wukv_ref[...])
    for h in range(MLA_HEADS):
        km_ref[:, LANES * h:LANES * (h + 1)] = (kvz[:, LANES * h:LANES * (h + 1)] + kr).astype(km_ref.dtype)
    vm_ref[...] = kvz[:, MLA_HEADS * LANES:].astype(vm_ref.dtype)

    u_ref[...] = proj(C_U, ML_WIDTH).astype(u_ref.dtype)
    vml_ref[...] = proj(C_V, ML_WIDTH).astype(vml_ref.dtype)
    oml_ref[...] = proj(C_O, ML_WIDTH).astype(oml_ref.dtype)
    gates = proj(C_G, GATE_COLS) + bg_ref[...]
    for pr in range(GATE_COLS // LANES):
        gt_ref[pr] = gates[:, LANES * pr:LANES * (pr + 1)].T[0:8, :]

    qg = _group_rms(proj(C_QG, GQA_HEADS * GQA_DH), gsum_ref[...], ggq_ref[...])
    kg = _group_rms(proj(C_KG, LANES), gsum_ref[0:LANES, 0:LANES], ggk_ref[...])
    if not is_ctx:
        qg = jnp.concatenate(
            [_rope(qg[:, 0:LANES], cg_ref[...], sag_ref[...], sbg_ref[...], 16),
             _rope(qg[:, LANES:], cg_ref[...], sag_ref[...], sbg_ref[...], 16)], axis=1)
        kg = _rope(kg, cg_ref[...], sag_ref[...], sbg_ref[...], 16)
    qg_ref[...] = (qg * (GQA_SCALE * LOG2E)).astype(qg_ref.dtype)
    kg_ref[...] = kg.astype(kg_ref.dtype)
    vg_ref[...] = proj(C_VG, LANES).astype(vg_ref.dtype)


def _in_proj(x2, mod, seq, is_ctx, wl, tabs):
    m = x2.shape[0]
    tm = TM
    nt = m // tm
    if is_ctx:
        mod_map = lambda i: (0, 0, 0)
    else:
        mod_map = lambda i: (1 + (i * tm) // seq, 0, 0)
    const = lambda i: (0, 0)
    row = lambda i: (i, 0)
    in_specs = [
        pl.BlockSpec((tm, D_MODEL), row),
        pl.BlockSpec((None, 6, D_MODEL), mod_map),
        pl.BlockSpec((1, D_MODEL), const),
        pl.BlockSpec((D_MODEL, IN_COLS_P), const),
        pl.BlockSpec((1, MLA_Q_RANK), const),
        pl.BlockSpec((MLA_Q_RANK, MLA_HEADS * LANES), const),
        pl.BlockSpec((1, MLA_KV_RANK), const),
        pl.BlockSpec((MLA_KV_RANK, MLA_HEADS * LANES + MLA_HEADS * MLA_V), const),
        pl.BlockSpec((1, GATE_COLS), const),
        pl.BlockSpec((1, GQA_HEADS * GQA_DH), const),
        pl.BlockSpec((1, LANES), const),
        pl.BlockSpec((2 * LANES, 2 * LANES), const),
    ]
    args = [x2, mod, wl["g_norm1"], wl["w_in"], wl["g_mla_q"], wl["w_uq"], wl["g_mla_kv"], wl["w_ukv"],
            wl["b_gates"], wl["g_gqa_q"], wl["g_gqa_k"], wl["gsum"]]
    if not is_ctx:
        tpos = seq // tm
        in_specs += [pl.BlockSpec((tm, LANES), lambda i: (i % tpos, 0))] * 6
        args += list(tabs)
    act = F32 if is_ctx else BF16
    outs = [
        (MLA_HEADS * LANES, BF16),
        (MLA_HEADS * LANES, BF16),
        (MLA_HEADS * MLA_V, BF16),
        (ML_WIDTH, F32),
        (ML_WIDTH, BF16),
        (ML_WIDTH, F32),
        (None, F32),
        (GQA_HEADS * GQA_DH, BF16),
        (LANES, act),
        (LANES, act),
    ]
    if is_ctx:
        outs += [(MLA_KV_RANK, F32), (LANES, F32)]
    return pl.pallas_call(
        functools.partial(_in_proj_kernel, is_ctx),
        grid=(nt,),
        in_specs=in_specs,
        out_specs=[pl.BlockSpec((GATE_COLS // LANES, 8, tm), lambda i: (0, 0, i)) if w is None
                   else pl.BlockSpec((tm, w), row) for w, _ in outs],
        out_shape=[jax.ShapeDtypeStruct((GATE_COLS // LANES, 8, m) if w is None else (m, w), dt)
                   for w, dt in outs],
        compiler_params=_cparams(("parallel",)),
        name="in_proj_ctx" if is_ctx else "in_proj_lat",
    )(*args)


def _cache_kv_kernel(ckv_ref, kr_ref, wukv_ref, k_ref, v_ref):
    kvz = _dot(ckv_ref[...].astype(BF16), wukv_ref[...])
    kr = kr_ref[...]
    for h in range(MLA_HEADS):
        k_ref[:, LANES * h:LANES * (h + 1)] = (kvz[:, LANES * h:LANES * (h + 1)] + kr).astype(k_ref.dtype)
    v_ref[...] = kvz[:, MLA_HEADS * LANES:].astype(v_ref.dtype)


def _cache_kv(ckv2, kr2, w_ukv):
    m = ckv2.shape[0]
    tm = 256
    row = lambda i: (i, 0)
    return pl.pallas_call(
        _cache_kv_kernel,
        grid=(m // tm,),
        in_specs=[pl.BlockSpec((tm, MLA_KV_RANK), row), pl.BlockSpec((tm, LANES), row),
                  pl.BlockSpec(w_ukv.shape, lambda i: (0, 0))],
        out_specs=[pl.BlockSpec((tm, MLA_HEADS * LANES), row), pl.BlockSpec((tm, MLA_HEADS * MLA_V), row)],
        out_shape=[jax.ShapeDtypeStruct((m, MLA_HEADS * LANES), BF16),
                   jax.ShapeDtypeStruct((m, MLA_HEADS * MLA_V), BF16)],
        compiler_params=_cparams(("parallel",)),
        name="cache_kv",
    )(ckv2, kr2, w_ukv)


def _attn_kernel(is_mla, nseg, q_ref, *refs):
    kv = refs[:2 * nseg]
    o_ref = refs[2 * nseg]
    tq = q_ref.shape[0]
    lo = lax.broadcasted_iota(jnp.int32, (tq, LANES), 1) < (LANES // 2)
    heads = []
    for h in range(4):
        if is_mla:
            qh = q_ref[:, LANES * h:LANES * (h + 1)]
            ksl = slice(LANES * h, LANES * (h + 1))
            vsl = slice(LANES * (h // 2), LANES * (h // 2 + 1))
        else:
            blk = q_ref[:, LANES * (h % 2):LANES * (h % 2 + 1)].astype(F32)
            qh = jnp.where(lo if h < 2 else jnp.logical_not(lo), blk, 0.0).astype(BF16)
            ksl = slice(0, LANES)
            vsl = slice(0, LANES)
        ss = [_dot_nt(qh, kv[2 * j][:, ksl].astype(BF16)) for j in range(nseg)]
        mx = jnp.max(ss[0], axis=-1, keepdims=True)
        for s in ss[1:]:
            mx = jnp.maximum(mx, jnp.max(s, axis=-1, keepdims=True))
        den = None
        acc = None
        for j, s in enumerate(ss):
            e = jnp.exp2(s - mx)
            d = jnp.sum(e, axis=-1, keepdims=True)
            a = _dot(e.astype(BF16), kv[2 * j + 1][:, vsl].astype(BF16))
            den = d if den is None else den + d
            acc = a if acc is None else acc + a
        heads.append(acc / den)
    if is_mla:
        o_ref[:, 0:LANES] = jnp.where(lo, heads[0], heads[1]).astype(o_ref.dtype)
        o_ref[:, LANES:] = jnp.where(lo, heads[2], heads[3]).astype(o_ref.dtype)
    else:
        half = LANES // 2
        o_ref[:, 0:LANES] = jnp.where(lo, heads[0], pltpu.roll(heads[1], half, 1)).astype(o_ref.dtype)
        o_ref[:, LANES:] = jnp.where(lo, pltpu.roll(heads[2], half, 1), heads[3]).astype(o_ref.dtype)


def _attention(is_mla, q, segs):
    b, tq_all, wq = q.shape
    tq = min(TQ, tq_all)
    in_specs = [pl.BlockSpec((None, tq, wq), lambda i, j: (i, j, 0))]
    args = [q]
    for k, v in segs:
        in_specs.append(pl.BlockSpec((None,) + k.shape[1:], lambda i, j: (i, 0, 0)))
        in_specs.append(pl.BlockSpec((None,) + v.shape[1:], lambda i, j: (i, 0, 0)))
        args += [k, v]
    return pl.pallas_call(
        functools.partial(_attn_kernel, is_mla, len(segs)),
        grid=(b, tq_all // tq),
        in_specs=in_specs,
        out_specs=pl.BlockSpec((None, tq, 2 * LANES), lambda i, j: (i, j, 0)),
        out_shape=jax.ShapeDtypeStruct((b, tq_all, 2 * LANES), BF16),
        compiler_params=_cparams(("parallel", "parallel")),
        name=("attn_mla" if is_mla else "attn_gqa") + ("_lat" if len(segs) > 1 else "_ctx"),
    )(*args)


def _log_sigmoid(x):
    return jnp.minimum(x, 0.0) - jnp.log(1.0 + jnp.exp(-jnp.abs(x)))


HPS = 2


def _mlstm_kernel(has_init, nc, u_ref, v_ref, o_ref, gt_ref, cw_ref, cb_ref, wqt_ref, wk_ref, go_ref, *rest):
    if has_init:
        ct0_ref, n0_ref, m0_ref, out_ref, qt_s, k_s, vt_s, ht_f, ht_b, acol_s, row_s, st_s = rest
    else:
        out_ref, cfin_ref, sfin_ref, qt_s, k_s, vt_s, ht_f, ht_b, acol_s, row_s, st_s = rest
    t = nc * ML_CHUNK
    sq = (ML_CHUNK, ML_CHUNK)
    w2 = HPS * LANES

    u = u_ref[...]
    rows = lax.broadcasted_iota(jnp.int32, (t, w2), 0)
    up = jnp.where(rows == 0, 0.0, pltpu.roll(u, 1, 0))
    un = jnp.where(rows == t - 1, 0.0, pltpu.roll(u, t - 1, 0))
    uc = _silu(cw_ref[0:1, :] * up + cw_ref[1:2, :] * u + cw_ref[2:3, :] * un + cb_ref[...]).astype(BF16)
    r_i = lax.broadcasted_iota(jnp.int32, sq, 0)
    c_i = lax.broadcasted_iota(jnp.int32, sq, 1)
    lower = c_i <= r_i
    upper = c_i >= r_i
    eye = jnp.where(c_i == r_i, 1.0, 0.0).astype(BF16)
    for j in range(HPS):
        hs = slice(LANES * j, LANES * (j + 1))
        qt_s[hs, :] = _dot_nt(wqt_ref[j], uc[:, hs]).astype(BF16)
        k_s[:, hs] = _dot(uc[:, hs], wk_ref[j]) * ML_K_SCALE
        vt_s[hs, :] = _dot_nt(eye, v_ref[:, hs]).astype(BF16)

    tri_lo = jnp.where(lower, 1.0, 0.0).astype(BF16)
    tri_up = jnp.where(upper, 1.0, 0.0).astype(BF16)
    gates = gt_ref[...]
    row8 = lax.broadcasted_iota(jnp.int32, (8, ML_CHUNK), 0)
    lg_all = jnp.where(lax.broadcasted_iota(jnp.int32, (8, t), 0) % 4 >= 2, _log_sigmoid(gates), gates)
    for c in range(nc):
        sl = slice(c * ML_CHUNK, (c + 1) * ML_CHUNK)
        lg = lg_all[:, sl]
        h1, h2, h3 = _split3(lg)
        cs = jnp.where(row8 < 4, _dot(h1, tri_up) + _dot(h2, tri_up) + _dot(h3, tri_up),
                       _dot(h1, tri_lo) + _dot(h2, tri_lo) + _dot(h3, tri_lo)) * LOG2E
        row_s[:, sl] = cs
        a = lg * LOG2E - pltpu.roll(cs, 6, 0)
        for j in range(HPS):
            hs = slice(LANES * j, LANES * (j + 1))
            st_s[j, sl, :] = _dot(k_s[sl, hs].astype(BF16), qt_s[hs, sl])
            for d in range(2):
                r = 4 * d + j
                acol_s[2 * j + d, sl, :] = jnp.broadcast_to(a[r:r + 1, :], sq).T

    def chunk(j, d, c0, ct, n, m2):
        hs = slice(LANES * j, LANES * (j + 1))
        cs_ = pl.ds(c0, ML_CHUNK)
        last = 0 if d else ML_CHUNK - 1
        qt = qt_s[hs, cs_]
        k = k_s[cs_, hs]
        vt = vt_s[hs, cs_]
        acol = acol_s[2 * j + d, cs_, :]
        bc = row_s[2 + 4 * d + j:3 + 4 * d + j, cs_]
        g = bc + m2
        dlog = jnp.where(lower if d else upper, acol + bc, -jnp.inf)
        m_t = jnp.maximum(g, jnp.max(dlog, axis=0, keepdims=True))
        w = jnp.exp2(dlog - m_t)
        inter = jnp.exp2(g - m_t)
        st = st_s[j, cs_, :] * w
        qn = _dot(jnp.broadcast_to(n, (8, LANES)).astype(BF16), qt)[0:1, :]
        den = jnp.sum(st, axis=0, keepdims=True) + inter * qn
        numt = _dot(vt, st.astype(BF16)) + inter * _dot(ct.astype(BF16), qt)
        ht = numt * (1.0 / jnp.maximum(jnp.abs(den), jnp.exp2(-m_t)))
        b_last = jnp.broadcast_to(bc[:, last:last + 1], (1, LANES))
        m_new = jnp.broadcast_to(m_t[:, last:last + 1], (1, LANES))
        kw = k * jnp.exp2(b_last + acol - m_new)
        decay = jnp.exp2(b_last + m2 - m_new)
        ct_new = decay * ct + _dot(vt, kw.astype(BF16))
        n_new = decay * n + jnp.sum(kw, axis=0, keepdims=True)
        return ht, ct_new, n_new, m_new

    def step(i, carry):
        cf = i * ML_CHUNK
        cb = (nc - 1 - i) * ML_CHUNK
        if not isinstance(i, int):
            cf = pl.multiple_of(cf, ML_CHUNK)
            cb = pl.multiple_of(cb, ML_CHUNK)
        new = []
        for j in range(HPS):
            hs = slice(LANES * j, LANES * (j + 1))
            for d, (c0, ht_s) in enumerate(((cf, ht_f), (cb, ht_b))):
                ht, ct, n, m2 = chunk(j, d, c0, *carry[2 * j + d])
                ht_s[hs, pl.ds(c0, ML_CHUNK)] = ht
                new.append((ct, n, m2))
        return tuple(new)

    if has_init:
        init = tuple((ct0_ref[d, j], n0_ref[j, d:d + 1, :], m0_ref[j, d:d + 1, :] * LOG2E)
                     for j in range(HPS) for d in range(2))
    else:
        init = tuple((jnp.zeros(sq, F32), jnp.zeros((1, LANES), F32), jnp.zeros((1, LANES), F32))
                     for j in range(HPS) for d in range(2))
    if nc <= 2:
        carry = init
        for i in range(nc):
            carry = step(i, carry)
    else:
        carry = lax.fori_loop(0, nc, step, init, unroll=4)

    hsum = (ht_f[...] + ht_b[...]).T
    for j in range(HPS):
        hs = slice(LANES * j, LANES * (j + 1))
        hn = _rms(hsum[:, hs], go_ref[:, hs])
        out_ref[:, hs] = (hn * jax.nn.sigmoid(o_ref[:, hs])).astype(out_ref.dtype)

    if not has_init:
        for j in range(HPS):
            for d in range(2):
                ct, n, m2 = carry[2 * j + d]
                cfin_ref[d, j] = ct.T
            nf, nb = carry[2 * j][1], carry[2 * j + 1][1]
            mf, mb = carry[2 * j][2], carry[2 * j + 1][2]
            sfin_ref[j] = jnp.concatenate(
                [nf, nb, mf * (1.0 / LOG2E), mb * (1.0 / LOG2E), jnp.zeros((4, LANES), F32)], axis=0)


def _mlstm(u, v_ml, o_ml, gates, wl, init):
    b, t, _ = u.shape
    nc = t // ML_CHUNK
    has_init = init is not None
    w2 = HPS * LANES
    pair_blk = lambda i, p: (i, 0, p)
    in_specs = [
        pl.BlockSpec((None, t, w2), pair_blk),
        pl.BlockSpec((None, t, w2), pair_blk),
        pl.BlockSpec((None, t, w2), pair_blk),
        pl.BlockSpec((None, 8, t), lambda i, p: (p, 0, i)),
        pl.BlockSpec((3, w2), lambda i, p: (0, p)),
        pl.BlockSpec((1, w2), lambda i, p: (0, p)),
        pl.BlockSpec((HPS, ML_DH, ML_DH), lambda i, p: (p, 0, 0)),
        pl.BlockSpec((HPS, ML_DH, ML_DH), lambda i, p: (p, 0, 0)),
        pl.BlockSpec((1, w2), lambda i, p: (0, p)),
    ]
    args = [u, v_ml, o_ml, gates, wl["w_ml_conv"], wl["b_ml_conv"], wl["w_ml_qt"], wl["w_ml_k"], wl["g_ml_out"]]
    out_specs = [pl.BlockSpec((None, t, w2), pair_blk)]
    out_shape = [jax.ShapeDtypeStruct((b, t, ML_WIDTH), BF16)]
    state_c = pl.BlockSpec((None, 2, HPS, ML_DH, ML_DH), lambda i, p: (i, 0, p, 0, 0))
    state_s = pl.BlockSpec((None, HPS, 8, LANES), lambda i, p: (i, p, 0, 0))
    if has_init:
        in_specs += [state_c, state_s, state_s]
        args += list(init)
    else:
        out_specs += [state_c, state_s]
        out_shape += [
            jax.ShapeDtypeStruct((b, 2, ML_HEADS, ML_DH, ML_DH), F32),
            jax.ShapeDtypeStruct((b, ML_HEADS, 8, LANES), F32),
        ]
    return pl.pallas_call(
        functools.partial(_mlstm_kernel, has_init, nc),
        grid=(b, ML_HEADS // HPS),
        in_specs=in_specs,
        out_specs=out_specs,
        out_shape=out_shape,
        scratch_shapes=[
            pltpu.VMEM((w2, t), BF16),
            pltpu.VMEM((t, w2), F32),
            pltpu.VMEM((w2, t), BF16),
            pltpu.VMEM((w2, t), F32),
            pltpu.VMEM((w2, t), F32),
            pltpu.VMEM((2 * HPS, t, LANES), F32),
            pltpu.VMEM((8, t), F32),
            pltpu.VMEM((HPS, t, LANES), F32),
        ],
        compiler_params=_cparams(("parallel", "parallel")),
        name="mlstm_lat" if has_init else "mlstm_ctx",
    )(*args)


def _out_proj_kernel(x_ref, mod_ref, oa_ref, ob_ref, oc_ref, w_ref, y_ref):
    a_w = MLA_HEADS * MLA_V
    mix = (_dot(oa_ref[...], w_ref[0:a_w, :]) + _dot(ob_ref[...], w_ref[a_w:a_w + ML_WIDTH, :])
           + _dot(oc_ref[...], w_ref[a_w + ML_WIDTH:, :]))
    y_ref[...] = x_ref[...] + mod_ref[2:3, :] * mix


def _out_proj(x2, mod, seq, is_ctx, o_a, o_b, o_c, w_out):
    m = x2.shape[0]
    tm = TM
    mod_map = (lambda i: (0, 0, 0)) if is_ctx else (lambda i: (1 + (i * tm) // seq, 0, 0))
    row = lambda i: (i, 0)
    return pl.pallas_call(
        _out_proj_kernel,
        grid=(m // tm,),
        in_specs=[
            pl.BlockSpec((tm, D_MODEL), row),
            pl.BlockSpec((None, 6, D_MODEL), mod_map),
            pl.BlockSpec((tm, o_a.shape[1]), row),
            pl.BlockSpec((tm, o_b.shape[1]), row),
            pl.BlockSpec((tm, o_c.shape[1]), row),
            pl.BlockSpec(w_out.shape, lambda i: (0, 0)),
        ],
        out_specs=pl.BlockSpec((tm, D_MODEL), row),
        out_shape=jax.ShapeDtypeStruct((m, D_MODEL), F32),
        compiler_params=_cparams(("parallel",)),
        name="out_proj",
    )(x2, mod, o_a, o_b, o_c, w_out)


def _ffn_kernel(seq, tm, final, x_ref, xp_ref, xn_ref, mod_ref, g2_ref, wup_ref, cw_ref, cb_ref, wdn_ref,
                gf_ref, y_ref, yext_s, gext_s, h_s):
    i = pl.program_id(0)
    g2 = g2_ref[...]
    shift = mod_ref[3:4, :]
    scale = mod_ref[4:5, :]
    x = x_ref[...]
    keep_p = jnp.where((i * tm) % seq == 0, 0.0, 1.0)
    keep_n = jnp.where(((i + 1) * tm) % seq == 0, 0.0, 1.0)
    yext_s[0:HALO, :] = (_adaln(xp_ref[...], g2, shift, scale) * keep_p).astype(BF16)
    yext_s[HALO:HALO + tm, :] = _adaln(x, g2, shift, scale).astype(BF16)
    yext_s[HALO + tm:, :] = (_adaln(xn_ref[...], g2, shift, scale) * keep_n).astype(BF16)

    for c in range(D_FF // FF_CHUNK):
        c0 = c * FF_CHUNK
        a = _dot(yext_s[HALO:HALO + tm, :], wup_ref[:, c0:c0 + FF_CHUNK])
        gext_s[...] = _dot(yext_s[...], wup_ref[:, D_FF + c0:D_FF + c0 + FF_CHUNK])
        g = (cw_ref[0:1, c0:c0 + FF_CHUNK] * gext_s[HALO - 1:HALO - 1 + tm, :]
             + cw_ref[1:2, c0:c0 + FF_CHUNK] * gext_s[HALO:HALO + tm, :]
             + cw_ref[2:3, c0:c0 + FF_CHUNK] * gext_s[HALO + 1:HALO + 1 + tm, :]
             + cb_ref[:, c0:c0 + FF_CHUNK])
        h_s[:, c0:c0 + FF_CHUNK] = (_silu(g) * a).astype(BF16)
    y = x + mod_ref[5:6, :] * _dot(h_s[...], wdn_ref[...])
    if final:
        y = _rms(y, gf_ref[...])
    y_ref[...] = y


def _ffn(x2, mod, seq, is_ctx, wl, g_final, final):
    m = x2.shape[0]
    tm = min(TM, seq)
    nb = tm // HALO
    last_blk = m // HALO - 1
    mod_map = (lambda i: (0, 0, 0)) if is_ctx else (lambda i: (1 + (i * tm) // seq, 0, 0))
    const = lambda i: (0, 0)
    row = lambda i: (i, 0)
    return pl.pallas_call(
        functools.partial(_ffn_kernel, seq, tm, final),
        grid=(m // tm,),
        in_specs=[
            pl.BlockSpec((tm, D_MODEL), row),
            pl.BlockSpec((HALO, D_MODEL), lambda i: (jnp.maximum(i * nb - 1, 0), 0)),
            pl.BlockSpec((HALO, D_MODEL), lambda i: (jnp.minimum((i + 1) * nb, last_blk), 0)),
            pl.BlockSpec((None, 6, D_MODEL), mod_map),
            pl.BlockSpec((1, D_MODEL), const),
            pl.BlockSpec((D_MODEL, 2 * D_FF), const),
            pl.BlockSpec((3, D_FF), const),
            pl.BlockSpec((1, D_FF), const),
            pl.BlockSpec((D_FF, D_MODEL), const),
            pl.BlockSpec((1, D_MODEL), const),
        ],
        out_specs=pl.BlockSpec((tm, D_MODEL), row),
        out_shape=jax.ShapeDtypeStruct((m, D_MODEL), F32),
        scratch_shapes=[pltpu.VMEM((tm + 2 * HALO, D_MODEL), BF16),
                        pltpu.VMEM((tm + 2 * HALO, FF_CHUNK), F32),
                        pltpu.VMEM((tm, D_FF), BF16)],
        compiler_params=_cparams(("parallel",)),
        name="ffn_ctx" if is_ctx else "ffn_lat",
    )(x2, x2, x2, mod, wl["g_norm2"], wl["w_ff_up"], wl["w_ff_conv"], wl["b_ff_conv"], wl["w_ff_down"],
      g_final)


def _rope_tables(t):
    pos = jnp.arange(t, dtype=jnp.int32)
    rows = (pos // GRID_W).astype(F32)
    cols = (pos % GRID_W).astype(F32)

    def group(p, d):
        inv = ROPE_THETA ** (-jnp.arange(0, d, 2, dtype=F32) / d)
        ang = p[:, None] * inv[None, :]
        cs, sn, z = jnp.cos(ang), jnp.sin(ang), jnp.zeros_like(ang)
        return (jnp.concatenate([cs, cs], 1), jnp.concatenate([-sn, z], 1), jnp.concatenate([z, sn], 1))

    gr, gc = group(rows, MLA_ROPE // 2), group(cols, MLA_ROPE // 2)
    ones = jnp.ones((t, MLA_NOPE), F32)
    zeros = jnp.zeros((t, MLA_NOPE), F32)
    pad1 = jnp.ones((t, LANES - MLA_NOPE - MLA_ROPE), F32)
    pad0 = jnp.zeros((t, LANES - MLA_NOPE - MLA_ROPE), F32)
    mla = (jnp.concatenate([ones, gr[0], gc[0], pad1], 1),
           jnp.concatenate([zeros, gr[1], gc[1], pad0], 1),
           jnp.concatenate([zeros, gr[2], gc[2], pad0], 1))
    gr, gc = group(rows, GQA_DH // 2), group(cols, GQA_DH // 2)
    gqa = tuple(jnp.concatenate([gr[k], gc[k], gr[k], gc[k]], 1) for k in range(3))
    return mla + gqa


def _prep_layer(l, p):
    w_in = p["w_in"][l]
    pad_cols = lambda a, lo, hi: jnp.pad(a, ((0, 0), (lo, hi)))
    qg = w_in[:, 1968:2224].reshape(D_MODEL, GQA_HEADS, GQA_DH)[:, jnp.array([0, 2, 1, 3])].reshape(D_MODEL, -1)
    gsrc = jnp.array([kind * ML_HEADS + HPS * pr + j for pr in range(2) for kind in range(4) for j in range(HPS)])
    gdst = jnp.array([LANES * pr + HPS * kind + j for pr in range(2) for kind in range(4) for j in range(HPS)])
    w_gates = jnp.zeros((D_MODEL, GATE_COLS), F32).at[:, gdst].set(w_in[:, 1952 + gsrc])
    b_gates = jnp.zeros((GATE_COLS,), F32).at[gdst].set(p["b_ml_gates"][l][gsrc])
    w_in_p = jnp.concatenate([
        w_in[:, 0:384],
        pad_cols(w_in[:, 384:416], MLA_NOPE, LANES - MLA_NOPE - MLA_ROPE),
        w_in[:, 416:1952],
        w_gates,
        qg,
        w_in[:, 2224:2480],
    ], axis=1).astype(BF16)
    w_uq = p["w_mla_uq"][l].reshape(MLA_Q_RANK, MLA_HEADS, MLA_NOPE + MLA_ROPE)
    w_uq = jnp.pad(w_uq, ((0, 0), (0, 0), (0, LANES - MLA_NOPE - MLA_ROPE))).reshape(MLA_Q_RANK, -1).astype(BF16)
    w_ukv = p["w_mla_ukv"][l].reshape(MLA_KV_RANK, MLA_HEADS, MLA_NOPE + MLA_V)
    w_uk = jnp.pad(w_ukv[:, :, :MLA_NOPE], ((0, 0), (0, 0), (0, LANES - MLA_NOPE))).reshape(MLA_KV_RANK, -1)
    w_uv = w_ukv[:, :, MLA_NOPE:].reshape(MLA_KV_RANK, -1)
    w_ukv_p = jnp.concatenate([w_uk, w_uv], axis=1).astype(BF16)
    grp = jnp.arange(2 * LANES) // GQA_DH
    return {
        "g_norm1": p["g_norm1"][l][None, :],
        "g_norm2": p["g_norm2"][l][None, :],
        "w_in": w_in_p,
        "g_mla_q": p["g_mla_q"][l][None, :],
        "w_uq": w_uq,
        "g_mla_kv": p["g_mla_kv"][l][None, :],
        "w_ukv": w_ukv_p,
        "b_gates": b_gates[None, :],
        "g_gqa_q": jnp.tile(p["g_gqa_q"][l], GQA_HEADS)[None, :],
        "g_gqa_k": jnp.tile(p["g_gqa_k"][l], GQA_KV_HEADS)[None, :],
        "gsum": (grp[:, None] == grp[None, :]).astype(BF16),
        "w_ml_conv": p["w_ml_conv"][l],
        "b_ml_conv": p["b_ml_conv"][l][None, :],
        "w_ml_qt": jnp.swapaxes(p["w_ml_q"][l], 1, 2).astype(BF16),
        "w_ml_k": p["w_ml_k"][l].astype(BF16),
        "g_ml_out": p["g_ml_out"][l][None, :],
        "w_out": p["w_out"][l].astype(BF16),
        "w_ff_up": p["w_ff_up"][l].astype(BF16),
        "w_ff_conv": p["w_ff_conv"][l],
        "b_ff_conv": p["b_ff_conv"][l][None, :],
        "w_ff_down": p["w_ff_down"][l].astype(BF16),
    }


def _layer(x2, mod, seq, is_ctx, wl, tabs, cache, g_final, final):
    m = x2.shape[0]
    b = m // seq
    r3 = lambda a: a.reshape(b, seq, a.shape[-1])
    outs = _in_proj(x2, mod, seq, is_ctx, wl, tabs)
    q_m, k_m, v_m, u, v_ml, o_ml, gates, q_g, k_g, v_g = outs[:10]
    if is_ctx:
        segs_a = [(r3(k_m), r3(v_m))]
        segs_c = [(r3(k_g), r3(v_g))]
        init = None
    else:
        ckv_c, kr_c, kg_c, vg_c, c0, n0, m0 = cache
        tc = ckv_c.shape[1]
        kr_pad = jnp.pad(kr_c, ((0, 0), (0, 0), (MLA_NOPE, LANES - MLA_NOPE - MLA_ROPE)))
        kc, vc = _cache_kv(ckv_c.reshape(b * tc, -1), kr_pad.reshape(b * tc, LANES), wl["w_ukv"])
        segs_a = [(r3(k_m), r3(v_m)), (kc.reshape(b, tc, -1), vc.reshape(b, tc, -1))]
        segs_c = [(r3(k_g), r3(v_g)), (kg_c.reshape(b, tc, -1), vg_c.reshape(b, tc, -1))]
        n0p = jnp.pad(jnp.swapaxes(n0, 1, 2), ((0, 0), (0, 0), (0, 6), (0, 0)))
        m0p = jnp.pad(jnp.broadcast_to(jnp.swapaxes(m0, 1, 2)[..., None], m0.shape[:1] + (ML_HEADS, 2, LANES)),
                      ((0, 0), (0, 0), (0, 6), (0, 0)))
        init = (jnp.swapaxes(c0, -1, -2), n0p, m0p)
    o_a = _attention(True, r3(q_m), segs_a)
    o_c = _attention(False, r3(q_g), segs_c)
    ml = _mlstm(r3(u), r3(v_ml), r3(o_ml), gates, wl, init)
    x2 = _out_proj(x2, mod, seq, is_ctx, o_a.reshape(m, -1), ml[0].reshape(m, -1), o_c.reshape(m, -1), wl["w_out"])
    x2 = _ffn(x2, mod, seq, is_ctx, wl, g_final, final)
    state = None
    if is_ctx:
        ckv_n, kr_raw = outs[10:12]
        sfin = ml[2]
        state = (
            r3(ckv_n),
            r3(kr_raw)[:, :, MLA_NOPE:MLA_NOPE + MLA_ROPE],
            k_g.reshape(b, seq, GQA_KV_HEADS, GQA_DH),
            v_g.reshape(b, seq, GQA_KV_HEADS, GQA_DH),
            ml[1],
            jnp.swapaxes(sfin[:, :, 0:2, :], 1, 2),
            jnp.swapaxes(sfin[:, :, 2:4, 0], 1, 2),
        )
    return x2, state


def kernel(x_prompt, x_sample, cache_mla_ckv, cache_mla_krope, cache_gqa_k, cache_gqa_v, state_mlstm_C, state_mlstm_n, state_mlstm_m, c, c_ctx, w_ada, b_ada, g_norm1, g_norm2, w_in, g_mla_q, w_mla_uq, g_mla_kv, w_mla_ukv, w_ml_conv, b_ml_conv, w_ml_q, w_ml_k, b_ml_gates, g_ml_out, g_gqa_q, g_gqa_k, w_out, w_ff_up, w_ff_conv, b_ff_conv, w_ff_down, g_final):
    params = {
        "g_norm1": g_norm1, "g_norm2": g_norm2, "w_in": w_in, "g_mla_q": g_mla_q, "w_mla_uq": w_mla_uq,
        "g_mla_kv": g_mla_kv, "w_mla_ukv": w_mla_ukv, "w_ml_conv": w_ml_conv, "b_ml_conv": b_ml_conv,
        "w_ml_q": w_ml_q, "w_ml_k": w_ml_k, "b_ml_gates": b_ml_gates, "g_ml_out": g_ml_out,
        "g_gqa_q": g_gqa_q, "g_gqa_k": g_gqa_k, "w_out": w_out, "w_ff_up": w_ff_up, "w_ff_conv": w_ff_conv,
        "b_ff_conv": b_ff_conv, "w_ff_down": w_ff_down,
    }
    depth = w_in.shape[0]
    bp, sp, _ = x_prompt.shape
    bs, ss, _ = x_sample.shape
    assert bs + 1 <= 16 and sp % ML_CHUNK == 0 and ss % TM == 0 and (bp * sp) % TM == 0

    cvec = jnp.concatenate([c_ctx[None, :], c, jnp.zeros((16 - 1 - bs, D_MODEL), F32)], axis=0)
    mod_all = _modulation(cvec, w_ada, b_ada).reshape(depth, 16, 6, D_MODEL)
    tabs = _rope_tables(ss)
    gf = g_final[None, :]

    xp = x_prompt.reshape(bp * sp, D_MODEL)
    xs = x_sample.reshape(bs * ss, D_MODEL)
    states = []
    for l in range(depth):
        wl = _prep_layer(l, params)
        final = l == depth - 1
        xp, st = _layer(xp, mod_all[l], sp, True, wl, None, None, gf, final)
        states.append(st)
        cache = (cache_mla_ckv[:, l], cache_mla_krope[:, l], cache_gqa_k[:, l], cache_gqa_v[:, l],
                 state_mlstm_C[:, l], state_mlstm_n[:, l], state_mlstm_m[:, l])
        xs, _ = _layer(xs, mod_all[l], ss, False, wl, tabs, cache, gf, final)
    new_state = tuple(jnp.stack([st[k] for st in states], axis=1) for k in range(7))
    return (xp.reshape(bp, sp, D_MODEL), xs.reshape(bs, ss, D_MODEL)) + new_state
```

```python
import functools

import jax
import jax.numpy as jnp
import numpy as np
from jax import lax
from jax.experimental import pallas as pl
from jax.experimental.pallas import tpu as pltpu

F32 = jnp.float32
BF16 = jnp.bfloat16

D_MODEL = 1024
GRID_W = 64
ROPE_THETA = 10000.0
EPS = 1e-6
MLA_HEADS = 4
MLA_NOPE = 64
MLA_ROPE = 32
MLA_V = 64
MLA_Q_RANK = 256
MLA_KV_RANK = 128
ML_HEADS = 4
ML_DH = 128
ML_WIDTH = ML_HEADS * ML_DH
ML_CHUNK = 128
GQA_HEADS = 4
GQA_KV_HEADS = 2
GQA_DH = 64
D_FF = 2816
MLA_SCALE = (MLA_NOPE + MLA_ROPE) ** -0.5
GQA_SCALE = GQA_DH ** -0.5
ML_K_SCALE = ML_DH ** -0.5
LOG2E = 1.4426950408889634

LANES = 128
HALO = 16
VMEM_LIMIT = 52 * 1024 * 1024

C_CQ, C_CKV, C_KR, C_U, C_V, C_O, C_G, C_QG, C_KG, C_VG, IN_COLS_P = (
    0, 256, 384, 512, 1024, 1536, 2048, 2304, 2560, 2688, 2816)
GATE_COLS = 2 * LANES
FF_CHUNK = 256
TM = 512
TQ = 512


def _cparams(sem):
    return pltpu.CompilerParams(dimension_semantics=sem, vmem_limit_bytes=VMEM_LIMIT)


def _lspec(arr, l):
    nd = arr.ndim - 1
    return pl.BlockSpec((None,) + arr.shape[1:], lambda *_: (l,) + (0,) * nd, pipeline_mode=pl.Buffered(1))


def _dot(a, b):
    return jnp.dot(a, b, preferred_element_type=F32)


def _dot_nt(a, b):
    return lax.dot_general(a, b, (((1,), (1,)), ((), ())), preferred_element_type=F32)


def _dot_tn(a, b):
    return lax.dot_general(a, b, (((0,), (0,)), ((), ())), preferred_element_type=F32)


def _rms(x, g):
    return (x * lax.rsqrt(jnp.mean(x * x, axis=-1, keepdims=True) + EPS)) * g


def _silu(x):
    return x * jax.nn.sigmoid(x)


def _adaln(x, g, shift, scale):
    return _rms(x, g) * (1.0 + scale) + shift


def _split2(x):
    hi = x.astype(BF16)
    lo = (x - hi.astype(F32)).astype(BF16)
    return hi, lo


def _split3(x):
    h1 = x.astype(BF16)
    r1 = x - h1.astype(F32)
    h2 = r1.astype(BF16)
    h3 = (r1 - h2.astype(F32)).astype(BF16)
    return h1, h2, h3


def _rope(x, cos, sa, sb, half):
    w = x.shape[-1]
    return x * cos + pltpu.roll(x, w - half, 1) * sa + pltpu.roll(x, half, 1) * sb


def _group_rms(x, gsum, g):
    hi, lo = _split2(x * x)
    ss = _dot(hi, gsum) + _dot(lo, gsum)
    return (x * lax.rsqrt(ss * (1.0 / GQA_DH) + EPS)) * g


def _mod_kernel(c_ref, w_ref, b_ref, o_ref):
    a = _silu(c_ref[...]).astype(BF16)
    o_ref[...] = _dot(a, w_ref[...].astype(BF16)) + b_ref[...]


def _modulation(cvec, w_ada, b_ada):
    nl = w_ada.shape[0]
    tn = 1536
    return pl.pallas_call(
        _mod_kernel,
        grid=(nl, 6 * D_MODEL // tn),
        in_specs=[
            pl.BlockSpec((16, D_MODEL), lambda l, j: (0, 0)),
            pl.BlockSpec((None, D_MODEL, tn), lambda l, j: (l, 0, j)),
            pl.BlockSpec((None, 1, tn), lambda l, j: (l, 0, j)),
        ],
        out_specs=pl.BlockSpec((None, 16, tn), lambda l, j: (l, 0, j)),
        out_shape=jax.ShapeDtypeStruct((nl, 16, 6 * D_MODEL), F32),
        compiler_params=_cparams(("arbitrary", "arbitrary")),
        name="modulation",
    )(cvec, w_ada, b_ada.reshape(nl, 1, 6 * D_MODEL))


def _in_proj_kernel(is_ctx, x_ref, mod_ref, g1_ref, win_ref, gq_ref, wuq_ref, gkv_ref, wukv_ref,
                    bg_ref, ggq_ref, ggk_ref, gsum_ref, *rest):
    if is_ctx:
        (qm_ref, km_ref, vm_ref, u_ref, vml_ref, oml_ref, gt_ref, qg_ref, kg_ref, vg_ref,
         ckv_ref, kr_ref) = rest
    else:
        (cm_ref, sam_ref, sbm_ref, cg_ref, sag_ref, sbg_ref,
         qm_ref, km_ref, vm_ref, u_ref, vml_ref, oml_ref, gt_ref, qg_ref, kg_ref, vg_ref) = rest

    y = _adaln(x_ref[...], g1_ref[...], mod_ref[0:1, :], mod_ref[1:2, :]).astype(BF16)

    def proj(c0, width):
        return _dot(y, win_ref[:, c0:c0 + width])

    cqn = _rms(proj(C_CQ, MLA_Q_RANK), gq_ref[...]).astype(BF16)
    qz = _dot(cqn, wuq_ref[...])
    for h in range(MLA_HEADS):
        blk = qz[:, LANES * h:LANES * (h + 1)]
        if not is_ctx:
            blk = _rope(blk, cm_ref[...], sam_ref[...], sbm_ref[...], 8)
        qm_ref[:, LANES * h:LANES * (h + 1)] = (blk * (MLA_SCALE * LOG2E)).astype(qm_ref.dtype)

    ckvn = _rms(proj(C_CKV, MLA_KV_RANK), gkv_ref[...])
    kr = proj(C_KR, LANES)
    if is_ctx:
        ckv_ref[...] = ckvn
        kr_ref[...] = kr
    else:
        kr = _rope(kr, cm_ref[...], sam_ref[...], sbm_ref[...], 8)
    kvz = _dot(ckvn.astype(BF16), wukv_ref[...])
    for h in range(MLA_HEADS):
        km_ref[:, LANES * h:LANES * (h + 1)] = (kvz[:, LANES * h:LANES * (h + 1)] + kr).astype(km_ref.dtype)
    vm_ref[...] = kvz[:, MLA_HEADS * LANES:].astype(vm_ref.dtype)

    u_ref[...] = proj(C_U, ML_WIDTH).astype(u_ref.dtype)
    vml_ref[...] = proj(C_V, ML_WIDTH).astype(vml_ref.dtype)
    oml_ref[...] = proj(C_O, ML_WIDTH).astype(oml_ref.dtype)
    gates = proj(C_G, GATE_COLS) + bg_ref[...]
    for pr in range(GATE_COLS // LANES):
        gt_ref[pr] = gates[:, LANES * pr:LANES * (pr + 1)].T[0:8, :]

    qg = _group_rms(proj(C_QG, GQA_HEADS * GQA_DH), gsum_ref[...], ggq_ref[...])
    kg = _group_rms(proj(C_KG, LANES), gsum_ref[0:LANES, 0:LANES], ggk_ref[...])
    if not is_ctx:
        qg = jnp.concatenate(
            [_rope(qg[:, 0:LANES], cg_ref[...], sag_ref[...], sbg_ref[...], 16),
             _rope(qg[:, LANES:], cg_ref[...], sag_ref[...], sbg_ref[...], 16)], axis=1)
        kg = _rope(kg, cg_ref[...], sag_ref[...], sbg_ref[...], 16)
    qg_ref[...] = (qg * (GQA_SCALE * LOG2E)).astype(qg_ref.dtype)
    kg_ref[...] = kg.astype(kg_ref.dtype)
    vg_ref[...] = proj(C_VG, LANES).astype(vg_ref.dtype)


def _mod_spec(l, seq, tm, is_ctx):
    if is_ctx:
        return pl.BlockSpec((None, None, 6, D_MODEL), lambda i: (l, 0, 0, 0))
    return pl.BlockSpec((None, None, 6, D_MODEL), lambda i: (l, 1 + (i * tm) // seq, 0, 0))


def _in_proj(x2, mod_all, l, seq, is_ctx, wts, tabs):
    m = x2.shape[0]
    tm = TM
    nt = m // tm
    row = lambda i: (i, 0)
    names = ["g_norm1", "w_in", "g_mla_q", "w_uq", "g_mla_kv", "w_ukv", "b_gates", "g_gqa_q", "g_gqa_k"]
    in_specs = ([pl.BlockSpec((tm, D_MODEL), row), _mod_spec(l, seq, tm, is_ctx)]
                + [_lspec(wts[n], l) for n in names]
                + [pl.BlockSpec((2 * LANES, 2 * LANES), lambda i: (0, 0))])
    args = [x2, mod_all] + [wts[n] for n in names] + [wts["gsum"]]
    if not is_ctx:
        tpos = seq // tm
        in_specs += [pl.BlockSpec((tm, LANES), lambda i: (i % tpos, 0))] * 6
        args += list(tabs)
    act = F32 if is_ctx else BF16
    outs = [
        (MLA_HEADS * LANES, BF16),
        (MLA_HEADS * LANES, BF16),
        (MLA_HEADS * MLA_V, BF16),
        (ML_WIDTH, F32),
        (ML_WIDTH, BF16),
        (ML_WIDTH, F32),
        (None, F32),
        (GQA_HEADS * GQA_DH, BF16),
        (LANES, act),
        (LANES, act),
    ]
    if is_ctx:
        outs += [(MLA_KV_RANK, F32), (LANES, F32)]
    return pl.pallas_call(
        functools.partial(_in_proj_kernel, is_ctx),
        grid=(nt,),
        in_specs=in_specs,
        out_specs=[pl.BlockSpec((GATE_COLS // LANES, 8, tm), lambda i: (0, 0, i)) if w is None
                   else pl.BlockSpec((tm, w), row) for w, _ in outs],
        out_shape=[jax.ShapeDtypeStruct((GATE_COLS // LANES, 8, m) if w is None else (m, w), dt)
                   for w, dt in outs],
        compiler_params=_cparams(("parallel",)),
        name="in_proj_ctx" if is_ctx else "in_proj_lat",
    )(*args)


def _cache_kv_kernel(ckv_ref, kr_ref, wukv_ref, k_ref, v_ref):
    kvz = _dot(ckv_ref[...].astype(BF16), wukv_ref[...])
    kr = kr_ref[...]
    for h in range(MLA_HEADS):
        k_ref[:, LANES * h:LANES * (h + 1)] = (kvz[:, LANES * h:LANES * (h + 1)] + kr).astype(k_ref.dtype)
    v_ref[...] = kvz[:, MLA_HEADS * LANES:].astype(v_ref.dtype)


def _cache_kv(ckv, kr_pad, l, w_ukv):
    b, _, tc, _ = ckv.shape
    cache_blk = pl.BlockSpec((None, None, tc, LANES), lambda i: (i, l, 0, 0))
    return pl.pallas_call(
        _cache_kv_kernel,
        grid=(b,),
        in_specs=[cache_blk, cache_blk, _lspec(w_ukv, l)],
        out_specs=[pl.BlockSpec((None, tc, MLA_HEADS * LANES), lambda i: (i, 0, 0)),
                   pl.BlockSpec((None, tc, MLA_HEADS * MLA_V), lambda i: (i, 0, 0))],
        out_shape=[jax.ShapeDtypeStruct((b, tc, MLA_HEADS * LANES), BF16),
                   jax.ShapeDtypeStruct((b, tc, MLA_HEADS * MLA_V), BF16)],
        compiler_params=_cparams(("parallel",)),
        name="cache_kv",
    )(ckv, kr_pad, w_ukv)


def _attn_kernel(is_mla, nseg, q_ref, *refs):
    kv = refs[:2 * nseg]
    o_ref = refs[2 * nseg]
    tq = q_ref.shape[0]
    lo = lax.broadcasted_iota(jnp.int32, (tq, LANES), 1) < (LANES // 2)
    heads = []
    for h in range(4):
        if is_mla:
            qh = q_ref[:, LANES * h:LANES * (h + 1)]
            ksl = slice(LANES * h, LANES * (h + 1))
            vsl = slice(LANES * (h // 2), LANES * (h // 2 + 1))
        else:
            blk = q_ref[:, LANES * (h % 2):LANES * (h % 2 + 1)].astype(F32)
            qh = jnp.where(lo if h < 2 else jnp.logical_not(lo), blk, 0.0).astype(BF16)
            ksl = slice(0, LANES)
            vsl = slice(0, LANES)
        ss = [_dot_nt(qh, kv[2 * j][:, ksl].astype(BF16)) for j in range(nseg)]
        mx = jnp.max(ss[0], axis=-1, keepdims=True)
        for s in ss[1:]:
            mx = jnp.maximum(mx, jnp.max(s, axis=-1, keepdims=True))
        den = None
        acc = None
        for j, s in enumerate(ss):
            e = jnp.exp2(s - mx)
            d = jnp.sum(e, axis=-1, keepdims=True)
            a = _dot(e.astype(BF16), kv[2 * j + 1][:, vsl].astype(BF16))
            den = d if den is None else den + d
            acc = a if acc is None else acc + a
        heads.append(acc / den)
    if is_mla:
        o_ref[:, 0:LANES] = jnp.where(lo, heads[0], heads[1]).astype(o_ref.dtype)
        o_ref[:, LANES:] = jnp.where(lo, heads[2], heads[3]).astype(o_ref.dtype)
    else:
        half = LANES // 2
        o_ref[:, 0:LANES] = jnp.where(lo, heads[0], pltpu.roll(heads[1], half, 1)).astype(o_ref.dtype)
        o_ref[:, LANES:] = jnp.where(lo, pltpu.roll(heads[2], half, 1), heads[3]).astype(o_ref.dtype)


def _attention(is_mla, q, segs):
    b, tq_all, wq = q.shape
    tq = min(TQ, tq_all)
    in_specs = [pl.BlockSpec((None, tq, wq), lambda i, j: (i, j, 0))]
    args = [q]
    for k, v, layer in segs:
        for a in (k, v):
            if layer is None:
                in_specs.append(pl.BlockSpec((None,) + a.shape[1:], lambda i, j: (i, 0, 0)))
            else:
                in_specs.append(pl.BlockSpec((None, None) + a.shape[2:], lambda i, j, layer=layer: (i, layer, 0, 0)))
            args.append(a)
    return pl.pallas_call(
        functools.partial(_attn_kernel, is_mla, len(segs)),
        grid=(b, tq_all // tq),
        in_specs=in_specs,
        out_specs=pl.BlockSpec((None, tq, 2 * LANES), lambda i, j: (i, j, 0)),
        out_shape=jax.ShapeDtypeStruct((b, tq_all, 2 * LANES), BF16),
        compiler_params=_cparams(("parallel", "parallel")),
        name=("attn_mla" if is_mla else "attn_gqa") + ("_lat" if len(segs) > 1 else "_ctx"),
    )(*args)


def _log_sigmoid(x):
    return jnp.minimum(x, 0.0) - jnp.log(1.0 + jnp.exp(-jnp.abs(x)))


HPS = 2


def _mlstm_kernel(has_init, nc, u_ref, v_ref, o_ref, gt_ref, cw_ref, cb_ref, wqt_ref, wk_ref, go_ref, *rest):
    if has_init:
        ct0_ref, n0_ref, m0_ref, out_ref, qt_s, k_s, vt_s, ht_f, ht_b, acol_s, row_s, st_s = rest
    else:
        out_ref, cfin_ref, sfin_ref, qt_s, k_s, vt_s, ht_f, ht_b, acol_s, row_s, st_s = rest
    t = nc * ML_CHUNK
    sq = (ML_CHUNK, ML_CHUNK)
    w2 = HPS * LANES

    u = u_ref[...]
    rows = lax.broadcasted_iota(jnp.int32, (t, w2), 0)
    up = jnp.where(rows == 0, 0.0, pltpu.roll(u, 1, 0))
    un = jnp.where(rows == t - 1, 0.0, pltpu.roll(u, t - 1, 0))
    uc = _silu(cw_ref[0:1, :] * up + cw_ref[1:2, :] * u + cw_ref[2:3, :] * un + cb_ref[...]).astype(BF16)
    r_i = lax.broadcasted_iota(jnp.int32, sq, 0)
    c_i = lax.broadcasted_iota(jnp.int32, sq, 1)
    lower = c_i <= r_i
    upper = c_i >= r_i
    eye = jnp.where(c_i == r_i, 1.0, 0.0).astype(BF16)
    for j in range(HPS):
        hs = slice(LANES * j, LANES * (j + 1))
        qt_s[hs, :] = _dot_nt(wqt_ref[j], uc[:, hs]).astype(BF16)
        k_s[:, hs] = _dot(uc[:, hs], wk_ref[j]) * ML_K_SCALE
        vt_s[hs, :] = _dot_nt(eye, v_ref[:, hs]).astype(BF16)

    tri_lo = jnp.where(lower, 1.0, 0.0).astype(BF16)
    tri_up = jnp.where(upper, 1.0, 0.0).astype(BF16)
    gates = gt_ref[...]
    row8 = lax.broadcasted_iota(jnp.int32, (8, ML_CHUNK), 0)
    lg_all = jnp.where(lax.broadcasted_iota(jnp.int32, (8, t), 0) % 4 >= 2, _log_sigmoid(gates), gates)
    for c in range(nc):
        sl = slice(c * ML_CHUNK, (c + 1) * ML_CHUNK)
        lg = lg_all[:, sl]
        h1, h2, h3 = _split3(lg)
        cs = jnp.where(row8 < 4, _dot(h1, tri_up) + _dot(h2, tri_up) + _dot(h3, tri_up),
                       _dot(h1, tri_lo) + _dot(h2, tri_lo) + _dot(h3, tri_lo)) * LOG2E
        row_s[:, sl] = cs
        a = lg * LOG2E - pltpu.roll(cs, 6, 0)
        for j in range(HPS):
            hs = slice(LANES * j, LANES * (j + 1))
            st_s[j, sl, :] = _dot(k_s[sl, hs].astype(BF16), qt_s[hs, sl])
            for d in range(2):
                r = 4 * d + j
                acol_s[2 * j + d, sl, :] = jnp.broadcast_to(a[r:r + 1, :], sq).T

    def chunk(j, d, c0, ct, n, m2):
        hs = slice(LANES * j, LANES * (j + 1))
        cs_ = pl.ds(c0, ML_CHUNK)
        last = 0 if d else ML_CHUNK - 1
        qt = qt_s[hs, cs_]
        k = k_s[cs_, hs]
        vt = vt_s[hs, cs_]
        acol = acol_s[2 * j + d, cs_, :]
        bc = row_s[2 + 4 * d + j:3 + 4 * d + j, cs_]
        g = bc + m2
        dlog = jnp.where(lower if d else upper, acol + bc, -jnp.inf)
        m_t = jnp.maximum(g, jnp.max(dlog, axis=0, keepdims=True))
        w = jnp.exp2(dlog - m_t)
        inter = jnp.exp2(g - m_t)
        st = st_s[j, cs_, :] * w
        qn = _dot(jnp.broadcast_to(n, (8, LANES)).astype(BF16), qt)[0:1, :]
        den = jnp.sum(st, axis=0, keepdims=True) + inter * qn
        numt = _dot(vt, st.astype(BF16)) + inter * _dot(ct.astype(BF16), qt)
        ht = numt * (1.0 / jnp.maximum(jnp.abs(den), jnp.exp2(-m_t)))
        b_last = jnp.broadcast_to(bc[:, last:last + 1], (1, LANES))
        m_new = jnp.broadcast_to(m_t[:, last:last + 1], (1, LANES))
        kw = k * jnp.exp2(b_last + acol - m_new)
        decay = jnp.exp2(b_last + m2 - m_new)
        ct_new = decay * ct + _dot(vt, kw.astype(BF16))
        n_new = decay * n + jnp.sum(kw, axis=0, keepdims=True)
        return ht, ct_new, n_new, m_new

    def step(i, carry):
        cf = i * ML_CHUNK
        cb = (nc - 1 - i) * ML_CHUNK
        if not isinstance(i, int):
            cf = pl.multiple_of(cf, ML_CHUNK)
            cb = pl.multiple_of(cb, ML_CHUNK)
        new = []
        for j in range(HPS):
            hs = slice(LANES * j, LANES * (j + 1))
            for d, (c0, ht_s) in enumerate(((cf, ht_f), (cb, ht_b))):
                ht, ct, n, m2 = chunk(j, d, c0, *carry[2 * j + d])
                ht_s[hs, pl.ds(c0, ML_CHUNK)] = ht
                new.append((ct, n, m2))
        return tuple(new)

    if has_init:
        init = tuple((ct0_ref[d, j], n0_ref[j, d:d + 1, :], m0_ref[j, d:d + 1, :] * LOG2E)
                     for j in range(HPS) for d in range(2))
    else:
        init = tuple((jnp.zeros(sq, F32), jnp.zeros((1, LANES), F32), jnp.zeros((1, LANES), F32))
                     for j in range(HPS) for d in range(2))
    if nc <= 2:
        carry = init
        for i in range(nc):
            carry = step(i, carry)
    else:
        carry = lax.fori_loop(0, nc, step, init, unroll=4)

    hsum = (ht_f[...] + ht_b[...]).T
    for j in range(HPS):
        hs = slice(LANES * j, LANES * (j + 1))
        hn = _rms(hsum[:, hs], go_ref[:, hs])
        out_ref[:, hs] = (hn * jax.nn.sigmoid(o_ref[:, hs])).astype(out_ref.dtype)

    if not has_init:
        for j in range(HPS):
            for d in range(2):
                ct, n, m2 = carry[2 * j + d]
                cfin_ref[d, j] = ct.T
            nf, nb = carry[2 * j][1], carry[2 * j + 1][1]
            mf, mb = carry[2 * j][2], carry[2 * j + 1][2]
            sfin_ref[j] = jnp.concatenate(
                [nf, nb, mf * (1.0 / LOG2E), mb * (1.0 / LOG2E), jnp.zeros((4, LANES), F32)], axis=0)


def _mlstm(u, v_ml, o_ml, gates, l, wts, init):
    b, t, _ = u.shape
    nc = t // ML_CHUNK
    has_init = init is not None
    w2 = HPS * LANES
    pair_blk = lambda i, p: (i, 0, p)
    in_specs = [
        pl.BlockSpec((None, t, w2), pair_blk),
        pl.BlockSpec((None, t, w2), pair_blk),
        pl.BlockSpec((None, t, w2), pair_blk),
        pl.BlockSpec((None, 8, t), lambda i, p: (p, 0, i)),
        pl.BlockSpec((None, 3, w2), lambda i, p: (l, 0, p)),
        pl.BlockSpec((None, 1, w2), lambda i, p: (l, 0, p)),
        pl.BlockSpec((None, HPS, ML_DH, ML_DH), lambda i, p: (l, p, 0, 0)),
        pl.BlockSpec((None, HPS, ML_DH, ML_DH), lambda i, p: (l, p, 0, 0)),
        pl.BlockSpec((None, 1, w2), lambda i, p: (l, 0, p)),
    ]
    args = [u, v_ml, o_ml, gates, wts["w_ml_conv"], wts["b_ml_conv"], wts["w_ml_qt"], wts["w_ml_k"],
            wts["g_ml_out"]]
    out_specs = [pl.BlockSpec((None, t, w2), pair_blk)]
    out_shape = [jax.ShapeDtypeStruct((b, t, ML_WIDTH), BF16)]
    state_c = pl.BlockSpec((None, 2, HPS, ML_DH, ML_DH), lambda i, p: (i, 0, p, 0, 0))
    state_s = pl.BlockSpec((None, HPS, 8, LANES), lambda i, p: (i, p, 0, 0))
    if has_init:
        in_specs += [pl.BlockSpec((None, None, 2, HPS, ML_DH, ML_DH), lambda i, p: (i, l, 0, p, 0, 0)),
                     pl.BlockSpec((None, None, HPS, 8, LANES), lambda i, p: (i, l, p, 0, 0)),
                     pl.BlockSpec((None, None, HPS, 8, LANES), lambda i, p: (i, l, p, 0, 0))]
        args += list(init)
    else:
        out_specs += [state_c, state_s]
        out_shape += [
            jax.ShapeDtypeStruct((b, 2, ML_HEADS, ML_DH, ML_DH), F32),
            jax.ShapeDtypeStruct((b, ML_HEADS, 8, LANES), F32),
        ]
    return pl.pallas_call(
        functools.partial(_mlstm_kernel, has_init, nc),
        grid=(b, ML_HEADS // HPS),
        in_specs=in_specs,
        out_specs=out_specs,
        out_shape=out_shape,
        scratch_shapes=[
            pltpu.VMEM((w2, t), BF16),
            pltpu.VMEM((t, w2), F32),
            pltpu.VMEM((w2, t), BF16),
            pltpu.VMEM((w2, t), F32),
            pltpu.VMEM((w2, t), F32),
            pltpu.VMEM((2 * HPS, t, LANES), F32),
            pltpu.VMEM((8, t), F32),
            pltpu.VMEM((HPS, t, LANES), F32),
        ],
        compiler_params=_cparams(("parallel", "parallel")),
        name="mlstm_lat" if has_init else "mlstm_ctx",
    )(*args)


def _ffn_kernel(seq, tm, final, *refs):
    x_refs, oa_refs, ob_refs, oc_refs = refs[0:3], refs[3:6], refs[6:9], refs[9:12]
    (mod_ref, g2_ref, wout_ref, wup_ref, cw_ref, cb_ref, wdn_ref, gf_ref, y_ref,
     oext_s, yext_s, gext_s, h_s) = refs[12:]
    i = pl.program_id(0)
    a_w = MLA_HEADS * MLA_V
    ext = ((0, HALO), (HALO, tm), (HALO + tm, HALO))
    for (r0, n), k in zip(ext, (1, 0, 2)):
        oext_s[r0:r0 + n, 0:a_w] = oa_refs[k][...]
        oext_s[r0:r0 + n, a_w:a_w + ML_WIDTH] = ob_refs[k][...]
        oext_s[r0:r0 + n, a_w + ML_WIDTH:] = oc_refs[k][...]
    mix = _dot(oext_s[...], wout_ref[...])
    gate1 = mod_ref[2:3, :]
    x = x_refs[0][...] + gate1 * mix[HALO:HALO + tm, :]
    xp = x_refs[1][...] + gate1 * mix[0:HALO, :]
    xn = x_refs[2][...] + gate1 * mix[HALO + tm:, :]

    g2 = g2_ref[...]
    shift = mod_ref[3:4, :]
    scale = mod_ref[4:5, :]
    keep_p = jnp.where((i * tm) % seq == 0, 0.0, 1.0)
    keep_n = jnp.where(((i + 1) * tm) % seq == 0, 0.0, 1.0)
    yext_s[0:HALO, :] = (_adaln(xp, g2, shift, scale) * keep_p).astype(BF16)
    yext_s[HALO:HALO + tm, :] = _adaln(x, g2, shift, scale).astype(BF16)
    yext_s[HALO + tm:, :] = (_adaln(xn, g2, shift, scale) * keep_n).astype(BF16)

    for c in range(D_FF // FF_CHUNK):
        c0 = c * FF_CHUNK
        a = _dot(yext_s[HALO:HALO + tm, :], wup_ref[:, c0:c0 + FF_CHUNK])
        gext_s[...] = _dot(yext_s[...], wup_ref[:, D_FF + c0:D_FF + c0 + FF_CHUNK])
        g = (cw_ref[0:1, c0:c0 + FF_CHUNK] * gext_s[HALO - 1:HALO - 1 + tm, :]
             + cw_ref[1:2, c0:c0 + FF_CHUNK] * gext_s[HALO:HALO + tm, :]
             + cw_ref[2:3, c0:c0 + FF_CHUNK] * gext_s[HALO + 1:HALO + 1 + tm, :]
             + cb_ref[:, c0:c0 + FF_CHUNK])
        h_s[:, c0:c0 + FF_CHUNK] = (_silu(g) * a).astype(BF16)
    y = x + mod_ref[5:6, :] * _dot(h_s[...], wdn_ref[...])
    if final:
        y = _rms(y, gf_ref[...])
    y_ref[...] = y


def _ffn(x2, o_a, o_b, o_c, mod_all, l, seq, is_ctx, wts, g_final, final):
    m = x2.shape[0]
    tm = min(TM, seq)
    nb = tm // HALO
    last_blk = m // HALO - 1
    row = lambda i: (i, 0)
    prev = lambda i: (jnp.maximum(i * nb - 1, 0), 0)
    nxt = lambda i: (jnp.minimum((i + 1) * nb, last_blk), 0)
    in_specs, args = [], []
    for a in (x2, o_a, o_b, o_c):
        w = a.shape[1]
        in_specs += [pl.BlockSpec((tm, w), row), pl.BlockSpec((HALO, w), prev), pl.BlockSpec((HALO, w), nxt)]
        args += [a, a, a]
    names = ["g_norm2", "w_out", "w_ff_up", "w_ff_conv", "b_ff_conv", "w_ff_down"]
    in_specs += ([_mod_spec(l, seq, tm, is_ctx)] + [_lspec(wts[n], l) for n in names]
                 + [pl.BlockSpec((1, D_MODEL), lambda i: (0, 0))])
    args += [mod_all] + [wts[n] for n in names] + [g_final]
    return pl.pallas_call(
        functools.partial(_ffn_kernel, seq, tm, final),
        grid=(m // tm,),
        in_specs=in_specs,
        out_specs=pl.BlockSpec((tm, D_MODEL), row),
        out_shape=jax.ShapeDtypeStruct((m, D_MODEL), F32),
        scratch_shapes=[pltpu.VMEM((tm + 2 * HALO, D_MODEL), BF16),
                        pltpu.VMEM((tm + 2 * HALO, D_MODEL), BF16),
                        pltpu.VMEM((tm + 2 * HALO, FF_CHUNK), F32),
                        pltpu.VMEM((tm, D_FF), BF16)],
        compiler_params=_cparams(("parallel",)),
        name="ffn_ctx" if is_ctx else "ffn_lat",
    )(*args)


def _rope_tables(t):
    pos = np.arange(t)
    rows = (pos // GRID_W).astype(np.float32)
    cols = (pos % GRID_W).astype(np.float32)

    def group(p, d):
        inv = (np.float32(ROPE_THETA) ** (-np.arange(0, d, 2, dtype=np.float32) / np.float32(d))).astype(np.float32)
        ang = (p[:, None] * inv[None, :]).astype(np.float32)
        cs, sn, z = np.cos(ang), np.sin(ang), np.zeros_like(ang)
        return (np.concatenate([cs, cs], 1), np.concatenate([-sn, z], 1), np.concatenate([z, sn], 1))

    gr, gc = group(rows, MLA_ROPE // 2), group(cols, MLA_ROPE // 2)
    ones = np.ones((t, MLA_NOPE), np.float32)
    zeros = np.zeros((t, MLA_NOPE), np.float32)
    pad1 = np.ones((t, LANES - MLA_NOPE - MLA_ROPE), np.float32)
    pad0 = np.zeros((t, LANES - MLA_NOPE - MLA_ROPE), np.float32)
    mla = (np.concatenate([ones, gr[0], gc[0], pad1], 1),
           np.concatenate([zeros, gr[1], gc[1], pad0], 1),
           np.concatenate([zeros, gr[2], gc[2], pad0], 1))
    gr, gc = group(rows, GQA_DH // 2), group(cols, GQA_DH // 2)
    gqa = tuple(np.concatenate([gr[k], gc[k], gr[k], gc[k]], 1) for k in range(3))
    return tuple(jnp.asarray(a, F32) for a in mla + gqa)


def _prep_weights(p):
    w_in = p["w_in"]
    nl = w_in.shape[0]
    qg = w_in[:, :, 1968:2224].reshape(nl, D_MODEL, GQA_HEADS, GQA_DH)[:, :, np.array([0, 2, 1, 3])]
    gsrc = np.array([kind * ML_HEADS + HPS * pr + j for pr in range(2) for kind in range(4) for j in range(HPS)])
    gdst = np.array([LANES * pr + HPS * kind + j for pr in range(2) for kind in range(4) for j in range(HPS)])
    w_gates = jnp.zeros((nl, D_MODEL, GATE_COLS), F32).at[:, :, gdst].set(w_in[:, :, 1952 + gsrc])
    b_gates = jnp.zeros((nl, GATE_COLS), F32).at[:, gdst].set(p["b_ml_gates"][:, gsrc])
    w_in_p = jnp.concatenate([
        w_in[:, :, 0:384],
        jnp.pad(w_in[:, :, 384:416], ((0, 0), (0, 0), (MLA_NOPE, LANES - MLA_NOPE - MLA_ROPE))),
        w_in[:, :, 416:1952],
        w_gates,
        qg.reshape(nl, D_MODEL, GQA_HEADS * GQA_DH),
        w_in[:, :, 2224:2480],
    ], axis=2).astype(BF16)
    w_uq = p["w_mla_uq"].reshape(nl, MLA_Q_RANK, MLA_HEADS, MLA_NOPE + MLA_ROPE)
    w_uq = jnp.pad(w_uq, ((0, 0), (0, 0), (0, 0), (0, LANES - MLA_NOPE - MLA_ROPE)))
    w_ukv = p["w_mla_ukv"].reshape(nl, MLA_KV_RANK, MLA_HEADS, MLA_NOPE + MLA_V)
    w_uk = jnp.pad(w_ukv[..., :MLA_NOPE], ((0, 0), (0, 0), (0, 0), (0, LANES - MLA_NOPE)))
    w_ukv_p = jnp.concatenate([w_uk.reshape(nl, MLA_KV_RANK, -1),
                               w_ukv[..., MLA_NOPE:].reshape(nl, MLA_KV_RANK, -1)], axis=2)
    grp = np.arange(2 * LANES) // GQA_DH
    row = lambda a: a[:, None, :]
    return {
        "g_norm1": row(p["g_norm1"]),
        "g_norm2": row(p["g_norm2"]),
        "w_in": w_in_p,
        "g_mla_q": row(p["g_mla_q"]),
        "w_uq": w_uq.reshape(nl, MLA_Q_RANK, -1).astype(BF16),
        "g_mla_kv": row(p["g_mla_kv"]),
        "w_ukv": w_ukv_p.astype(BF16),
        "b_gates": row(b_gates),
        "g_gqa_q": row(jnp.tile(p["g_gqa_q"], (1, GQA_HEADS))),
        "g_gqa_k": row(jnp.tile(p["g_gqa_k"], (1, GQA_KV_HEADS))),
        "gsum": jnp.asarray(grp[:, None] == grp[None, :], BF16),
        "w_ml_conv": p["w_ml_conv"],
        "b_ml_conv": row(p["b_ml_conv"]),
        "w_ml_qt": jnp.swapaxes(p["w_ml_q"], 2, 3).astype(BF16),
        "w_ml_k": p["w_ml_k"].astype(BF16),
        "g_ml_out": row(p["g_ml_out"]),
        "w_out": p["w_out"].astype(BF16),
        "w_ff_up": p["w_ff_up"].astype(BF16),
        "w_ff_conv": p["w_ff_conv"],
        "b_ff_conv": row(p["b_ff_conv"]),
        "w_ff_down": p["w_ff_down"].astype(BF16),
    }


def _layer(x2, mod_all, l, seq, is_ctx, wts, tabs, cache, g_final, final):
    m = x2.shape[0]
    b = m // seq
    r3 = lambda a: a.reshape(b, seq, a.shape[-1])
    outs = _in_proj(x2, mod_all, l, seq, is_ctx, wts, tabs)
    q_m, k_m, v_m, u, v_ml, o_ml, gates, q_g, k_g, v_g = outs[:10]
    segs_a = [(r3(k_m), r3(v_m), None)]
    segs_c = [(r3(k_g), r3(v_g), None)]
    init = None
    if not is_ctx:
        ckv_c, kr_pad, kg_c, vg_c, ct0, n0p, m0p = cache
        kc, vc = _cache_kv(ckv_c, kr_pad, l, wts["w_ukv"])
        segs_a.append((kc, vc, None))
        segs_c.append((kg_c, vg_c, l))
        init = (ct0, n0p, m0p)
    o_a = _attention(True, r3(q_m), segs_a)
    o_c = _attention(False, r3(q_g), segs_c)
    ml = _mlstm(r3(u), r3(v_ml), r3(o_ml), gates, l, wts, init)
    x2 = _ffn(x2, o_a.reshape(m, -1), ml[0].reshape(m, -1), o_c.reshape(m, -1), mod_all, l, seq, is_ctx, wts,
              g_final, final)
    state = None
    if is_ctx:
        ckv_n, kr_raw = outs[10:12]
        sfin = ml[2]
        state = (
            r3(ckv_n),
            r3(kr_raw)[:, :, MLA_NOPE:MLA_NOPE + MLA_ROPE],
            k_g.reshape(b, seq, GQA_KV_HEADS, GQA_DH),
            v_g.reshape(b, seq, GQA_KV_HEADS, GQA_DH),
            ml[1],
            jnp.swapaxes(sfin[:, :, 0:2, :], 1, 2),
            jnp.swapaxes(sfin[:, :, 2:4, 0], 1, 2),
        )
    return x2, state


def kernel(x_prompt, x_sample, cache_mla_ckv, cache_mla_krope, cache_gqa_k, cache_gqa_v, state_mlstm_C, state_mlstm_n, state_mlstm_m, c, c_ctx, w_ada, b_ada, g_norm1, g_norm2, w_in, g_mla_q, w_mla_uq, g_mla_kv, w_mla_ukv, w_ml_conv, b_ml_conv, w_ml_q, w_ml_k, b_ml_gates, g_ml_out, g_gqa_q, g_gqa_k, w_out, w_ff_up, w_ff_conv, b_ff_conv, w_ff_down, g_final):
    params = {
        "g_norm1": g_norm1, "g_norm2": g_norm2, "w_in": w_in, "g_mla_q": g_mla_q, "w_mla_uq": w_mla_uq,
        "g_mla_kv": g_mla_kv, "w_mla_ukv": w_mla_ukv, "w_ml_conv": w_ml_conv, "b_ml_conv": b_ml_conv,
        "w_ml_q": w_ml_q, "w_ml_k": w_ml_k, "b_ml_gates": b_ml_gates, "g_ml_out": g_ml_out,
        "g_gqa_q": g_gqa_q, "g_gqa_k": g_gqa_k, "w_out": w_out, "w_ff_up": w_ff_up, "w_ff_conv": w_ff_conv,
        "b_ff_conv": b_ff_conv, "w_ff_down": w_ff_down,
    }
    depth = w_in.shape[0]
    bp, sp, _ = x_prompt.shape
    bs, ss, _ = x_sample.shape
    assert bs + 1 <= 16 and sp % ML_CHUNK == 0 and ss % TM == 0 and (bp * sp) % TM == 0

    cvec = jnp.concatenate([c_ctx[None, :], c, jnp.zeros((16 - 1 - bs, D_MODEL), F32)], axis=0)
    mod_all = _modulation(cvec, w_ada, b_ada).reshape(depth, 16, 6, D_MODEL)
    tabs = _rope_tables(ss)
    gf = g_final[None, :]
    wts = _prep_weights(params)

    tc = cache_mla_ckv.shape[2]
    pad_rows = ((0, 0), (0, 0), (0, 0), (0, 8 - 2), (0, 0))
    cache = (
        cache_mla_ckv,
        jnp.pad(cache_mla_krope, ((0, 0), (0, 0), (0, 0), (MLA_NOPE, LANES - MLA_NOPE - MLA_ROPE))),
        cache_gqa_k.reshape(bs, depth, tc, GQA_KV_HEADS * GQA_DH),
        cache_gqa_v.reshape(bs, depth, tc, GQA_KV_HEADS * GQA_DH),
        jnp.swapaxes(state_mlstm_C, -1, -2),
        jnp.pad(jnp.swapaxes(state_mlstm_n, 2, 3), pad_rows),
        jnp.pad(jnp.broadcast_to(jnp.swapaxes(state_mlstm_m, 2, 3)[..., None], (bs, depth, ML_HEADS, 2, LANES)),
                pad_rows),
    )

    xp = x_prompt.reshape(bp * sp, D_MODEL)
    xs = x_sample.reshape(bs * ss, D_MODEL)
    states = []
    for l in range(depth):
        final = l == depth - 1
        xp, st = _layer(xp, mod_all, l, sp, True, wts, None, None, gf, final)
        states.append(st)
        xs, _ = _layer(xs, mod_all, l, ss, False, wts, tabs, cache, gf, final)
    new_state = tuple(jnp.stack([st[k] for st in states], axis=1) for k in range(7))
    return (xp.reshape(bp, sp, D_MODEL), xs.reshape(bs, ss, D_MODEL)) + new_state
```

```python
import functools

import jax
import jax.numpy as jnp
import numpy as np
from jax import lax
from jax.experimental import pallas as pl
from jax.experimental.pallas import tpu as pltpu

F32 = jnp.float32
BF16 = jnp.bfloat16

D_MODEL = 1024
GRID_W = 64
ROPE_THETA = 10000.0
EPS = 1e-6
MLA_HEADS = 4
MLA_NOPE = 64
MLA_ROPE = 32
MLA_V = 64
MLA_Q_RANK = 256
MLA_KV_RANK = 128
ML_HEADS = 4
ML_DH = 128
ML_WIDTH = ML_HEADS * ML_DH
ML_CHUNK = 128
GQA_HEADS = 4
GQA_KV_HEADS = 2
GQA_DH = 64
D_FF = 2816
MLA_SCALE = (MLA_NOPE + MLA_ROPE) ** -0.5
GQA_SCALE = GQA_DH ** -0.5
ML_K_SCALE = ML_DH ** -0.5
LOG2E = 1.4426950408889634

LANES = 128
HALO = 16
VMEM_LIMIT = 52 * 1024 * 1024

C_CQ, C_CKV, C_KR, C_U, C_V, C_O, C_G, C_QG, C_KG, C_VG, IN_COLS_P = (
    0, 256, 384, 512, 1024, 1536, 2048, 2304, 2560, 2688, 2816)
GATE_COLS = 2 * LANES
FF_CHUNK = 256
TM = 512
TQ = 512


def _cparams(sem):
    return pltpu.CompilerParams(dimension_semantics=sem, vmem_limit_bytes=VMEM_LIMIT)


def _lspec(arr, l):
    nd = arr.ndim - 1
    return pl.BlockSpec((None,) + arr.shape[1:], lambda *_: (l,) + (0,) * nd, pipeline_mode=pl.Buffered(1))


def _dot(a, b):
    return jnp.dot(a, b, preferred_element_type=F32)


def _dot_nt(a, b):
    return lax.dot_general(a, b, (((1,), (1,)), ((), ())), preferred_element_type=F32)


def _dot_tn(a, b):
    return lax.dot_general(a, b, (((0,), (0,)), ((), ())), preferred_element_type=F32)


def _rms(x, g):
    return (x * lax.rsqrt(jnp.mean(x * x, axis=-1, keepdims=True) + EPS)) * g


def _silu(x):
    return x * jax.nn.sigmoid(x)


def _adaln(x, g, shift, scale):
    return _rms(x, g) * (1.0 + scale) + shift


def _split2(x):
    hi = x.astype(BF16)
    lo = (x - hi.astype(F32)).astype(BF16)
    return hi, lo


def _split3(x):
    h1 = x.astype(BF16)
    r1 = x - h1.astype(F32)
    h2 = r1.astype(BF16)
    h3 = (r1 - h2.astype(F32)).astype(BF16)
    return h1, h2, h3


def _rope(x, cos, sa, sb, half):
    w = x.shape[-1]
    return x * cos + pltpu.roll(x, w - half, 1) * sa + pltpu.roll(x, half, 1) * sb


def _group_rms(x, gsum, g):
    hi, lo = _split2(x * x)
    ss = _dot(hi, gsum) + _dot(lo, gsum)
    return (x * lax.rsqrt(ss * (1.0 / GQA_DH) + EPS)) * g


def _mod_kernel(c_ref, w_ref, b_ref, o_ref):
    a = _silu(c_ref[...]).astype(BF16)
    o_ref[...] = _dot(a, w_ref[...].astype(BF16)) + b_ref[...]


def _modulation(cvec, w_ada, b_ada):
    nl = w_ada.shape[0]
    tn = 1536
    return pl.pallas_call(
        _mod_kernel,
        grid=(nl, 6 * D_MODEL // tn),
        in_specs=[
            pl.BlockSpec((16, D_MODEL), lambda l, j: (0, 0)),
            pl.BlockSpec((None, D_MODEL, tn), lambda l, j: (l, 0, j)),
            pl.BlockSpec((None, 1, tn), lambda l, j: (l, 0, j)),
        ],
        out_specs=pl.BlockSpec((None, 16, tn), lambda l, j: (l, 0, j)),
        out_shape=jax.ShapeDtypeStruct((nl, 16, 6 * D_MODEL), F32),
        compiler_params=_cparams(("arbitrary", "arbitrary")),
        name="modulation",
    )(cvec, w_ada, b_ada.reshape(nl, 1, 6 * D_MODEL))


def _in_proj_kernel(is_ctx, seq, x_ref, xp_ref, xn_ref, mod_ref, g1_ref, win_ref, gq_ref, wuq_ref, gkv_ref,
                    wukv_ref, bg_ref, ggq_ref, ggk_ref, cw_ref, cb_ref, gsum_ref, *rest):
    if is_ctx:
        (qm_ref, km_ref, vm_ref, u_ref, vml_ref, oml_ref, gt_ref, qg_ref, kg_ref, vg_ref,
         ckv_ref, kr_ref, yext_s, uext_s) = rest
    else:
        (cm_ref, sam_ref, sbm_ref, cg_ref, sag_ref, sbg_ref,
         qm_ref, km_ref, vm_ref, u_ref, vml_ref, oml_ref, gt_ref, qg_ref, kg_ref, vg_ref, yext_s, uext_s) = rest
    tm = x_ref.shape[0]

    for r0, ref in ((0, xp_ref), (HALO, x_ref), (HALO + tm, xn_ref)):
        yext_s[r0:r0 + ref.shape[0], :] = _adaln(
            ref[...], g1_ref[...], mod_ref[0:1, :], mod_ref[1:2, :]).astype(BF16)
    y = yext_s[HALO:HALO + tm, :]

    def proj(c0, width):
        return _dot(y, win_ref[:, c0:c0 + width])

    cqn = _rms(proj(C_CQ, MLA_Q_RANK), gq_ref[...]).astype(BF16)
    qz = _dot(cqn, wuq_ref[...])
    for h in range(MLA_HEADS):
        blk = qz[:, LANES * h:LANES * (h + 1)]
        if not is_ctx:
            blk = _rope(blk, cm_ref[...], sam_ref[...], sbm_ref[...], 8)
        qm_ref[:, LANES * h:LANES * (h + 1)] = (blk * (MLA_SCALE * LOG2E)).astype(qm_ref.dtype)

    ckv_kr = proj(C_CKV, MLA_KV_RANK + LANES)
    ckvn = _rms(ckv_kr[:, 0:MLA_KV_RANK], gkv_ref[...])
    kr = ckv_kr[:, MLA_KV_RANK:]
    if is_ctx:
        ckv_ref[...] = ckvn
        kr_ref[...] = kr
    else:
        kr = _rope(kr, cm_ref[...], sam_ref[...], sbm_ref[...], 8)
    kvz = _dot(ckvn.astype(BF16), wukv_ref[...])
    for h in range(MLA_HEADS):
        km_ref[:, LANES * h:LANES * (h + 1)] = (kvz[:, LANES * h:LANES * (h + 1)] + kr).astype(km_ref.dtype)
    vm_ref[...] = kvz[:, MLA_HEADS * LANES:].astype(vm_ref.dtype)

    uext_s[...] = _dot(yext_s[...], win_ref[:, C_U:C_U + ML_WIDTH])
    tok = pl.program_id(0) * tm + lax.broadcasted_iota(jnp.int32, (tm, ML_WIDTH), 0)
    u_prev = jnp.where(tok % seq == 0, 0.0, uext_s[HALO - 1:HALO - 1 + tm, :])
    u_next = jnp.where(tok % seq == seq - 1, 0.0, uext_s[HALO + 1:HALO + 1 + tm, :])
    u_ref[...] = _silu(cw_ref[0:1, :] * u_prev + cw_ref[1:2, :] * uext_s[HALO:HALO + tm, :]
                       + cw_ref[2:3, :] * u_next + cb_ref[...]).astype(u_ref.dtype)
    vml_ref[...] = proj(C_V, ML_WIDTH).astype(vml_ref.dtype)
    oml_ref[...] = proj(C_O, ML_WIDTH).astype(oml_ref.dtype)
    gates = proj(C_G, GATE_COLS) + bg_ref[...]
    for pr in range(GATE_COLS // LANES):
        gt_ref[pr] = gates[:, LANES * pr:LANES * (pr + 1)].T[0:8, :]

    qg = _group_rms(proj(C_QG, GQA_HEADS * GQA_DH), gsum_ref[...], ggq_ref[...])
    kv_g = proj(C_KG, 2 * LANES)
    kg = _group_rms(kv_g[:, 0:LANES], gsum_ref[0:LANES, 0:LANES], ggk_ref[...])
    if not is_ctx:
        qg = jnp.concatenate(
            [_rope(qg[:, 0:LANES], cg_ref[...], sag_ref[...], sbg_ref[...], 16),
             _rope(qg[:, LANES:], cg_ref[...], sag_ref[...], sbg_ref[...], 16)], axis=1)
        kg = _rope(kg, cg_ref[...], sag_ref[...], sbg_ref[...], 16)
    qg_ref[...] = (qg * (GQA_SCALE * LOG2E)).astype(qg_ref.dtype)
    kg_ref[...] = kg.astype(kg_ref.dtype)
    vg_ref[...] = kv_g[:, LANES:].astype(vg_ref.dtype)


def _mod_spec(l, seq, tm, is_ctx):
    if is_ctx:
        return pl.BlockSpec((None, None, 6, D_MODEL), lambda i: (l, 0, 0, 0))
    return pl.BlockSpec((None, None, 6, D_MODEL), lambda i: (l, 1 + (i * tm) // seq, 0, 0))


def _in_proj(x2, mod_all, l, seq, is_ctx, wts, tabs):
    m = x2.shape[0]
    tm = TM
    nt = m // tm
    row = lambda i: (i, 0)
    nb = tm // HALO
    last_blk = m // HALO - 1
    names = ["g_norm1", "w_in", "g_mla_q", "w_uq", "g_mla_kv", "w_ukv", "b_gates", "g_gqa_q", "g_gqa_k",
             "w_ml_conv", "b_ml_conv"]
    in_specs = ([pl.BlockSpec((tm, D_MODEL), row),
                 pl.BlockSpec((HALO, D_MODEL), lambda i: (jnp.maximum(i * nb - 1, 0), 0)),
                 pl.BlockSpec((HALO, D_MODEL), lambda i: (jnp.minimum((i + 1) * nb, last_blk), 0)),
                 _mod_spec(l, seq, tm, is_ctx)]
                + [_lspec(wts[n], l) for n in names]
                + [pl.BlockSpec((2 * LANES, 2 * LANES), lambda i: (0, 0))])
    args = [x2, x2, x2, mod_all] + [wts[n] for n in names] + [wts["gsum"]]
    if not is_ctx:
        tpos = seq // tm
        in_specs += [pl.BlockSpec((tm, LANES), lambda i: (i % tpos, 0))] * 6
        args += list(tabs)
    act = F32 if is_ctx else BF16
    outs = [
        (MLA_HEADS * LANES, BF16),
        (MLA_HEADS * LANES, BF16),
        (MLA_HEADS * MLA_V, BF16),
        (ML_WIDTH, BF16),
        (ML_WIDTH, BF16),
        (ML_WIDTH, F32),
        (None, F32),
        (GQA_HEADS * GQA_DH, BF16),
        (LANES, act),
        (LANES, act),
    ]
    if is_ctx:
        outs += [(MLA_KV_RANK, F32), (LANES, F32)]
    return pl.pallas_call(
        functools.partial(_in_proj_kernel, is_ctx, seq),
        grid=(nt,),
        in_specs=in_specs,
        out_specs=[pl.BlockSpec((GATE_COLS // LANES, 8, tm), lambda i: (0, 0, i)) if w is None
                   else pl.BlockSpec((tm, w), row) for w, _ in outs],
        out_shape=[jax.ShapeDtypeStruct((GATE_COLS // LANES, 8, m) if w is None else (m, w), dt)
                   for w, dt in outs],
        scratch_shapes=[pltpu.VMEM((tm + 2 * HALO, D_MODEL), BF16),
                        pltpu.VMEM((tm + 2 * HALO, ML_WIDTH), F32)],
        compiler_params=_cparams(("parallel",)),
        name="in_proj_ctx" if is_ctx else "in_proj_lat",
    )(*args)


def _cache_kv_kernel(ckv_ref, kr_ref, wukv_ref, k_ref, v_ref):
    kvz = _dot(ckv_ref[...].astype(BF16), wukv_ref[...])
    kr = kr_ref[...]
    for h in range(MLA_HEADS):
        k_ref[:, LANES * h:LANES * (h + 1)] = (kvz[:, LANES * h:LANES * (h + 1)] + kr).astype(k_ref.dtype)
    v_ref[...] = kvz[:, MLA_HEADS * LANES:].astype(v_ref.dtype)


def _cache_kv(ckv, kr_pad, l, w_ukv):
    b, _, tc, _ = ckv.shape
    cache_blk = pl.BlockSpec((None, None, tc, LANES), lambda i: (i, l, 0, 0))
    return pl.pallas_call(
        _cache_kv_kernel,
        grid=(b,),
        in_specs=[cache_blk, cache_blk, _lspec(w_ukv, l)],
        out_specs=[pl.BlockSpec((None, tc, MLA_HEADS * LANES), lambda i: (i, 0, 0)),
                   pl.BlockSpec((None, tc, MLA_HEADS * MLA_V), lambda i: (i, 0, 0))],
        out_shape=[jax.ShapeDtypeStruct((b, tc, MLA_HEADS * LANES), BF16),
                   jax.ShapeDtypeStruct((b, tc, MLA_HEADS * MLA_V), BF16)],
        compiler_params=_cparams(("parallel",)),
        name="cache_kv",
    )(ckv, kr_pad, w_ukv)


def _attn_kernel(is_mla, nseg, q_ref, *refs):
    kv = refs[:2 * nseg]
    o_ref = refs[2 * nseg]
    tq = q_ref.shape[0]
    lo = lax.broadcasted_iota(jnp.int32, (tq, LANES), 1) < (LANES // 2)
    heads = []
    for h in range(4):
        if is_mla:
            qh = q_ref[:, LANES * h:LANES * (h + 1)]
            ksl = slice(LANES * h, LANES * (h + 1))
            vsl = slice(LANES * (h // 2), LANES * (h // 2 + 1))
        else:
            blk = q_ref[:, LANES * (h % 2):LANES * (h % 2 + 1)].astype(F32)
            qh = jnp.where(lo if h < 2 else jnp.logical_not(lo), blk, 0.0).astype(BF16)
            ksl = slice(0, LANES)
            vsl = slice(0, LANES)
        ss = [_dot_nt(qh, kv[2 * j][:, ksl].astype(BF16)) for j in range(nseg)]
        mx = jnp.max(ss[0], axis=-1, keepdims=True)
        for s in ss[1:]:
            mx = jnp.maximum(mx, jnp.max(s, axis=-1, keepdims=True))
        den = None
        acc = None
        for j, s in enumerate(ss):
            e = jnp.exp2(s - mx)
            d = jnp.sum(e, axis=-1, keepdims=True)
            a = _dot(e.astype(BF16), kv[2 * j + 1][:, vsl].astype(BF16))
            den = d if den is None else den + d
            acc = a if acc is None else acc + a
        heads.append(acc / den)
    if is_mla:
        o_ref[:, 0:LANES] = jnp.where(lo, heads[0], heads[1]).astype(o_ref.dtype)
        o_ref[:, LANES:] = jnp.where(lo, heads[2], heads[3]).astype(o_ref.dtype)
    else:
        half = LANES // 2
        o_ref[:, 0:LANES] = jnp.where(lo, heads[0], pltpu.roll(heads[1], half, 1)).astype(o_ref.dtype)
        o_ref[:, LANES:] = jnp.where(lo, pltpu.roll(heads[2], half, 1), heads[3]).astype(o_ref.dtype)


def _attention(is_mla, q, segs):
    b, tq_all, wq = q.shape
    tq = min(TQ, tq_all)
    in_specs = [pl.BlockSpec((None, tq, wq), lambda i, j: (i, j, 0))]
    args = [q]
    for k, v, layer in segs:
        for a in (k, v):
            if layer is None:
                in_specs.append(pl.BlockSpec((None,) + a.shape[1:], lambda i, j: (i, 0, 0)))
            else:
                in_specs.append(pl.BlockSpec((None, None) + a.shape[2:], lambda i, j, layer=layer: (i, layer, 0, 0)))
            args.append(a)
    return pl.pallas_call(
        functools.partial(_attn_kernel, is_mla, len(segs)),
        grid=(b, tq_all // tq),
        in_specs=in_specs,
        out_specs=pl.BlockSpec((None, tq, 2 * LANES), lambda i, j: (i, j, 0)),
        out_shape=jax.ShapeDtypeStruct((b, tq_all, 2 * LANES), BF16),
        compiler_params=_cparams(("parallel", "parallel")),
        name=("attn_mla" if is_mla else "attn_gqa") + ("_lat" if len(segs) > 1 else "_ctx"),
    )(*args)


def _log_sigmoid(x):
    return jnp.minimum(x, 0.0) - jnp.log(1.0 + jnp.exp(-jnp.abs(x)))


GATE_HEADS = 2


def _mlstm_kernel(has_init, nc, hps, uc_ref, v_ref, o_ref, gt_ref, wqt_ref, wk_ref, go_ref, *rest):
    if has_init:
        c0_ref, n0_ref, m0_ref, out_ref, qt_s, k_s, vt_s, ht_f, ht_b, acol_s, row_s, st_s = rest
    else:
        out_ref, cfin_ref, sfin_ref, qt_s, k_s, vt_s, ht_f, ht_b, acol_s, row_s, st_s = rest
    t = nc * ML_CHUNK
    sq = (ML_CHUNK, ML_CHUNK)
    w2 = hps * LANES

    r_i = lax.broadcasted_iota(jnp.int32, sq, 0)
    c_i = lax.broadcasted_iota(jnp.int32, sq, 1)
    lower = c_i <= r_i
    upper = c_i >= r_i
    eye = jnp.where(c_i == r_i, 1.0, 0.0).astype(BF16)
    for j in range(hps):
        hs = slice(LANES * j, LANES * (j + 1))
        qt_s[hs, :] = _dot_nt(wqt_ref[j], uc_ref[:, hs]).astype(BF16)
        k_s[:, hs] = _dot(uc_ref[:, hs], wk_ref[j]) * ML_K_SCALE
        vt_s[hs, :] = _dot_nt(eye, v_ref[:, hs]).astype(BF16)

    tri_lo = jnp.where(lower, 1.0, 0.0).astype(BF16)
    tri_up = jnp.where(upper, 1.0, 0.0).astype(BF16)
    row8 = lax.broadcasted_iota(jnp.int32, (8, ML_CHUNK), 0)
    is_f = lax.broadcasted_iota(jnp.int32, (8, t), 0) % 4 >= 2
    for pb in range(hps // GATE_HEADS):
        gates = gt_ref[pb]
        lg_all = jnp.where(is_f, _log_sigmoid(gates), gates)
        for c in range(nc):
            sl = slice(c * ML_CHUNK, (c + 1) * ML_CHUNK)
            lg = lg_all[:, sl]
            h1, h2, h3 = _split3(lg)
            cs = jnp.where(row8 < 4, _dot(h1, tri_up) + _dot(h2, tri_up) + _dot(h3, tri_up),
                           _dot(h1, tri_lo) + _dot(h2, tri_lo) + _dot(h3, tri_lo)) * LOG2E
            row_s[pb, :, sl] = cs
            a = lg * LOG2E - pltpu.roll(cs, 6, 0)
            for jj in range(GATE_HEADS):
                j = GATE_HEADS * pb + jj
                hs = slice(LANES * j, LANES * (j + 1))
                st_s[j, sl, :] = _dot(k_s[sl, hs].astype(BF16), qt_s[hs, sl])
                for d in range(2):
                    r = 4 * d + jj
                    acol_s[2 * j + d, sl, :] = jnp.broadcast_to(a[r:r + 1, :], sq).T

    def chunk(j, d, c0, ct, n, m2):
        hs = slice(LANES * j, LANES * (j + 1))
        cs_ = pl.ds(c0, ML_CHUNK)
        last = 0 if d else ML_CHUNK - 1
        qt = qt_s[hs, cs_]
        k = k_s[cs_, hs]
        vt = vt_s[hs, cs_]
        acol = acol_s[2 * j + d, cs_, :]
        r = 2 + 4 * d + j % GATE_HEADS
        bc = row_s[j // GATE_HEADS, r:r + 1, cs_]
        g = bc + m2
        dlog = jnp.where(lower if d else upper, acol + bc, -jnp.inf)
        m_t = jnp.maximum(g, jnp.max(dlog, axis=0, keepdims=True))
        w = jnp.exp2(dlog - m_t)
        inter = jnp.exp2(g - m_t)
        st = st_s[j, cs_, :] * w
        qn = _dot(jnp.broadcast_to(n, (8, LANES)).astype(BF16), qt)[0:1, :]
        den = jnp.sum(st, axis=0, keepdims=True) + inter * qn
        numt = _dot(vt, st.astype(BF16)) + inter * _dot(ct.astype(BF16), qt)
        ht = numt * (1.0 / jnp.maximum(jnp.abs(den), jnp.exp2(-m_t)))
        b_last = jnp.broadcast_to(bc[:, last:last + 1], (1, LANES))
        m_new = jnp.broadcast_to(m_t[:, last:last + 1], (1, LANES))
        kw = k * jnp.exp2(b_last + acol - m_new)
        decay = jnp.exp2(b_last + m2 - m_new)
        ct_new = decay * ct + _dot(vt, kw.astype(BF16))
        n_new = decay * n + jnp.sum(kw, axis=0, keepdims=True)
        return ht, ct_new, n_new, m_new

    def step(i, carry):
        cf = i * ML_CHUNK
        cb = (nc - 1 - i) * ML_CHUNK
        if not isinstance(i, int):
            cf = pl.multiple_of(cf, ML_CHUNK)
            cb = pl.multiple_of(cb, ML_CHUNK)
        new = []
        for j in range(hps):
            hs = slice(LANES * j, LANES * (j + 1))
            for d, (c0, ht_s) in enumerate(((cf, ht_f), (cb, ht_b))):
                ht, ct, n, m2 = chunk(j, d, c0, *carry[2 * j + d])
                ht_s[hs, pl.ds(c0, ML_CHUNK)] = ht
                new.append((ct, n, m2))
        return tuple(new)

    if has_init:
        init = tuple((c0_ref[d, j].T, n0_ref[j, d:d + 1, :], m0_ref[j, d:d + 1, :] * LOG2E)
                     for j in range(hps) for d in range(2))
    else:
        init = tuple((jnp.zeros(sq, F32), jnp.zeros((1, LANES), F32), jnp.zeros((1, LANES), F32))
                     for j in range(hps) for d in range(2))
    if nc <= 2:
        carry = init
        for i in range(nc):
            carry = step(i, carry)
    else:
        carry = lax.fori_loop(0, nc, step, init, unroll=4)

    hsum = (ht_f[...] + ht_b[...]).T
    for j in range(hps):
        hs = slice(LANES * j, LANES * (j + 1))
        hn = _rms(hsum[:, hs], go_ref[:, hs])
        out_ref[:, hs] = (hn * jax.nn.sigmoid(o_ref[:, hs])).astype(out_ref.dtype)

    if not has_init:
        for j in range(hps):
            for d in range(2):
                ct, n, m2 = carry[2 * j + d]
                cfin_ref[d, j] = ct.T
            nf, nb = carry[2 * j][1], carry[2 * j + 1][1]
            mf, mb = carry[2 * j][2], carry[2 * j + 1][2]
            sfin_ref[j] = jnp.concatenate(
                [nf, nb, mf * (1.0 / LOG2E), mb * (1.0 / LOG2E), jnp.zeros((4, LANES), F32)], axis=0)


def _mlstm(uc, v_ml, o_ml, gates, l, wts, init):
    b, t, _ = uc.shape
    nc = t // ML_CHUNK
    has_init = init is not None
    hps = ML_HEADS if t <= 4 * ML_CHUNK else GATE_HEADS
    w2 = hps * LANES
    pair_blk = lambda i, p: (i, 0, p)
    in_specs = [
        pl.BlockSpec((None, t, w2), pair_blk),
        pl.BlockSpec((None, t, w2), pair_blk),
        pl.BlockSpec((None, t, w2), pair_blk),
        pl.BlockSpec((hps // GATE_HEADS, 8, t), lambda i, p: (p, 0, i)),
        pl.BlockSpec((None, hps, ML_DH, ML_DH), lambda i, p: (l, p, 0, 0)),
        pl.BlockSpec((None, hps, ML_DH, ML_DH), lambda i, p: (l, p, 0, 0)),
        pl.BlockSpec((None, 1, w2), lambda i, p: (l, 0, p)),
    ]
    args = [uc, v_ml, o_ml, gates, wts["w_ml_qt"], wts["w_ml_k"], wts["g_ml_out"]]
    out_specs = [pl.BlockSpec((None, t, w2), pair_blk)]
    out_shape = [jax.ShapeDtypeStruct((b, t, ML_WIDTH), BF16)]
    state_c = pl.BlockSpec((None, 2, hps, ML_DH, ML_DH), lambda i, p: (i, 0, p, 0, 0))
    state_s = pl.BlockSpec((None, hps, 8, LANES), lambda i, p: (i, p, 0, 0))
    if has_init:
        in_specs += [pl.BlockSpec((None, None, 2, hps, ML_DH, ML_DH), lambda i, p: (i, l, 0, p, 0, 0)),
                     pl.BlockSpec((None, None, hps, 8, LANES), lambda i, p: (i, l, p, 0, 0)),
                     pl.BlockSpec((None, None, hps, 8, LANES), lambda i, p: (i, l, p, 0, 0))]
        args += list(init)
    else:
        out_specs += [state_c, state_s]
        out_shape += [
            jax.ShapeDtypeStruct((b, 2, ML_HEADS, ML_DH, ML_DH), F32),
            jax.ShapeDtypeStruct((b, ML_HEADS, 8, LANES), F32),
        ]
    return pl.pallas_call(
        functools.partial(_mlstm_kernel, has_init, nc, hps),
        grid=(b, ML_HEADS // hps),
        in_specs=in_specs,
        out_specs=out_specs,
        out_shape=out_shape,
        scratch_shapes=[
            pltpu.VMEM((w2, t), BF16),
            pltpu.VMEM((t, w2), F32),
            pltpu.VMEM((w2, t), BF16),
            pltpu.VMEM((w2, t), F32),
            pltpu.VMEM((w2, t), F32),
            pltpu.VMEM((2 * hps, t, LANES), F32),
            pltpu.VMEM((hps // GATE_HEADS, 8, t), F32),
            pltpu.VMEM((hps, t, LANES), F32),
        ],
        compiler_params=_cparams(("parallel", "parallel")),
        name="mlstm_lat" if has_init else "mlstm_ctx",
    )(*args)


def _ffn_kernel(seq, tm, final, *refs):
    x_refs, oa_refs, ob_refs, oc_refs = refs[0:3], refs[3:6], refs[6:9], refs[9:12]
    (mod_ref, g2_ref, wout_ref, wup_ref, cw_ref, cb_ref, wdn_ref, gf_ref, y_ref,
     oext_s, yext_s, gext_s, h_s) = refs[12:]
    i = pl.program_id(0)
    a_w = MLA_HEADS * MLA_V
    ext = ((0, HALO), (HALO, tm), (HALO + tm, HALO))
    for (r0, n), k in zip(ext, (1, 0, 2)):
        oext_s[r0:r0 + n, 0:a_w] = oa_refs[k][...]
        oext_s[r0:r0 + n, a_w:a_w + ML_WIDTH] = ob_refs[k][...]
        oext_s[r0:r0 + n, a_w + ML_WIDTH:] = oc_refs[k][...]
    mix = _dot(oext_s[...], wout_ref[...])
    gate1 = mod_ref[2:3, :]
    x = x_refs[0][...] + gate1 * mix[HALO:HALO + tm, :]
    xp = x_refs[1][...] + gate1 * mix[0:HALO, :]
    xn = x_refs[2][...] + gate1 * mix[HALO + tm:, :]

    g2 = g2_ref[...]
    shift = mod_ref[3:4, :]
    scale = mod_ref[4:5, :]
    keep_p = jnp.where((i * tm) % seq == 0, 0.0, 1.0)
    keep_n = jnp.where(((i + 1) * tm) % seq == 0, 0.0, 1.0)
    yext_s[0:HALO, :] = (_adaln(xp, g2, shift, scale) * keep_p).astype(BF16)
    yext_s[HALO:HALO + tm, :] = _adaln(x, g2, shift, scale).astype(BF16)
    yext_s[HALO + tm:, :] = (_adaln(xn, g2, shift, scale) * keep_n).astype(BF16)

    for c in range(D_FF // FF_CHUNK):
        c0 = c * FF_CHUNK
        a = _dot(yext_s[HALO:HALO + tm, :], wup_ref[:, c0:c0 + FF_CHUNK])
        gext_s[...] = _dot(yext_s[...], wup_ref[:, D_FF + c0:D_FF + c0 + FF_CHUNK])
        g = (cw_ref[0:1, c0:c0 + FF_CHUNK] * gext_s[HALO - 1:HALO - 1 + tm, :]
             + cw_ref[1:2, c0:c0 + FF_CHUNK] * gext_s[HALO:HALO + tm, :]
             + cw_ref[2:3, c0:c0 + FF_CHUNK] * gext_s[HALO + 1:HALO + 1 + tm, :]
             + cb_ref[:, c0:c0 + FF_CHUNK])
        h_s[:, c0:c0 + FF_CHUNK] = (_silu(g) * a).astype(BF16)
    y = x + mod_ref[5:6, :] * _dot(h_s[...], wdn_ref[...])
    if final:
        y = _rms(y, gf_ref[...])
    y_ref[...] = y


def _ffn(x2, o_a, o_b, o_c, mod_all, l, seq, is_ctx, wts, g_final, final):
    m = x2.shape[0]
    tm = min(TM, seq)
    nb = tm // HALO
    last_blk = m // HALO - 1
    row = lambda i: (i, 0)
    prev = lambda i: (jnp.maximum(i * nb - 1, 0), 0)
    nxt = lambda i: (jnp.minimum((i + 1) * nb, last_blk), 0)
    in_specs, args = [], []
    for a in (x2, o_a, o_b, o_c):
        w = a.shape[1]
        in_specs += [pl.BlockSpec((tm, w), row), pl.BlockSpec((HALO, w), prev), pl.BlockSpec((HALO, w), nxt)]
        args += [a, a, a]
    names = ["g_norm2", "w_out", "w_ff_up", "w_ff_conv", "b_ff_conv", "w_ff_down"]
    in_specs += ([_mod_spec(l, seq, tm, is_ctx)] + [_lspec(wts[n], l) for n in names]
                 + [pl.BlockSpec((1, D_MODEL), lambda i: (0, 0))])
    args += [mod_all] + [wts[n] for n in names] + [g_final]
    return pl.pallas_call(
        functools.partial(_ffn_kernel, seq, tm, final),
        grid=(m // tm,),
        in_specs=in_specs,
        out_specs=pl.BlockSpec((tm, D_MODEL), row),
        out_shape=jax.ShapeDtypeStruct((m, D_MODEL), F32),
        scratch_shapes=[pltpu.VMEM((tm + 2 * HALO, D_MODEL), BF16),
                        pltpu.VMEM((tm + 2 * HALO, D_MODEL), BF16),
                        pltpu.VMEM((tm + 2 * HALO, FF_CHUNK), F32),
                        pltpu.VMEM((tm, D_FF), BF16)],
        compiler_params=_cparams(("parallel",)),
        name="ffn_ctx" if is_ctx else "ffn_lat",
    )(*args)


def _rope_tables(t):
    pos = np.arange(t)
    rows = (pos // GRID_W).astype(np.float32)
    cols = (pos % GRID_W).astype(np.float32)

    def group(p, d):
        inv = (np.float32(ROPE_THETA) ** (-np.arange(0, d, 2, dtype=np.float32) / np.float32(d))).astype(np.float32)
        ang = (p[:, None] * inv[None, :]).astype(np.float32)
        cs, sn, z = np.cos(ang), np.sin(ang), np.zeros_like(ang)
        return (np.concatenate([cs, cs], 1), np.concatenate([-sn, z], 1), np.concatenate([z, sn], 1))

    gr, gc = group(rows, MLA_ROPE // 2), group(cols, MLA_ROPE // 2)
    ones = np.ones((t, MLA_NOPE), np.float32)
    zeros = np.zeros((t, MLA_NOPE), np.float32)
    pad1 = np.ones((t, LANES - MLA_NOPE - MLA_ROPE), np.float32)
    pad0 = np.zeros((t, LANES - MLA_NOPE - MLA_ROPE), np.float32)
    mla = (np.concatenate([ones, gr[0], gc[0], pad1], 1),
           np.concatenate([zeros, gr[1], gc[1], pad0], 1),
           np.concatenate([zeros, gr[2], gc[2], pad0], 1))
    gr, gc = group(rows, GQA_DH // 2), group(cols, GQA_DH // 2)
    gqa = tuple(np.concatenate([gr[k], gc[k], gr[k], gc[k]], 1) for k in range(3))
    return tuple(jnp.asarray(a, F32) for a in mla + gqa)


def _prep_weights(p):
    w_in = p["w_in"]
    nl = w_in.shape[0]
    qg = w_in[:, :, 1968:2224].reshape(nl, D_MODEL, GQA_HEADS, GQA_DH)[:, :, np.array([0, 2, 1, 3])]
    gh = GATE_HEADS
    gsrc = np.array([kind * ML_HEADS + gh * pr + j for pr in range(2) for kind in range(4) for j in range(gh)])
    gdst = np.array([LANES * pr + gh * kind + j for pr in range(2) for kind in range(4) for j in range(gh)])
    w_gates = jnp.zeros((nl, D_MODEL, GATE_COLS), F32).at[:, :, gdst].set(w_in[:, :, 1952 + gsrc])
    b_gates = jnp.zeros((nl, GATE_COLS), F32).at[:, gdst].set(p["b_ml_gates"][:, gsrc])
    w_in_p = jnp.concatenate([
        w_in[:, :, 0:384],
        jnp.pad(w_in[:, :, 384:416], ((0, 0), (0, 0), (MLA_NOPE, LANES - MLA_NOPE - MLA_ROPE))),
        w_in[:, :, 416:1952],
        w_gates,
        qg.reshape(nl, D_MODEL, GQA_HEADS * GQA_DH),
        w_in[:, :, 2224:2480],
    ], axis=2).astype(BF16)
    w_uq = p["w_mla_uq"].reshape(nl, MLA_Q_RANK, MLA_HEADS, MLA_NOPE + MLA_ROPE)
    w_uq = jnp.pad(w_uq, ((0, 0), (0, 0), (0, 0), (0, LANES - MLA_NOPE - MLA_ROPE)))
    w_ukv = p["w_mla_ukv"].reshape(nl, MLA_KV_RANK, MLA_HEADS, MLA_NOPE + MLA_V)
    w_uk = jnp.pad(w_ukv[..., :MLA_NOPE], ((0, 0), (0, 0), (0, 0), (0, LANES - MLA_NOPE)))
    w_ukv_p = jnp.concatenate([w_uk.reshape(nl, MLA_KV_RANK, -1),
                               w_ukv[..., MLA_NOPE:].reshape(nl, MLA_KV_RANK, -1)], axis=2)
    grp = np.arange(2 * LANES) // GQA_DH
    row = lambda a: a[:, None, :]
    return {
        "g_norm1": row(p["g_norm1"]),
        "g_norm2": row(p["g_norm2"]),
        "w_in": w_in_p,
        "g_mla_q": row(p["g_mla_q"]),
        "w_uq": w_uq.reshape(nl, MLA_Q_RANK, -1).astype(BF16),
        "g_mla_kv": row(p["g_mla_kv"]),
        "w_ukv": w_ukv_p.astype(BF16),
        "b_gates": row(b_gates),
        "g_gqa_q": row(jnp.tile(p["g_gqa_q"], (1, GQA_HEADS))),
        "g_gqa_k": row(jnp.tile(p["g_gqa_k"], (1, GQA_KV_HEADS))),
        "gsum": jnp.asarray(grp[:, None] == grp[None, :], BF16),
        "w_ml_conv": p["w_ml_conv"],
        "b_ml_conv": row(p["b_ml_conv"]),
        "w_ml_qt": jnp.swapaxes(p["w_ml_q"], 2, 3).astype(BF16),
        "w_ml_k": p["w_ml_k"].astype(BF16),
        "g_ml_out": row(p["g_ml_out"]),
        "w_out": p["w_out"].astype(BF16),
        "w_ff_up": p["w_ff_up"].astype(BF16),
        "w_ff_conv": p["w_ff_conv"],
        "b_ff_conv": row(p["b_ff_conv"]),
        "w_ff_down": p["w_ff_down"].astype(BF16),
    }


def _layer(x2, mod_all, l, seq, is_ctx, wts, tabs, cache, g_final, final):
    m = x2.shape[0]
    b = m // seq
    r3 = lambda a: a.reshape(b, seq, a.shape[-1])
    outs = _in_proj(x2, mod_all, l, seq, is_ctx, wts, tabs)
    q_m, k_m, v_m, u, v_ml, o_ml, gates, q_g, k_g, v_g = outs[:10]
    segs_a = [(r3(k_m), r3(v_m), None)]
    segs_c = [(r3(k_g), r3(v_g), None)]
    init = None
    if not is_ctx:
        ckv_c, kr_pad, kg_c, vg_c, ct0, n0p, m0p = cache
        kc, vc = _cache_kv(ckv_c, kr_pad, l, wts["w_ukv"])
        segs_a.append((kc, vc, None))
        segs_c.append((kg_c, vg_c, l))
        init = (ct0, n0p, m0p)
    o_a = _attention(True, r3(q_m), segs_a)
    o_c = _attention(False, r3(q_g), segs_c)
    ml = _mlstm(r3(u), r3(v_ml), r3(o_ml), gates, l, wts, init)
    x2 = _ffn(x2, o_a.reshape(m, -1), ml[0].reshape(m, -1), o_c.reshape(m, -1), mod_all, l, seq, is_ctx, wts,
              g_final, final)
    state = None
    if is_ctx:
        ckv_n, kr_raw = outs[10:12]
        sfin = ml[2]
        state = (
            r3(ckv_n),
            r3(kr_raw)[:, :, MLA_NOPE:MLA_NOPE + MLA_ROPE],
            k_g.reshape(b, seq, GQA_KV_HEADS, GQA_DH),
            v_g.reshape(b, seq, GQA_KV_HEADS, GQA_DH),
            ml[1],
            jnp.swapaxes(sfin[:, :, 0:2, :], 1, 2),
            jnp.swapaxes(sfin[:, :, 2:4, 0], 1, 2),
        )
    return x2, state


def kernel(x_prompt, x_sample, cache_mla_ckv, cache_mla_krope, cache_gqa_k, cache_gqa_v, state_mlstm_C, state_mlstm_n, state_mlstm_m, c, c_ctx, w_ada, b_ada, g_norm1, g_norm2, w_in, g_mla_q, w_mla_uq, g_mla_kv, w_mla_ukv, w_ml_conv, b_ml_conv, w_ml_q, w_ml_k, b_ml_gates, g_ml_out, g_gqa_q, g_gqa_k, w_out, w_ff_up, w_ff_conv, b_ff_conv, w_ff_down, g_final):
    params = {
        "g_norm1": g_norm1, "g_norm2": g_norm2, "w_in": w_in, "g_mla_q": g_mla_q, "w_mla_uq": w_mla_uq,
        "g_mla_kv": g_mla_kv, "w_mla_ukv": w_mla_ukv, "w_ml_conv": w_ml_conv, "b_ml_conv": b_ml_conv,
        "w_ml_q": w_ml_q, "w_ml_k": w_ml_k, "b_ml_gates": b_ml_gates, "g_ml_out": g_ml_out,
        "g_gqa_q": g_gqa_q, "g_gqa_k": g_gqa_k, "w_out": w_out, "w_ff_up": w_ff_up, "w_ff_conv": w_ff_conv,
        "b_ff_conv": b_ff_conv, "w_ff_down": w_ff_down,
    }
    depth = w_in.shape[0]
    bp, sp, _ = x_prompt.shape
    bs, ss, _ = x_sample.shape
    assert bs + 1 <= 16 and sp % ML_CHUNK == 0 and ss % TM == 0 and (bp * sp) % TM == 0

    cvec = jnp.concatenate([c_ctx[None, :], c, jnp.zeros((16 - 1 - bs, D_MODEL), F32)], axis=0)
    mod_all = _modulation(cvec, w_ada, b_ada).reshape(depth, 16, 6, D_MODEL)
    tabs = _rope_tables(ss)
    gf = g_final[None, :]
    wts = _prep_weights(params)

    tc = cache_mla_ckv.shape[2]
    pad_rows = ((0, 0), (0, 0), (0, 0), (0, 8 - 2), (0, 0))
    cache = (
        cache_mla_ckv,
        jnp.pad(cache_mla_krope, ((0, 0), (0, 0), (0, 0), (MLA_NOPE, LANES - MLA_NOPE - MLA_ROPE))),
        cache_gqa_k.reshape(bs, depth, tc, GQA_KV_HEADS * GQA_DH),
        cache_gqa_v.reshape(bs, depth, tc, GQA_KV_HEADS * GQA_DH),
        state_mlstm_C,
        jnp.pad(jnp.swapaxes(state_mlstm_n, 2, 3), pad_rows),
        jnp.pad(jnp.broadcast_to(jnp.swapaxes(state_mlstm_m, 2, 3)[..., None], (bs, depth, ML_HEADS, 2, LANES)),
                pad_rows),
    )

    xp = x_prompt.reshape(bp * sp, D_MODEL)
    xs = x_sample.reshape(bs * ss, D_MODEL)
    states = []
    for l in range(depth):
        final = l == depth - 1
        xp, st = _layer(xp, mod_all, l, sp, True, wts, None, None, gf, final)
        states.append(st)
        xs, _ = _layer(xs, mod_all, l, ss, False, wts, tabs, cache, gf, final)
    new_state = tuple(jnp.stack([st[k] for st in states], axis=1) for k in range(7))
    return (xp.reshape(bp, sp, D_MODEL), xs.reshape(bs, ss, D_MODEL)) + new_state
```

```python
import functools

import jax
import jax.numpy as jnp
import numpy as np
from jax import lax
from jax.experimental import pallas as pl
from jax.experimental.pallas import tpu as pltpu

F32 = jnp.float32
BF16 = jnp.bfloat16

D_MODEL = 1024
GRID_W = 64
ROPE_THETA = 10000.0
EPS = 1e-6
MLA_HEADS = 4
MLA_NOPE = 64
MLA_ROPE = 32
MLA_V = 64
MLA_Q_RANK = 256
MLA_KV_RANK = 128
ML_HEADS = 4
ML_DH = 128
ML_WIDTH = ML_HEADS * ML_DH
ML_CHUNK = 128
GQA_HEADS = 4
GQA_KV_HEADS = 2
GQA_DH = 64
D_FF = 2816
MLA_SCALE = (MLA_NOPE + MLA_ROPE) ** -0.5
GQA_SCALE = GQA_DH ** -0.5
ML_K_SCALE = ML_DH ** -0.5
LOG2E = 1.4426950408889634

LANES = 128
HALO = 16
VMEM_LIMIT = 52 * 1024 * 1024

C_CQ, C_CKV, C_KR, C_U, C_V, C_O, C_G, C_QG, C_KG, C_VG, IN_COLS_P = (
    0, 256, 384, 512, 1024, 1536, 2048, 2304, 2560, 2688, 2816)
GATE_COLS = 2 * LANES
FF_CHUNK = 256
TM = 512
TQ = 512


def _cparams(sem):
    return pltpu.CompilerParams(dimension_semantics=sem, vmem_limit_bytes=VMEM_LIMIT)


def _lspec(arr, l):
    nd = arr.ndim - 1
    return pl.BlockSpec((None,) + arr.shape[1:], lambda *_: (l,) + (0,) * nd, pipeline_mode=pl.Buffered(1))


def _dot(a, b):
    return jnp.dot(a, b, preferred_element_type=F32)


def _dot_nt(a, b):
    return lax.dot_general(a, b, (((1,), (1,)), ((), ())), preferred_element_type=F32)


def _dot_tn(a, b):
    return lax.dot_general(a, b, (((0,), (0,)), ((), ())), preferred_element_type=F32)


def _rms(x, g):
    return (x * lax.rsqrt(jnp.mean(x * x, axis=-1, keepdims=True) + EPS)) * g


def _silu(x):
    return x * jax.nn.sigmoid(x)


def _adaln(x, g, shift, scale):
    return _rms(x, g) * (1.0 + scale) + shift


def _split2(x):
    hi = x.astype(BF16)
    lo = (x - hi.astype(F32)).astype(BF16)
    return hi, lo


def _split3(x):
    h1 = x.astype(BF16)
    r1 = x - h1.astype(F32)
    h2 = r1.astype(BF16)
    h3 = (r1 - h2.astype(F32)).astype(BF16)
    return h1, h2, h3


def _rope(x, cos, sa, sb, half):
    w = x.shape[-1]
    return x * cos + pltpu.roll(x, w - half, 1) * sa + pltpu.roll(x, half, 1) * sb


def _group_rms(x, gsum, g):
    hi, lo = _split2(x * x)
    ss = _dot(hi, gsum) + _dot(lo, gsum)
    return (x * lax.rsqrt(ss * (1.0 / GQA_DH) + EPS)) * g


def _mod_kernel(c_ref, w_ref, b_ref, o_ref):
    a = _silu(c_ref[...]).astype(BF16)
    o_ref[...] = _dot(a, w_ref[...].astype(BF16)) + b_ref[...]


def _modulation(cvec, w_ada, b_ada):
    nl = w_ada.shape[0]
    tn = 1536
    return pl.pallas_call(
        _mod_kernel,
        grid=(nl, 6 * D_MODEL // tn),
        in_specs=[
            pl.BlockSpec((16, D_MODEL), lambda l, j: (0, 0)),
            pl.BlockSpec((None, D_MODEL, tn), lambda l, j: (l, 0, j)),
            pl.BlockSpec((None, 1, tn), lambda l, j: (l, 0, j)),
        ],
        out_specs=pl.BlockSpec((None, 16, tn), lambda l, j: (l, 0, j)),
        out_shape=jax.ShapeDtypeStruct((nl, 16, 6 * D_MODEL), F32),
        compiler_params=_cparams(("arbitrary", "arbitrary")),
        name="modulation",
    )(cvec, w_ada, b_ada.reshape(nl, 1, 6 * D_MODEL))


def _in_proj_kernel(is_ctx, x_ref, mod_ref, g1_ref, win_ref, gq_ref, wuq_ref, gkv_ref, wukv_ref,
                    bg_ref, ggq_ref, ggk_ref, gsum_ref, *rest):
    if is_ctx:
        (qm_ref, km_ref, vm_ref, u_ref, vml_ref, oml_ref, gt_ref, qg_ref, kg_ref, vg_ref,
         ckv_ref, kr_ref) = rest
    else:
        (cm_ref, sam_ref, sbm_ref, cg_ref, sag_ref, sbg_ref,
         qm_ref, km_ref, vm_ref, u_ref, vml_ref, oml_ref, gt_ref, qg_ref, kg_ref, vg_ref) = rest

    y = _adaln(x_ref[...], g1_ref[...], mod_ref[0:1, :], mod_ref[1:2, :]).astype(BF16)

    def proj(c0, width):
        return _dot(y, win_ref[:, c0:c0 + width])

    cqn = _rms(proj(C_CQ, MLA_Q_RANK), gq_ref[...]).astype(BF16)
    qz = _dot(cqn, wuq_ref[...])
    for h in range(MLA_HEADS):
        blk = qz[:, LANES * h:LANES * (h + 1)]
        if not is_ctx:
            blk = _rope(blk, cm_ref[...], sam_ref[...], sbm_ref[...], 8)
        qm_ref[:, LANES * h:LANES * (h + 1)] = (blk * (MLA_SCALE * LOG2E)).astype(qm_ref.dtype)

    ckv_kr = proj(C_CKV, MLA_KV_RANK + LANES)
    ckvn = _rms(ckv_kr[:, 0:MLA_KV_RANK], gkv_ref[...])
    kr = ckv_kr[:, MLA_KV_RANK:]
    if is_ctx:
        ckv_ref[...] = ckvn
        kr_ref[...] = kr
    else:
        kr = _rope(kr, cm_ref[...], sam_ref[...], sbm_ref[...], 8)
    kvz = _dot(ckvn.astype(BF16), wukv_ref[...])
    for h in range(MLA_HEADS):
        km_ref[:, LANES * h:LANES * (h + 1)] = (kvz[:, LANES * h:LANES * (h + 1)] + kr).astype(km_ref.dtype)
    vm_ref[...] = kvz[:, MLA_HEADS * LANES:].astype(vm_ref.dtype)

    u_ref[...] = proj(C_U, ML_WIDTH).astype(u_ref.dtype)
    vml_ref[...] = proj(C_V, ML_WIDTH).astype(vml_ref.dtype)
    oml_ref[...] = proj(C_O, ML_WIDTH).astype(oml_ref.dtype)
    gates = proj(C_G, GATE_COLS) + bg_ref[...]
    for pr in range(GATE_COLS // LANES):
        gt_ref[pr] = gates[:, LANES * pr:LANES * (pr + 1)].T[0:8, :]

    qg = _group_rms(proj(C_QG, GQA_HEADS * GQA_DH), gsum_ref[...], ggq_ref[...])
    kv_g = proj(C_KG, 2 * LANES)
    kg = _group_rms(kv_g[:, 0:LANES], gsum_ref[0:LANES, 0:LANES], ggk_ref[...])
    if not is_ctx:
        qg = jnp.concatenate(
            [_rope(qg[:, 0:LANES], cg_ref[...], sag_ref[...], sbg_ref[...], 16),
             _rope(qg[:, LANES:], cg_ref[...], sag_ref[...], sbg_ref[...], 16)], axis=1)
        kg = _rope(kg, cg_ref[...], sag_ref[...], sbg_ref[...], 16)
    qg_ref[...] = (qg * (GQA_SCALE * LOG2E)).astype(qg_ref.dtype)
    kg_ref[...] = kg.astype(kg_ref.dtype)
    vg_ref[...] = kv_g[:, LANES:].astype(vg_ref.dtype)


def _mod_spec(l, seq, tm, is_ctx):
    if is_ctx:
        return pl.BlockSpec((None, None, 6, D_MODEL), lambda i: (l, 0, 0, 0))
    return pl.BlockSpec((None, None, 6, D_MODEL), lambda i: (l, 1 + (i * tm) // seq, 0, 0))


def _in_proj(x2, mod_all, l, seq, is_ctx, wts, tabs):
    m = x2.shape[0]
    tm = TM
    nt = m // tm
    row = lambda i: (i, 0)
    names = ["g_norm1", "w_in", "g_mla_q", "w_uq", "g_mla_kv", "w_ukv", "b_gates", "g_gqa_q", "g_gqa_k"]
    in_specs = ([pl.BlockSpec((tm, D_MODEL), row), _mod_spec(l, seq, tm, is_ctx)]
                + [_lspec(wts[n], l) for n in names]
                + [pl.BlockSpec((2 * LANES, 2 * LANES), lambda i: (0, 0))])
    args = [x2, mod_all] + [wts[n] for n in names] + [wts["gsum"]]
    if not is_ctx:
        tpos = seq // tm
        in_specs += [pl.BlockSpec((tm, LANES), lambda i: (i % tpos, 0))] * 6
        args += list(tabs)
    act = F32 if is_ctx else BF16
    outs = [
        (MLA_HEADS * LANES, BF16),
        (MLA_HEADS * LANES, BF16),
        (MLA_HEADS * MLA_V, BF16),
        (ML_WIDTH, F32),
        (ML_WIDTH, BF16),
        (ML_WIDTH, F32),
        (None, F32),
        (GQA_HEADS * GQA_DH, BF16),
        (LANES, act),
        (LANES, act),
    ]
    if is_ctx:
        outs += [(MLA_KV_RANK, F32), (LANES, F32)]
    return pl.pallas_call(
        functools.partial(_in_proj_kernel, is_ctx),
        grid=(nt,),
        in_specs=in_specs,
        out_specs=[pl.BlockSpec((GATE_COLS // LANES, 8, tm), lambda i: (0, 0, i)) if w is None
                   else pl.BlockSpec((tm, w), row) for w, _ in outs],
        out_shape=[jax.ShapeDtypeStruct((GATE_COLS // LANES, 8, m) if w is None else (m, w), dt)
                   for w, dt in outs],
        compiler_params=_cparams(("parallel",)),
        name="in_proj_ctx" if is_ctx else "in_proj_lat",
    )(*args)


def _cache_kv_kernel(ckv_ref, kr_ref, wukv_ref, k_ref, v_ref):
    kvz = _dot(ckv_ref[...].astype(BF16), wukv_ref[...])
    kr = kr_ref[...]
    for h in range(MLA_HEADS):
        k_ref[:, LANES * h:LANES * (h + 1)] = (kvz[:, LANES * h:LANES * (h + 1)] + kr).astype(k_ref.dtype)
    v_ref[...] = kvz[:, MLA_HEADS * LANES:].astype(v_ref.dtype)


def _cache_kv(ckv, kr_pad, l, w_ukv):
    b, _, tc, _ = ckv.shape
    cache_blk = pl.BlockSpec((None, None, tc, LANES), lambda i: (i, l, 0, 0))
    return pl.pallas_call(
        _cache_kv_kernel,
        grid=(b,),
        in_specs=[cache_blk, cache_blk, _lspec(w_ukv, l)],
        out_specs=[pl.BlockSpec((None, tc, MLA_HEADS * LANES), lambda i: (i, 0, 0)),
                   pl.BlockSpec((None, tc, MLA_HEADS * MLA_V), lambda i: (i, 0, 0))],
        out_shape=[jax.ShapeDtypeStruct((b, tc, MLA_HEADS * LANES), BF16),
                   jax.ShapeDtypeStruct((b, tc, MLA_HEADS * MLA_V), BF16)],
        compiler_params=_cparams(("parallel",)),
        name="cache_kv",
    )(ckv, kr_pad, w_ukv)


def _attn_kernel(is_mla, nseg, q_ref, *refs):
    kv = refs[:2 * nseg]
    o_ref = refs[2 * nseg]
    tq = q_ref.shape[0]
    lo = lax.broadcasted_iota(jnp.int32, (tq, LANES), 1) < (LANES // 2)
    heads = []
    for h in range(4):
        if is_mla:
            qh = q_ref[:, LANES * h:LANES * (h + 1)]
            ksl = slice(LANES * h, LANES * (h + 1))
            vsl = slice(LANES * (h // 2), LANES * (h // 2 + 1))
        else:
            blk = q_ref[:, LANES * (h % 2):LANES * (h % 2 + 1)].astype(F32)
            qh = jnp.where(lo if h < 2 else jnp.logical_not(lo), blk, 0.0).astype(BF16)
            ksl = slice(0, LANES)
            vsl = slice(0, LANES)
        ss = [_dot_nt(qh, kv[2 * j][:, ksl].astype(BF16)) for j in range(nseg)]
        mx = jnp.max(ss[0], axis=-1, keepdims=True)
        for s in ss[1:]:
            mx = jnp.maximum(mx, jnp.max(s, axis=-1, keepdims=True))
        den = None
        acc = None
        for j, s in enumerate(ss):
            e = jnp.exp2(s - mx)
            d = jnp.sum(e, axis=-1, keepdims=True)
            a = _dot(e.astype(BF16), kv[2 * j + 1][:, vsl].astype(BF16))
            den = d if den is None else den + d
            acc = a if acc is None else acc + a
        heads.append(acc / den)
    if is_mla:
        o_ref[:, 0:LANES] = jnp.where(lo, heads[0], heads[1]).astype(o_ref.dtype)
        o_ref[:, LANES:] = jnp.where(lo, heads[2], heads[3]).astype(o_ref.dtype)
    else:
        half = LANES // 2
        o_ref[:, 0:LANES] = jnp.where(lo, heads[0], pltpu.roll(heads[1], half, 1)).astype(o_ref.dtype)
        o_ref[:, LANES:] = jnp.where(lo, pltpu.roll(heads[2], half, 1), heads[3]).astype(o_ref.dtype)


def _attention(is_mla, q, segs):
    b, tq_all, wq = q.shape
    tq = min(TQ, tq_all)
    in_specs = [pl.BlockSpec((None, tq, wq), lambda i, j: (i, j, 0))]
    args = [q]
    for k, v, layer in segs:
        for a in (k, v):
            if layer is None:
                in_specs.append(pl.BlockSpec((None,) + a.shape[1:], lambda i, j: (i, 0, 0)))
            else:
                in_specs.append(pl.BlockSpec((None, None) + a.shape[2:], lambda i, j, layer=layer: (i, layer, 0, 0)))
            args.append(a)
    return pl.pallas_call(
        functools.partial(_attn_kernel, is_mla, len(segs)),
        grid=(b, tq_all // tq),
        in_specs=in_specs,
        out_specs=pl.BlockSpec((None, tq, 2 * LANES), lambda i, j: (i, j, 0)),
        out_shape=jax.ShapeDtypeStruct((b, tq_all, 2 * LANES), BF16),
        compiler_params=_cparams(("parallel", "parallel")),
        name=("attn_mla" if is_mla else "attn_gqa") + ("_lat" if len(segs) > 1 else "_ctx"),
    )(*args)


def _log_sigmoid(x):
    return jnp.minimum(x, 0.0) - jnp.log(1.0 + jnp.exp(-jnp.abs(x)))


GATE_HEADS = 2


def _mlstm_kernel(has_init, nc, hps, u_ref, v_ref, o_ref, gt_ref, cw_ref, cb_ref, wqt_ref, wk_ref, go_ref,
                  *rest):
    if has_init:
        c0_ref, n0_ref, m0_ref, out_ref, qt_s, k_s, vt_s, ht_f, ht_b, acol_s, row_s, st_s = rest
    else:
        out_ref, cfin_ref, sfin_ref, qt_s, k_s, vt_s, ht_f, ht_b, acol_s, row_s, st_s = rest
    t = nc * ML_CHUNK
    sq = (ML_CHUNK, ML_CHUNK)
    w2 = hps * LANES

    u = u_ref[...]
    rows = lax.broadcasted_iota(jnp.int32, (t, w2), 0)
    up = jnp.where(rows == 0, 0.0, pltpu.roll(u, 1, 0))
    un = jnp.where(rows == t - 1, 0.0, pltpu.roll(u, t - 1, 0))
    uc = _silu(cw_ref[0:1, :] * up + cw_ref[1:2, :] * u + cw_ref[2:3, :] * un + cb_ref[...]).astype(BF16)
    r_i = lax.broadcasted_iota(jnp.int32, sq, 0)
    c_i = lax.broadcasted_iota(jnp.int32, sq, 1)
    lower = c_i <= r_i
    upper = c_i >= r_i
    eye = jnp.where(c_i == r_i, 1.0, 0.0).astype(BF16)
    for j in range(hps):
        hs = slice(LANES * j, LANES * (j + 1))
        qt_s[hs, :] = _dot_nt(wqt_ref[j], uc[:, hs]).astype(BF16)
        k_s[:, hs] = _dot(uc[:, hs], wk_ref[j]) * ML_K_SCALE
        vt_s[hs, :] = _dot_nt(eye, v_ref[:, hs]).astype(BF16)

    tri_lo = jnp.where(lower, 1.0, 0.0).astype(BF16)
    tri_up = jnp.where(upper, 1.0, 0.0).astype(BF16)
    row8 = lax.broadcasted_iota(jnp.int32, (8, ML_CHUNK), 0)
    is_f = lax.broadcasted_iota(jnp.int32, (8, t), 0) % 4 >= 2
    for pb in range(hps // GATE_HEADS):
        gates = gt_ref[pb]
        lg_all = jnp.where(is_f, _log_sigmoid(gates), gates)
        for c in range(nc):
            sl = slice(c * ML_CHUNK, (c + 1) * ML_CHUNK)
            lg = lg_all[:, sl]
            h1, h2, h3 = _split3(lg)
            cs = jnp.where(row8 < 4, _dot(h1, tri_up) + _dot(h2, tri_up) + _dot(h3, tri_up),
                           _dot(h1, tri_lo) + _dot(h2, tri_lo) + _dot(h3, tri_lo)) * LOG2E
            row_s[pb, :, sl] = cs
            a = lg * LOG2E - pltpu.roll(cs, 6, 0)
            for jj in range(GATE_HEADS):
                j = GATE_HEADS * pb + jj
                hs = slice(LANES * j, LANES * (j + 1))
                st_s[j, sl, :] = _dot(k_s[sl, hs].astype(BF16), qt_s[hs, sl])
                for d in range(2):
                    r = 4 * d + jj
                    acol_s[2 * j + d, sl, :] = jnp.broadcast_to(a[r:r + 1, :], sq).T

    def chunk(j, d, c0, ct, n, m2):
        hs = slice(LANES * j, LANES * (j + 1))
        cs_ = pl.ds(c0, ML_CHUNK)
        last = 0 if d else ML_CHUNK - 1
        qt = qt_s[hs, cs_]
        k = k_s[cs_, hs]
        vt = vt_s[hs, cs_]
        acol = acol_s[2 * j + d, cs_, :]
        r = 2 + 4 * d + j % GATE_HEADS
        bc = row_s[j // GATE_HEADS, r:r + 1, cs_]
        g = bc + m2
        dlog = jnp.where(lower if d else upper, acol + bc, -jnp.inf)
        m_t = jnp.maximum(g, jnp.max(dlog, axis=0, keepdims=True))
        w = jnp.exp2(dlog - m_t)
        inter = jnp.exp2(g - m_t)
        st = st_s[j, cs_, :] * w
        qn = _dot(jnp.broadcast_to(n, (8, LANES)).astype(BF16), qt)[0:1, :]
        den = jnp.sum(st, axis=0, keepdims=True) + inter * qn
        numt = _dot(vt, st.astype(BF16)) + inter * _dot(ct.astype(BF16), qt)
        ht = numt * (1.0 / jnp.maximum(jnp.abs(den), jnp.exp2(-m_t)))
        b_last = jnp.broadcast_to(bc[:, last:last + 1], (1, LANES))
        m_new = jnp.broadcast_to(m_t[:, last:last + 1], (1, LANES))
        kw = k * jnp.exp2(b_last + acol - m_new)
        decay = jnp.exp2(b_last + m2 - m_new)
        ct_new = decay * ct + _dot(vt, kw.astype(BF16))
        n_new = decay * n + jnp.sum(kw, axis=0, keepdims=True)
        return ht, ct_new, n_new, m_new

    def step(i, carry):
        cf = i * ML_CHUNK
        cb = (nc - 1 - i) * ML_CHUNK
        if not isinstance(i, int):
            cf = pl.multiple_of(cf, ML_CHUNK)
            cb = pl.multiple_of(cb, ML_CHUNK)
        new = []
        for j in range(hps):
            hs = slice(LANES * j, LANES * (j + 1))
            for d, (c0, ht_s) in enumerate(((cf, ht_f), (cb, ht_b))):
                ht, ct, n, m2 = chunk(j, d, c0, *carry[2 * j + d])
                ht_s[hs, pl.ds(c0, ML_CHUNK)] = ht
                new.append((ct, n, m2))
        return tuple(new)

    if has_init:
        init = tuple((c0_ref[d, j].T, n0_ref[j, d:d + 1, :], m0_ref[j, d:d + 1, :] * LOG2E)
                     for j in range(hps) for d in range(2))
    else:
        init = tuple((jnp.zeros(sq, F32), jnp.zeros((1, LANES), F32), jnp.zeros((1, LANES), F32))
                     for j in range(hps) for d in range(2))
    if nc <= 2:
        carry = init
        for i in range(nc):
            carry = step(i, carry)
    else:
        carry = lax.fori_loop(0, nc, step, init, unroll=4)

    hsum = (ht_f[...] + ht_b[...]).T
    for j in range(hps):
        hs = slice(LANES * j, LANES * (j + 1))
        hn = _rms(hsum[:, hs], go_ref[:, hs])
        out_ref[:, hs] = (hn * jax.nn.sigmoid(o_ref[:, hs])).astype(out_ref.dtype)

    if not has_init:
        for j in range(hps):
            for d in range(2):
                ct, n, m2 = carry[2 * j + d]
                cfin_ref[d, j] = ct.T
            nf, nb = carry[2 * j][1], carry[2 * j + 1][1]
            mf, mb = carry[2 * j][2], carry[2 * j + 1][2]
            sfin_ref[j] = jnp.concatenate(
                [nf, nb, mf * (1.0 / LOG2E), mb * (1.0 / LOG2E), jnp.zeros((4, LANES), F32)], axis=0)


def _mlstm(u, v_ml, o_ml, gates, l, wts, init):
    b, t, _ = u.shape
    nc = t // ML_CHUNK
    has_init = init is not None
    hps = ML_HEADS if t <= 4 * ML_CHUNK else GATE_HEADS
    w2 = hps * LANES
    pair_blk = lambda i, p: (i, 0, p)
    in_specs = [
        pl.BlockSpec((None, t, w2), pair_blk),
        pl.BlockSpec((None, t, w2), pair_blk),
        pl.BlockSpec((None, t, w2), pair_blk),
        pl.BlockSpec((hps // GATE_HEADS, 8, t), lambda i, p: (p, 0, i)),
        pl.BlockSpec((None, 3, w2), lambda i, p: (l, 0, p)),
        pl.BlockSpec((None, 1, w2), lambda i, p: (l, 0, p)),
        pl.BlockSpec((None, hps, ML_DH, ML_DH), lambda i, p: (l, p, 0, 0)),
        pl.BlockSpec((None, hps, ML_DH, ML_DH), lambda i, p: (l, p, 0, 0)),
        pl.BlockSpec((None, 1, w2), lambda i, p: (l, 0, p)),
    ]
    args = [u, v_ml, o_ml, gates, wts["w_ml_conv"], wts["b_ml_conv"], wts["w_ml_qt"], wts["w_ml_k"],
            wts["g_ml_out"]]
    out_specs = [pl.BlockSpec((None, t, w2), pair_blk)]
    out_shape = [jax.ShapeDtypeStruct((b, t, ML_WIDTH), BF16)]
    state_c = pl.BlockSpec((None, 2, hps, ML_DH, ML_DH), lambda i, p: (i, 0, p, 0, 0))
    state_s = pl.BlockSpec((None, hps, 8, LANES), lambda i, p: (i, p, 0, 0))
    if has_init:
        in_specs += [pl.BlockSpec((None, None, 2, hps, ML_DH, ML_DH), lambda i, p: (i, l, 0, p, 0, 0)),
                     pl.BlockSpec((None, None, hps, 8, LANES), lambda i, p: (i, l, p, 0, 0)),
                     pl.BlockSpec((None, None, hps, 8, LANES), lambda i, p: (i, l, p, 0, 0))]
        args += list(init)
    else:
        out_specs += [state_c, state_s]
        out_shape += [
            jax.ShapeDtypeStruct((b, 2, ML_HEADS, ML_DH, ML_DH), F32),
            jax.ShapeDtypeStruct((b, ML_HEADS, 8, LANES), F32),
        ]
    return pl.pallas_call(
        functools.partial(_mlstm_kernel, has_init, nc, hps),
        grid=(b, ML_HEADS // hps),
        in_specs=in_specs,
        out_specs=out_specs,
        out_shape=out_shape,
        scratch_shapes=[
            pltpu.VMEM((w2, t), BF16),
            pltpu.VMEM((t, w2), F32),
            pltpu.VMEM((w2, t), BF16),
            pltpu.VMEM((w2, t), F32),
            pltpu.VMEM((w2, t), F32),
            pltpu.VMEM((2 * hps, t, LANES), F32),
            pltpu.VMEM((hps // GATE_HEADS, 8, t), F32),
            pltpu.VMEM((hps, t, LANES), F32),
        ],
        compiler_params=_cparams(("parallel", "parallel")),
        name="mlstm_lat" if has_init else "mlstm_ctx",
    )(*args)


def _ffn_kernel(seq, tm, final, *refs):
    x_refs, oa_refs, ob_refs, oc_refs = refs[0:3], refs[3:6], refs[6:9], refs[9:12]
    (mod_ref, g2_ref, wout_ref, wup_ref, cw_ref, cb_ref, wdn_ref, gf_ref, y_ref,
     oext_s, yext_s, gext_s, h_s) = refs[12:]
    i = pl.program_id(0)
    a_w = MLA_HEADS * MLA_V
    ext = ((0, HALO), (HALO, tm), (HALO + tm, HALO))
    for (r0, n), k in zip(ext, (1, 0, 2)):
        oext_s[r0:r0 + n, 0:a_w] = oa_refs[k][...]
        oext_s[r0:r0 + n, a_w:a_w + ML_WIDTH] = ob_refs[k][...]
        oext_s[r0:r0 + n, a_w + ML_WIDTH:] = oc_refs[k][...]
    mix = _dot(oext_s[...], wout_ref[...])
    gate1 = mod_ref[2:3, :]
    x = x_refs[0][...] + gate1 * mix[HALO:HALO + tm, :]
    xp = x_refs[1][...] + gate1 * mix[0:HALO, :]
    xn = x_refs[2][...] + gate1 * mix[HALO + tm:, :]

    g2 = g2_ref[...]
    shift = mod_ref[3:4, :]
    scale = mod_ref[4:5, :]
    spans = tm > seq
    if spans:
        keep_p = keep_n = 1.0
        tok = lax.broadcasted_iota(jnp.int32, (tm, FF_CHUNK), 0) % seq
        first, last = tok == 0, tok == seq - 1
    else:
        keep_p = jnp.where((i * tm) % seq == 0, 0.0, 1.0)
        keep_n = jnp.where(((i + 1) * tm) % seq == 0, 0.0, 1.0)
    yext_s[0:HALO, :] = (_adaln(xp, g2, shift, scale) * keep_p).astype(BF16)
    yext_s[HALO:HALO + tm, :] = _adaln(x, g2, shift, scale).astype(BF16)
    yext_s[HALO + tm:, :] = (_adaln(xn, g2, shift, scale) * keep_n).astype(BF16)

    for c in range(D_FF // FF_CHUNK):
        c0 = c * FF_CHUNK
        a = _dot(yext_s[HALO:HALO + tm, :], wup_ref[:, c0:c0 + FF_CHUNK])
        gext_s[...] = _dot(yext_s[...], wup_ref[:, D_FF + c0:D_FF + c0 + FF_CHUNK])
        g_prev = gext_s[HALO - 1:HALO - 1 + tm, :]
        g_next = gext_s[HALO + 1:HALO + 1 + tm, :]
        if spans:
            g_prev = jnp.where(first, 0.0, g_prev)
            g_next = jnp.where(last, 0.0, g_next)
        g = (cw_ref[0:1, c0:c0 + FF_CHUNK] * g_prev
             + cw_ref[1:2, c0:c0 + FF_CHUNK] * gext_s[HALO:HALO + tm, :]
             + cw_ref[2:3, c0:c0 + FF_CHUNK] * g_next
             + cb_ref[:, c0:c0 + FF_CHUNK])
        h_s[:, c0:c0 + FF_CHUNK] = (_silu(g) * a).astype(BF16)
    y = x + mod_ref[5:6, :] * _dot(h_s[...], wdn_ref[...])
    if final:
        y = _rms(y, gf_ref[...])
    y_ref[...] = y


def _ffn(x2, o_a, o_b, o_c, mod_all, l, seq, is_ctx, wts, g_final, final):
    m = x2.shape[0]
    tm = TM
    assert tm % seq == 0 or seq % tm == 0
    nb = tm // HALO
    last_blk = m // HALO - 1
    row = lambda i: (i, 0)
    prev = lambda i: (jnp.maximum(i * nb - 1, 0), 0)
    nxt = lambda i: (jnp.minimum((i + 1) * nb, last_blk), 0)
    in_specs, args = [], []
    for a in (x2, o_a, o_b, o_c):
        w = a.shape[1]
        in_specs += [pl.BlockSpec((tm, w), row), pl.BlockSpec((HALO, w), prev), pl.BlockSpec((HALO, w), nxt)]
        args += [a, a, a]
    names = ["g_norm2", "w_out", "w_ff_up", "w_ff_conv", "b_ff_conv", "w_ff_down"]
    in_specs += ([_mod_spec(l, seq, tm, is_ctx)] + [_lspec(wts[n], l) for n in names]
                 + [pl.BlockSpec((1, D_MODEL), lambda i: (0, 0))])
    args += [mod_all] + [wts[n] for n in names] + [g_final]
    return pl.pallas_call(
        functools.partial(_ffn_kernel, seq, tm, final),
        grid=(m // tm,),
        in_specs=in_specs,
        out_specs=pl.BlockSpec((tm, D_MODEL), row),
        out_shape=jax.ShapeDtypeStruct((m, D_MODEL), F32),
        scratch_shapes=[pltpu.VMEM((tm + 2 * HALO, D_MODEL), BF16),
                        pltpu.VMEM((tm + 2 * HALO, D_MODEL), BF16),
                        pltpu.VMEM((tm + 2 * HALO, FF_CHUNK), F32),
                        pltpu.VMEM((tm, D_FF), BF16)],
        compiler_params=_cparams(("parallel",)),
        name="ffn_ctx" if is_ctx else "ffn_lat",
    )(*args)


def _rope_tables(t):
    pos = np.arange(t)
    rows = (pos // GRID_W).astype(np.float32)
    cols = (pos % GRID_W).astype(np.float32)

    def group(p, d):
        inv = (np.float32(ROPE_THETA) ** (-np.arange(0, d, 2, dtype=np.float32) / np.float32(d))).astype(np.float32)
        ang = (p[:, None] * inv[None, :]).astype(np.float32)
        cs, sn, z = np.cos(ang), np.sin(ang), np.zeros_like(ang)
        return (np.concatenate([cs, cs], 1), np.concatenate([-sn, z], 1), np.concatenate([z, sn], 1))

    gr, gc = group(rows, MLA_ROPE // 2), group(cols, MLA_ROPE // 2)
    ones = np.ones((t, MLA_NOPE), np.float32)
    zeros = np.zeros((t, MLA_NOPE), np.float32)
    pad1 = np.ones((t, LANES - MLA_NOPE - MLA_ROPE), np.float32)
    pad0 = np.zeros((t, LANES - MLA_NOPE - MLA_ROPE), np.float32)
    mla = (np.concatenate([ones, gr[0], gc[0], pad1], 1),
           np.concatenate([zeros, gr[1], gc[1], pad0], 1),
           np.concatenate([zeros, gr[2], gc[2], pad0], 1))
    gr, gc = group(rows, GQA_DH // 2), group(cols, GQA_DH // 2)
    gqa = tuple(np.concatenate([gr[k], gc[k], gr[k], gc[k]], 1) for k in range(3))
    return tuple(jnp.asarray(a, F32) for a in mla + gqa)


def _prep_weights(p):
    w_in = p["w_in"]
    nl = w_in.shape[0]
    qg = [w_in[:, :, 1968 + GQA_DH * h:1968 + GQA_DH * (h + 1)] for h in (0, 2, 1, 3)]
    gh = GATE_HEADS
    gsrc = np.array([kind * ML_HEADS + gh * pr + j for pr in range(2) for kind in range(4) for j in range(gh)])
    gdst = np.array([LANES * pr + gh * kind + j for pr in range(2) for kind in range(4) for j in range(gh)])
    w_gates = jnp.zeros((nl, D_MODEL, GATE_COLS), F32).at[:, :, gdst].set(w_in[:, :, 1952 + gsrc])
    b_gates = jnp.zeros((nl, GATE_COLS), F32).at[:, gdst].set(p["b_ml_gates"][:, gsrc])
    w_in_p = jnp.concatenate([
        w_in[:, :, 0:384],
        jnp.pad(w_in[:, :, 384:416], ((0, 0), (0, 0), (MLA_NOPE, LANES - MLA_NOPE - MLA_ROPE))),
        w_in[:, :, 416:1952],
        w_gates,
        *qg,
        w_in[:, :, 2224:2480],
    ], axis=2).astype(BF16)
    w_uq = p["w_mla_uq"].reshape(nl, MLA_Q_RANK, MLA_HEADS, MLA_NOPE + MLA_ROPE)
    w_uq = jnp.pad(w_uq, ((0, 0), (0, 0), (0, 0), (0, LANES - MLA_NOPE - MLA_ROPE)))
    w_ukv = p["w_mla_ukv"].reshape(nl, MLA_KV_RANK, MLA_HEADS, MLA_NOPE + MLA_V)
    w_uk = jnp.pad(w_ukv[..., :MLA_NOPE], ((0, 0), (0, 0), (0, 0), (0, LANES - MLA_NOPE)))
    w_ukv_p = jnp.concatenate([w_uk.reshape(nl, MLA_KV_RANK, -1),
                               w_ukv[..., MLA_NOPE:].reshape(nl, MLA_KV_RANK, -1)], axis=2)
    grp = np.arange(2 * LANES) // GQA_DH
    row = lambda a: a[:, None, :]
    return {
        "g_norm1": row(p["g_norm1"]),
        "g_norm2": row(p["g_norm2"]),
        "w_in": w_in_p,
        "g_mla_q": row(p["g_mla_q"]),
        "w_uq": w_uq.reshape(nl, MLA_Q_RANK, -1).astype(BF16),
        "g_mla_kv": row(p["g_mla_kv"]),
        "w_ukv": w_ukv_p.astype(BF16),
        "b_gates": row(b_gates),
        "g_gqa_q": row(jnp.tile(p["g_gqa_q"], (1, GQA_HEADS))),
        "g_gqa_k": row(jnp.tile(p["g_gqa_k"], (1, GQA_KV_HEADS))),
        "gsum": jnp.asarray(grp[:, None] == grp[None, :], BF16),
        "w_ml_conv": p["w_ml_conv"],
        "b_ml_conv": row(p["b_ml_conv"]),
        "w_ml_qt": jnp.swapaxes(p["w_ml_q"], 2, 3).astype(BF16),
        "w_ml_k": p["w_ml_k"].astype(BF16),
        "g_ml_out": row(p["g_ml_out"]),
        "w_out": p["w_out"].astype(BF16),
        "w_ff_up": p["w_ff_up"].astype(BF16),
        "w_ff_conv": p["w_ff_conv"],
        "b_ff_conv": row(p["b_ff_conv"]),
        "w_ff_down": p["w_ff_down"].astype(BF16),
    }


def _layer(x2, mod_all, l, seq, is_ctx, wts, tabs, cache, g_final, final):
    m = x2.shape[0]
    b = m // seq
    r3 = lambda a: a.reshape(b, seq, a.shape[-1])
    outs = _in_proj(x2, mod_all, l, seq, is_ctx, wts, tabs)
    q_m, k_m, v_m, u, v_ml, o_ml, gates, q_g, k_g, v_g = outs[:10]
    segs_a = [(r3(k_m), r3(v_m), None)]
    segs_c = [(r3(k_g), r3(v_g), None)]
    init = None
    if not is_ctx:
        ckv_c, kr_pad, kg_c, vg_c, ct0, n0p, m0p = cache
        kc, vc = _cache_kv(ckv_c, kr_pad, l, wts["w_ukv"])
        segs_a.append((kc, vc, None))
        segs_c.append((kg_c, vg_c, l))
        init = (ct0, n0p, m0p)
    o_a = _attention(True, r3(q_m), segs_a)
    o_c = _attention(False, r3(q_g), segs_c)
    ml = _mlstm(r3(u), r3(v_ml), r3(o_ml), gates, l, wts, init)
    x2 = _ffn(x2, o_a.reshape(m, -1), ml[0].reshape(m, -1), o_c.reshape(m, -1), mod_all, l, seq, is_ctx, wts,
              g_final, final)
    state = None
    if is_ctx:
        ckv_n, kr_raw = outs[10:12]
        sfin = ml[2]
        state = (
            r3(ckv_n),
            r3(kr_raw)[:, :, MLA_NOPE:MLA_NOPE + MLA_ROPE],
            k_g.reshape(b, seq, GQA_KV_HEADS, GQA_DH),
            v_g.reshape(b, seq, GQA_KV_HEADS, GQA_DH),
            ml[1],
            jnp.swapaxes(sfin[:, :, 0:2, :], 1, 2),
            jnp.swapaxes(sfin[:, :, 2:4, 0], 1, 2),
        )
    return x2, state


def kernel(x_prompt, x_sample, cache_mla_ckv, cache_mla_krope, cache_gqa_k, cache_gqa_v, state_mlstm_C, state_mlstm_n, state_mlstm_m, c, c_ctx, w_ada, b_ada, g_norm1, g_norm2, w_in, g_mla_q, w_mla_uq, g_mla_kv, w_mla_ukv, w_ml_conv, b_ml_conv, w_ml_q, w_ml_k, b_ml_gates, g_ml_out, g_gqa_q, g_gqa_k, w_out, w_ff_up, w_ff_conv, b_ff_conv, w_ff_down, g_final):
    params = {
        "g_norm1": g_norm1, "g_norm2": g_norm2, "w_in": w_in, "g_mla_q": g_mla_q, "w_mla_uq": w_mla_uq,
        "g_mla_kv": g_mla_kv, "w_mla_ukv": w_mla_ukv, "w_ml_conv": w_ml_conv, "b_ml_conv": b_ml_conv,
        "w_ml_q": w_ml_q, "w_ml_k": w_ml_k, "b_ml_gates": b_ml_gates, "g_ml_out": g_ml_out,
        "g_gqa_q": g_gqa_q, "g_gqa_k": g_gqa_k, "w_out": w_out, "w_ff_up": w_ff_up, "w_ff_conv": w_ff_conv,
        "b_ff_conv": b_ff_conv, "w_ff_down": w_ff_down,
    }
    depth = w_in.shape[0]
    bp, sp, _ = x_prompt.shape
    bs, ss, _ = x_sample.shape
    assert bs + 1 <= 16 and sp % ML_CHUNK == 0 and ss % TM == 0 and (bp * sp) % TM == 0

    cvec = jnp.concatenate([c_ctx[None, :], c, jnp.zeros((16 - 1 - bs, D_MODEL), F32)], axis=0)
    mod_all = _modulation(cvec, w_ada, b_ada).reshape(depth, 16, 6, D_MODEL)
    tabs = _rope_tables(ss)
    gf = g_final[None, :]
    wts = _prep_weights(params)

    tc = cache_mla_ckv.shape[2]
    pad_rows = ((0, 0), (0, 0), (0, 0), (0, 8 - 2), (0, 0))
    cache = (
        cache_mla_ckv,
        jnp.pad(cache_mla_krope, ((0, 0), (0, 0), (0, 0), (MLA_NOPE, LANES - MLA_NOPE - MLA_ROPE))),
        cache_gqa_k.reshape(bs, depth, tc, GQA_KV_HEADS * GQA_DH),
        cache_gqa_v.reshape(bs, depth, tc, GQA_KV_HEADS * GQA_DH),
        state_mlstm_C,
        jnp.pad(jnp.swapaxes(state_mlstm_n, 2, 3), pad_rows),
        jnp.pad(jnp.broadcast_to(jnp.swapaxes(state_mlstm_m, 2, 3)[..., None], (bs, depth, ML_HEADS, 2, LANES)),
                pad_rows),
    )

    xp = x_prompt.reshape(bp * sp, D_MODEL)
    xs = x_sample.reshape(bs * ss, D_MODEL)
    states = []
    for l in range(depth):
        final = l == depth - 1
        xp, st = _layer(xp, mod_all, l, sp, True, wts, None, None, gf, final)
        states.append(st)
        xs, _ = _layer(xs, mod_all, l, ss, False, wts, tabs, cache, gf, final)
    new_state = tuple(jnp.stack([st[k] for st in states], axis=1) for k in range(7))
    return (xp.reshape(bp, sp, D_MODEL), xs.reshape(bs, ss, D_MODEL)) + new_state
```

```python
import functools

import jax
import jax.numpy as jnp
import numpy as np
from jax import lax
from jax.experimental import pallas as pl
from jax.experimental.pallas import tpu as pltpu

F32 = jnp.float32
BF16 = jnp.bfloat16

D_MODEL = 1024
GRID_W = 64
ROPE_THETA = 10000.0
EPS = 1e-6
MLA_HEADS = 4
MLA_NOPE = 64
MLA_ROPE = 32
MLA_V = 64
MLA_Q_RANK = 256
MLA_KV_RANK = 128
ML_HEADS = 4
ML_DH = 128
ML_WIDTH = ML_HEADS * ML_DH
ML_CHUNK = 128
GQA_HEADS = 4
GQA_KV_HEADS = 2
GQA_DH = 64
D_FF = 2816
MLA_SCALE = (MLA_NOPE + MLA_ROPE) ** -0.5
GQA_SCALE = GQA_DH ** -0.5
ML_K_SCALE = ML_DH ** -0.5
LOG2E = 1.4426950408889634

LANES = 128
HALO = 16
VMEM_LIMIT = 52 * 1024 * 1024

C_CQ, C_CKV, C_KR, C_U, C_V, C_O, C_G, C_QG, C_KG, C_VG, IN_COLS_P = (
    0, 256, 384, 512, 1024, 1536, 2048, 2304, 2560, 2688, 2816)
GATE_COLS = 2 * LANES
FF_CHUNK = 256
TM = 512
TQ = 512
KEY_CHUNK = 4096


def _cparams(sem):
    return pltpu.CompilerParams(dimension_semantics=sem, vmem_limit_bytes=VMEM_LIMIT)


def _lspec(arr, l):
    nd = arr.ndim - 1
    return pl.BlockSpec((None,) + arr.shape[1:], lambda *_: (l,) + (0,) * nd, pipeline_mode=pl.Buffered(1))


def _dot(a, b):
    return jnp.dot(a, b, preferred_element_type=F32)


def _dot_nt(a, b):
    return lax.dot_general(a, b, (((1,), (1,)), ((), ())), preferred_element_type=F32)


def _dot_tn(a, b):
    return lax.dot_general(a, b, (((0,), (0,)), ((), ())), preferred_element_type=F32)


def _rms(x, g):
    return (x * lax.rsqrt(jnp.mean(x * x, axis=-1, keepdims=True) + EPS)) * g


def _silu(x):
    return x * jax.nn.sigmoid(x)


def _adaln(x, g, shift, scale):
    return _rms(x, g) * (1.0 + scale) + shift


def _split2(x):
    hi = x.astype(BF16)
    lo = (x - hi.astype(F32)).astype(BF16)
    return hi, lo


def _split3(x):
    h1 = x.astype(BF16)
    r1 = x - h1.astype(F32)
    h2 = r1.astype(BF16)
    h3 = (r1 - h2.astype(F32)).astype(BF16)
    return h1, h2, h3


def _rope(x, cos, sa, sb, half):
    w = x.shape[-1]
    return x * cos + pltpu.roll(x, w - half, 1) * sa + pltpu.roll(x, half, 1) * sb


def _group_rms(x, gsum, g):
    hi, lo = _split2(x * x)
    ss = _dot(hi, gsum) + _dot(lo, gsum)
    return (x * lax.rsqrt(ss * (1.0 / GQA_DH) + EPS)) * g


def _mod_kernel(c_ref, w_ref, b_ref, o_ref):
    a = _silu(c_ref[...]).astype(BF16)
    o_ref[...] = _dot(a, w_ref[...].astype(BF16)) + b_ref[...]


def _modulation(cvec, w_ada, b_ada):
    nl = w_ada.shape[0]
    tn = 1536
    return pl.pallas_call(
        _mod_kernel,
        grid=(nl, 6 * D_MODEL // tn),
        in_specs=[
            pl.BlockSpec((16, D_MODEL), lambda l, j: (0, 0)),
            pl.BlockSpec((None, D_MODEL, tn), lambda l, j: (l, 0, j)),
            pl.BlockSpec((None, 1, tn), lambda l, j: (l, 0, j)),
        ],
        out_specs=pl.BlockSpec((None, 16, tn), lambda l, j: (l, 0, j)),
        out_shape=jax.ShapeDtypeStruct((nl, 16, 6 * D_MODEL), F32),
        compiler_params=_cparams(("arbitrary", "arbitrary")),
        name="modulation",
    )(cvec, w_ada, b_ada.reshape(nl, 1, 6 * D_MODEL))


def _in_proj_kernel(is_ctx, x_ref, mod_ref, g1_ref, win_ref, gq_ref, wuq_ref, gkv_ref, wukv_ref,
                    bg_ref, ggq_ref, ggk_ref, gsum_ref, *rest):
    if is_ctx:
        (qm_ref, km_ref, vm_ref, u_ref, vml_ref, oml_ref, gt_ref, qg_ref, kg_ref, vg_ref,
         ckv_ref, kr_ref) = rest
    else:
        (cm_ref, sam_ref, sbm_ref, cg_ref, sag_ref, sbg_ref,
         qm_ref, km_ref, vm_ref, u_ref, vml_ref, oml_ref, gt_ref, qg_ref, kg_ref, vg_ref) = rest

    y = _adaln(x_ref[...], g1_ref[...], mod_ref[0:1, :], mod_ref[1:2, :]).astype(BF16)

    def proj(c0, width):
        return _dot(y, win_ref[:, c0:c0 + width])

    z_cq = proj(C_CQ, MLA_Q_RANK)
    ckv_kr = proj(C_CKV, MLA_KV_RANK + LANES)

    cqn = _rms(z_cq, gq_ref[...]).astype(BF16)
    z_qg = proj(C_QG, GQA_HEADS * GQA_DH)
    qz = _dot(cqn, wuq_ref[...])

    ckvn = _rms(ckv_kr[:, 0:MLA_KV_RANK], gkv_ref[...])
    kr = ckv_kr[:, MLA_KV_RANK:]
    if is_ctx:
        ckv_ref[...] = ckvn
        kr_ref[...] = kr
    else:
        kr = _rope(kr, cm_ref[...], sam_ref[...], sbm_ref[...], 8)
    kv_g = proj(C_KG, 2 * LANES)
    kvz = _dot(ckvn.astype(BF16), wukv_ref[...])

    z_u = proj(C_U, ML_WIDTH)
    for h in range(MLA_HEADS):
        blk = qz[:, LANES * h:LANES * (h + 1)]
        if not is_ctx:
            blk = _rope(blk, cm_ref[...], sam_ref[...], sbm_ref[...], 8)
        qm_ref[:, LANES * h:LANES * (h + 1)] = (blk * (MLA_SCALE * LOG2E)).astype(qm_ref.dtype)

    z_v = proj(C_V, ML_WIDTH)
    for h in range(MLA_HEADS):
        km_ref[:, LANES * h:LANES * (h + 1)] = (kvz[:, LANES * h:LANES * (h + 1)] + kr).astype(km_ref.dtype)
    vm_ref[...] = kvz[:, MLA_HEADS * LANES:].astype(vm_ref.dtype)
    u_ref[...] = z_u.astype(u_ref.dtype)

    z_o = proj(C_O, ML_WIDTH)
    qg = _group_rms(z_qg, gsum_ref[...], ggq_ref[...])
    kg = _group_rms(kv_g[:, 0:LANES], gsum_ref[0:LANES, 0:LANES], ggk_ref[...])
    vml_ref[...] = z_v.astype(vml_ref.dtype)
    gates = proj(C_G, GATE_COLS) + bg_ref[...]
    if not is_ctx:
        qg = jnp.concatenate(
            [_rope(qg[:, 0:LANES], cg_ref[...], sag_ref[...], sbg_ref[...], 16),
             _rope(qg[:, LANES:], cg_ref[...], sag_ref[...], sbg_ref[...], 16)], axis=1)
        kg = _rope(kg, cg_ref[...], sag_ref[...], sbg_ref[...], 16)
    qg_ref[...] = (qg * (GQA_SCALE * LOG2E)).astype(qg_ref.dtype)
    kg_ref[...] = kg.astype(kg_ref.dtype)
    vg_ref[...] = kv_g[:, LANES:].astype(vg_ref.dtype)

    oml_ref[...] = z_o.astype(oml_ref.dtype)
    for pr in range(GATE_COLS // LANES):
        gt_ref[pr] = gates[:, LANES * pr:LANES * (pr + 1)].T[0:8, :]


def _mod_spec(l, seq, tm, is_ctx):
    if is_ctx:
        return pl.BlockSpec((None, None, 6, D_MODEL), lambda i: (l, 0, 0, 0))
    return pl.BlockSpec((None, None, 6, D_MODEL), lambda i: (l, 1 + (i * tm) // seq, 0, 0))


def _in_proj(x2, mod_all, l, seq, is_ctx, wts, tabs):
    m = x2.shape[0]
    tm = TM
    nt = m // tm
    row = lambda i: (i, 0)
    names = ["g_norm1", "w_in", "g_mla_q", "w_uq", "g_mla_kv", "w_ukv", "b_gates", "g_gqa_q", "g_gqa_k"]
    in_specs = ([pl.BlockSpec((tm, D_MODEL), row), _mod_spec(l, seq, tm, is_ctx)]
                + [_lspec(wts[n], l) for n in names]
                + [pl.BlockSpec((2 * LANES, 2 * LANES), lambda i: (0, 0))])
    args = [x2, mod_all] + [wts[n] for n in names] + [wts["gsum"]]
    if not is_ctx:
        tpos = seq // tm
        in_specs += [pl.BlockSpec((tm, LANES), lambda i: (i % tpos, 0))] * 6
        args += list(tabs)
    act = F32 if is_ctx else BF16
    outs = [
        (MLA_HEADS * LANES, BF16),
        (MLA_HEADS * LANES, BF16),
        (MLA_HEADS * MLA_V, BF16),
        (ML_WIDTH, F32),
        (ML_WIDTH, BF16),
        (ML_WIDTH, F32),
        (None, F32),
        (GQA_HEADS * GQA_DH, BF16),
        (LANES, act),
        (LANES, act),
    ]
    if is_ctx:
        outs += [(MLA_KV_RANK, F32), (LANES, F32)]
    return pl.pallas_call(
        functools.partial(_in_proj_kernel, is_ctx),
        grid=(nt,),
        in_specs=in_specs,
        out_specs=[pl.BlockSpec((GATE_COLS // LANES, 8, tm), lambda i: (0, 0, i)) if w is None
                   else pl.BlockSpec((tm, w), row) for w, _ in outs],
        out_shape=[jax.ShapeDtypeStruct((GATE_COLS // LANES, 8, m) if w is None else (m, w), dt)
                   for w, dt in outs],
        compiler_params=_cparams(("parallel",)),
        name="in_proj_ctx" if is_ctx else "in_proj_lat",
    )(*args)


def _cache_kv_kernel(ckv_ref, kr_ref, wukv_ref, k_ref, v_ref):
    for i in range(ckv_ref.shape[0]):
        kvz = _dot(ckv_ref[i].astype(BF16), wukv_ref[...])
        kr = kr_ref[i]
        for h in range(MLA_HEADS):
            k_ref[i, :, LANES * h:LANES * (h + 1)] = (kvz[:, LANES * h:LANES * (h + 1)] + kr).astype(k_ref.dtype)
        v_ref[i] = kvz[:, MLA_HEADS * LANES:].astype(v_ref.dtype)


def _cache_kv(ckv, kr_pad, l, w_ukv):
    b, _, tc, _ = ckv.shape
    cache_blk = pl.BlockSpec((b, None, tc, LANES), lambda i: (0, l, 0, 0))
    return pl.pallas_call(
        _cache_kv_kernel,
        grid=(1,),
        in_specs=[cache_blk, cache_blk, _lspec(w_ukv, l)],
        out_specs=[pl.BlockSpec((b, tc, MLA_HEADS * LANES), lambda i: (0, 0, 0)),
                   pl.BlockSpec((b, tc, MLA_HEADS * MLA_V), lambda i: (0, 0, 0))],
        out_shape=[jax.ShapeDtypeStruct((b, tc, MLA_HEADS * LANES), BF16),
                   jax.ShapeDtypeStruct((b, tc, MLA_HEADS * MLA_V), BF16)],
        compiler_params=_cparams(("arbitrary",)),
        name="cache_kv",
    )(ckv, kr_pad, w_ukv)


def _attn_kernel(is_mla, nseg, q_ref, *refs):
    kv = refs[:2 * nseg]
    o_ref = refs[2 * nseg]
    tq = q_ref.shape[0]
    lo = lax.broadcasted_iota(jnp.int32, (tq, LANES), 1) < (LANES // 2)
    heads = []
    pieces = []
    for j in range(nseg):
        tk = kv[2 * j].shape[0]
        for c0 in range(0, tk, KEY_CHUNK):
            pieces.append((kv[2 * j], kv[2 * j + 1], slice(c0, min(c0 + KEY_CHUNK, tk))))

    def qk(h):
        if is_mla:
            qh = q_ref[:, LANES * h:LANES * (h + 1)]
            ksl = slice(LANES * h, LANES * (h + 1))
        else:
            blk = q_ref[:, LANES * (h % 2):LANES * (h % 2 + 1)].astype(F32)
            qh = jnp.where(lo if h < 2 else jnp.logical_not(lo), blk, 0.0).astype(BF16)
            ksl = slice(0, LANES)
        return [_dot_nt(qh, k_ref[rs, ksl].astype(BF16)) for k_ref, _, rs in pieces]

    def softmax(ss):
        mx = jnp.max(ss[0], axis=-1, keepdims=True)
        for s in ss[1:]:
            mx = jnp.maximum(mx, jnp.max(s, axis=-1, keepdims=True))
        es = [jnp.exp2(s - mx) for s in ss]
        den = es[0].sum(axis=-1, keepdims=True)
        for e in es[1:]:
            den = den + e.sum(axis=-1, keepdims=True)
        return [e.astype(BF16) for e in es], den

    def pv(h, es, den):
        vsl = slice(LANES * (h // 2), LANES * (h // 2 + 1)) if is_mla else slice(0, LANES)
        acc = None
        for e, (_, v_ref, rs) in zip(es, pieces):
            a = _dot(e, v_ref[rs, vsl].astype(BF16))
            acc = a if acc is None else acc + a
        return acc / den

    nh = 4
    ahead = qk(0)
    for h in range(nh):
        ss = ahead
        if h + 1 < nh:
            ahead = qk(h + 1)
        heads.append(pv(h, *softmax(ss)))
    if is_mla:
        o_ref[:, 0:LANES] = jnp.where(lo, heads[0], heads[1]).astype(o_ref.dtype)
        o_ref[:, LANES:] = jnp.where(lo, heads[2], heads[3]).astype(o_ref.dtype)
    else:
        half = LANES // 2
        o_ref[:, 0:LANES] = jnp.where(lo, heads[0], pltpu.roll(heads[1], half, 1)).astype(o_ref.dtype)
        o_ref[:, LANES:] = jnp.where(lo, pltpu.roll(heads[2], half, 1), heads[3]).astype(o_ref.dtype)


def _attention(is_mla, q, segs):
    b, tq_all, wq = q.shape
    tq = min(TQ, tq_all)
    in_specs = [pl.BlockSpec((None, tq, wq), lambda i, j: (i, j, 0))]
    args = [q]
    for k, v, layer in segs:
        for a in (k, v):
            if layer is None:
                in_specs.append(pl.BlockSpec((None,) + a.shape[1:], lambda i, j: (i, 0, 0)))
            else:
                in_specs.append(pl.BlockSpec((None, None) + a.shape[2:], lambda i, j, layer=layer: (i, layer, 0, 0)))
            args.append(a)
    return pl.pallas_call(
        functools.partial(_attn_kernel, is_mla, len(segs)),
        grid=(b, tq_all // tq),
        in_specs=in_specs,
        out_specs=pl.BlockSpec((None, tq, 2 * LANES), lambda i, j: (i, j, 0)),
        out_shape=jax.ShapeDtypeStruct((b, tq_all, 2 * LANES), BF16),
        compiler_params=_cparams(("parallel", "parallel")),
        name=("attn_mla" if is_mla else "attn_gqa") + ("_lat" if len(segs) > 1 else "_ctx"),
    )(*args)


def _log_sigmoid(x):
    return jnp.minimum(x, 0.0) - jnp.log(1.0 + jnp.exp(-jnp.abs(x)))


GATE_HEADS = 2


def _mlstm_kernel(has_init, nc, hps, u_ref, v_ref, o_ref, gt_ref, cw_ref, cb_ref, wqt_ref, wk_ref, go_ref,
                  *rest):
    if has_init:
        c0_ref, n0_ref, m0_ref, out_ref, qt_s, k_s, vt_s, ht_f, ht_b, acol_s, row_s, st_s = rest
    else:
        out_ref, cfin_ref, sfin_ref, qt_s, k_s, vt_s, ht_f, ht_b, acol_s, row_s, st_s = rest
    t = nc * ML_CHUNK
    sq = (ML_CHUNK, ML_CHUNK)
    w2 = hps * LANES

    u = u_ref[...]
    rows = lax.broadcasted_iota(jnp.int32, (t, w2), 0)
    up = jnp.where(rows == 0, 0.0, pltpu.roll(u, 1, 0))
    un = jnp.where(rows == t - 1, 0.0, pltpu.roll(u, t - 1, 0))
    uc = _silu(cw_ref[0:1, :] * up + cw_ref[1:2, :] * u + cw_ref[2:3, :] * un + cb_ref[...]).astype(BF16)
    r_i = lax.broadcasted_iota(jnp.int32, sq, 0)
    c_i = lax.broadcasted_iota(jnp.int32, sq, 1)
    lower = c_i <= r_i
    upper = c_i >= r_i
    eye = jnp.where(c_i == r_i, 1.0, 0.0).astype(BF16)
    for j in range(hps):
        hs = slice(LANES * j, LANES * (j + 1))
        qt_s[hs, :] = _dot_nt(wqt_ref[j], uc[:, hs]).astype(BF16)
        k_s[:, hs] = _dot(uc[:, hs], wk_ref[j]) * ML_K_SCALE
        vt_s[hs, :] = _dot_nt(eye, v_ref[:, hs]).astype(BF16)

    tri_lo = jnp.where(lower, 1.0, 0.0).astype(BF16)
    tri_up = jnp.where(upper, 1.0, 0.0).astype(BF16)
    row8 = lax.broadcasted_iota(jnp.int32, (8, ML_CHUNK), 0)
    is_f = lax.broadcasted_iota(jnp.int32, (8, t), 0) % 4 >= 2
    for pb in range(hps // GATE_HEADS):
        gates = gt_ref[pb]
        lg_all = jnp.where(is_f, _log_sigmoid(gates), gates)
        for c in range(nc):
            sl = slice(c * ML_CHUNK, (c + 1) * ML_CHUNK)
            lg = lg_all[:, sl]
            h1, h2, h3 = _split3(lg)
            cs = jnp.where(row8 < 4, _dot(h1, tri_up) + _dot(h2, tri_up) + _dot(h3, tri_up),
                           _dot(h1, tri_lo) + _dot(h2, tri_lo) + _dot(h3, tri_lo)) * LOG2E
            row_s[pb, :, sl] = cs
            a = lg * LOG2E - pltpu.roll(cs, 6, 0)
            for jj in range(GATE_HEADS):
                j = GATE_HEADS * pb + jj
                hs = slice(LANES * j, LANES * (j + 1))
                st_s[j, sl, :] = _dot(k_s[sl, hs].astype(BF16), qt_s[hs, sl])
                for d in range(2):
                    r = 4 * d + jj
                    acol_s[2 * j + d, sl, :] = jnp.broadcast_to(a[r:r + 1, :], sq).T

    def chunk(j, d, c0, ct, n, m2):
        hs = slice(LANES * j, LANES * (j + 1))
        cs_ = pl.ds(c0, ML_CHUNK)
        last = 0 if d else ML_CHUNK - 1
        qt = qt_s[hs, cs_]
        k = k_s[cs_, hs]
        vt = vt_s[hs, cs_]
        acol = acol_s[2 * j + d, cs_, :]
        r = 2 + 4 * d + j % GATE_HEADS
        bc = row_s[j // GATE_HEADS, r:r + 1, cs_]
        g = bc + m2
        dlog = jnp.where(lower if d else upper, acol + bc, -jnp.inf)
        m_t = jnp.maximum(g, jnp.max(dlog, axis=0, keepdims=True))
        w = jnp.exp2(dlog - m_t)
        inter = jnp.exp2(g - m_t)
        st = st_s[j, cs_, :] * w
        qn = _dot(jnp.broadcast_to(n, (8, LANES)).astype(BF16), qt)[0:1, :]
        den = jnp.sum(st, axis=0, keepdims=True) + inter * qn
        numt = _dot(vt, st.astype(BF16)) + inter * _dot(ct.astype(BF16), qt)
        ht = numt * (1.0 / jnp.maximum(jnp.abs(den), jnp.exp2(-m_t)))
        b_last = jnp.broadcast_to(bc[:, last:last + 1], (1, LANES))
        m_new = jnp.broadcast_to(m_t[:, last:last + 1], (1, LANES))
        kw = k * jnp.exp2(b_last + acol - m_new)
        decay = jnp.exp2(b_last + m2 - m_new)
        ct_new = decay * ct + _dot(vt, kw.astype(BF16))
        n_new = decay * n + jnp.sum(kw, axis=0, keepdims=True)
        return ht, ct_new, n_new, m_new

    def step(i, carry):
        cf = i * ML_CHUNK
        cb = (nc - 1 - i) * ML_CHUNK
        if not isinstance(i, int):
            cf = pl.multiple_of(cf, ML_CHUNK)
            cb = pl.multiple_of(cb, ML_CHUNK)
        new = []
        for j in range(hps):
            hs = slice(LANES * j, LANES * (j + 1))
            for d, (c0, ht_s) in enumerate(((cf, ht_f), (cb, ht_b))):
                ht, ct, n, m2 = chunk(j, d, c0, *carry[2 * j + d])
                ht_s[hs, pl.ds(c0, ML_CHUNK)] = ht
                new.append((ct, n, m2))
        return tuple(new)

    if has_init:
        init = tuple((c0_ref[d, j].T, n0_ref[j, d:d + 1, :], m0_ref[j, d:d + 1, :] * LOG2E)
                     for j in range(hps) for d in range(2))
    else:
        init = tuple((jnp.zeros(sq, F32), jnp.zeros((1, LANES), F32), jnp.zeros((1, LANES), F32))
                     for j in range(hps) for d in range(2))
    if nc <= 2:
        carry = init
        for i in range(nc):
            carry = step(i, carry)
    else:
        carry = lax.fori_loop(0, nc, step, init, unroll=4)

    hsum = (ht_f[...] + ht_b[...]).T
    for j in range(hps):
        hs = slice(LANES * j, LANES * (j + 1))
        hn = _rms(hsum[:, hs], go_ref[:, hs])
        out_ref[:, hs] = (hn * jax.nn.sigmoid(o_ref[:, hs])).astype(out_ref.dtype)

    if not has_init:
        for j in range(hps):
            for d in range(2):
                ct, n, m2 = carry[2 * j + d]
                cfin_ref[d, j] = ct.T
            nf, nb = carry[2 * j][1], carry[2 * j + 1][1]
            mf, mb = carry[2 * j][2], carry[2 * j + 1][2]
            sfin_ref[j] = jnp.concatenate(
                [nf, nb, mf * (1.0 / LOG2E), mb * (1.0 / LOG2E), jnp.zeros((4, LANES), F32)], axis=0)


def _mlstm(u, v_ml, o_ml, gates, l, wts, init):
    b, t, _ = u.shape
    nc = t // ML_CHUNK
    has_init = init is not None
    hps = ML_HEADS if t <= 4 * ML_CHUNK else GATE_HEADS
    w2 = hps * LANES
    pair_blk = lambda i, p: (i, 0, p)
    in_specs = [
        pl.BlockSpec((None, t, w2), pair_blk),
        pl.BlockSpec((None, t, w2), pair_blk),
        pl.BlockSpec((None, t, w2), pair_blk),
        pl.BlockSpec((hps // GATE_HEADS, 8, t), lambda i, p: (p, 0, i)),
        pl.BlockSpec((None, 3, w2), lambda i, p: (l, 0, p)),
        pl.BlockSpec((None, 1, w2), lambda i, p: (l, 0, p)),
        pl.BlockSpec((None, hps, ML_DH, ML_DH), lambda i, p: (l, p, 0, 0)),
        pl.BlockSpec((None, hps, ML_DH, ML_DH), lambda i, p: (l, p, 0, 0)),
        pl.BlockSpec((None, 1, w2), lambda i, p: (l, 0, p)),
    ]
    args = [u, v_ml, o_ml, gates, wts["w_ml_conv"], wts["b_ml_conv"], wts["w_ml_qt"], wts["w_ml_k"],
            wts["g_ml_out"]]
    out_specs = [pl.BlockSpec((None, t, w2), pair_blk)]
    out_shape = [jax.ShapeDtypeStruct((b, t, ML_WIDTH), BF16)]
    state_c = pl.BlockSpec((None, 2, hps, ML_DH, ML_DH), lambda i, p: (i, 0, p, 0, 0))
    state_s = pl.BlockSpec((None, hps, 8, LANES), lambda i, p: (i, p, 0, 0))
    if has_init:
        in_specs += [pl.BlockSpec((None, None, 2, hps, ML_DH, ML_DH), lambda i, p: (i, l, 0, p, 0, 0)),
                     pl.BlockSpec((None, None, hps, 8, LANES), lambda i, p: (i, l, p, 0, 0)),
                     pl.BlockSpec((None, None, hps, 8, LANES), lambda i, p: (i, l, p, 0, 0))]
        args += list(init)
    else:
        out_specs += [state_c, state_s]
        out_shape += [
            jax.ShapeDtypeStruct((b, 2, ML_HEADS, ML_DH, ML_DH), F32),
            jax.ShapeDtypeStruct((b, ML_HEADS, 8, LANES), F32),
        ]
    return pl.pallas_call(
        functools.partial(_mlstm_kernel, has_init, nc, hps),
        grid=(b, ML_HEADS // hps),
        in_specs=in_specs,
        out_specs=out_specs,
        out_shape=out_shape,
        scratch_shapes=[
            pltpu.VMEM((w2, t), BF16),
            pltpu.VMEM((t, w2), F32),
            pltpu.VMEM((w2, t), BF16),
            pltpu.VMEM((w2, t), F32),
            pltpu.VMEM((w2, t), F32),
            pltpu.VMEM((2 * hps, t, LANES), F32),
            pltpu.VMEM((hps // GATE_HEADS, 8, t), F32),
            pltpu.VMEM((hps, t, LANES), F32),
        ],
        compiler_params=_cparams(("parallel", "parallel")),
        name="mlstm_lat" if has_init else "mlstm_ctx",
    )(*args)


def _ffn_kernel(seq, tm, final, *refs):
    x_refs, oa_refs, ob_refs, oc_refs = refs[0:3], refs[3:6], refs[6:9], refs[9:12]
    (mod_ref, g2_ref, wout_ref, wup_ref, cw_ref, cb_ref, wdn_ref, gf_ref, y_ref,
     oext_s, yext_s, gext_s, h_s) = refs[12:]
    i = pl.program_id(0)
    a_w = MLA_HEADS * MLA_V
    ext = ((0, HALO), (HALO, tm), (HALO + tm, HALO))
    for (r0, n), k in zip(ext, (1, 0, 2)):
        oext_s[r0:r0 + n, 0:a_w] = oa_refs[k][...]
        oext_s[r0:r0 + n, a_w:a_w + ML_WIDTH] = ob_refs[k][...]
        oext_s[r0:r0 + n, a_w + ML_WIDTH:] = oc_refs[k][...]
    mix = _dot(oext_s[...], wout_ref[...])
    gate1 = mod_ref[2:3, :]
    x = x_refs[0][...] + gate1 * mix[HALO:HALO + tm, :]
    xp = x_refs[1][...] + gate1 * mix[0:HALO, :]
    xn = x_refs[2][...] + gate1 * mix[HALO + tm:, :]

    g2 = g2_ref[...]
    shift = mod_ref[3:4, :]
    scale = mod_ref[4:5, :]
    spans = tm > seq
    if spans:
        keep_p = keep_n = 1.0
        tok = lax.broadcasted_iota(jnp.int32, (tm, FF_CHUNK), 0) % seq
        first, last = tok == 0, tok == seq - 1
    else:
        keep_p = jnp.where((i * tm) % seq == 0, 0.0, 1.0)
        keep_n = jnp.where(((i + 1) * tm) % seq == 0, 0.0, 1.0)
    yext_s[0:HALO, :] = (_adaln(xp, g2, shift, scale) * keep_p).astype(BF16)
    yext_s[HALO:HALO + tm, :] = _adaln(x, g2, shift, scale).astype(BF16)
    yext_s[HALO + tm:, :] = (_adaln(xn, g2, shift, scale) * keep_n).astype(BF16)

    def up(c):
        c0 = c * FF_CHUNK
        gext_s[c % 2] = _dot(yext_s[...], wup_ref[:, D_FF + c0:D_FF + c0 + FF_CHUNK])
        return _dot(yext_s[HALO:HALO + tm, :], wup_ref[:, c0:c0 + FF_CHUNK])

    nchunk = D_FF // FF_CHUNK
    a_next = up(0)
    for c in range(nchunk):
        c0 = c * FF_CHUNK
        a = a_next
        if c + 1 < nchunk:
            a_next = up(c + 1)
        gs = gext_s.at[c % 2]
        g_prev = gs[HALO - 1:HALO - 1 + tm, :]
        g_next = gs[HALO + 1:HALO + 1 + tm, :]
        if spans:
            g_prev = jnp.where(first, 0.0, g_prev)
            g_next = jnp.where(last, 0.0, g_next)
        g = (cw_ref[0:1, c0:c0 + FF_CHUNK] * g_prev
             + cw_ref[1:2, c0:c0 + FF_CHUNK] * gs[HALO:HALO + tm, :]
             + cw_ref[2:3, c0:c0 + FF_CHUNK] * g_next
             + cb_ref[:, c0:c0 + FF_CHUNK])
        h_s[:, c0:c0 + FF_CHUNK] = (_silu(g) * a).astype(BF16)
    y = x + mod_ref[5:6, :] * _dot(h_s[...], wdn_ref[...])
    if final:
        y = _rms(y, gf_ref[...])
    y_ref[...] = y


def _ffn(x2, o_a, o_b, o_c, mod_all, l, seq, is_ctx, wts, g_final, final):
    m = x2.shape[0]
    tm = TM
    assert tm % seq == 0 or seq % tm == 0
    nb = tm // HALO
    last_blk = m // HALO - 1
    row = lambda i: (i, 0)
    prev = lambda i: (jnp.maximum(i * nb - 1, 0), 0)
    nxt = lambda i: (jnp.minimum((i + 1) * nb, last_blk), 0)
    in_specs, args = [], []
    for a in (x2, o_a, o_b, o_c):
        w = a.shape[1]
        in_specs += [pl.BlockSpec((tm, w), row), pl.BlockSpec((HALO, w), prev), pl.BlockSpec((HALO, w), nxt)]
        args += [a, a, a]
    names = ["g_norm2", "w_out", "w_ff_up", "w_ff_conv", "b_ff_conv", "w_ff_down"]
    in_specs += ([_mod_spec(l, seq, tm, is_ctx)] + [_lspec(wts[n], l) for n in names]
                 + [pl.BlockSpec((1, D_MODEL), lambda i: (0, 0))])
    args += [mod_all] + [wts[n] for n in names] + [g_final]
    return pl.pallas_call(
        functools.partial(_ffn_kernel, seq, tm, final),
        grid=(m // tm,),
        in_specs=in_specs,
        out_specs=pl.BlockSpec((tm, D_MODEL), row),
        out_shape=jax.ShapeDtypeStruct((m, D_MODEL), F32),
        scratch_shapes=[pltpu.VMEM((tm + 2 * HALO, D_MODEL), BF16),
                        pltpu.VMEM((tm + 2 * HALO, D_MODEL), BF16),
                        pltpu.VMEM((2, tm + 2 * HALO, FF_CHUNK), F32),
                        pltpu.VMEM((tm, D_FF), BF16)],
        compiler_params=_cparams(("parallel",)),
        name="ffn_ctx" if is_ctx else "ffn_lat",
    )(*args)


def _rope_tables(t):
    pos = np.arange(t)
    rows = (pos // GRID_W).astype(np.float32)
    cols = (pos % GRID_W).astype(np.float32)

    def group(p, d):
        inv = (np.float32(ROPE_THETA) ** (-np.arange(0, d, 2, dtype=np.float32) / np.float32(d))).astype(np.float32)
        ang = (p[:, None] * inv[None, :]).astype(np.float32)
        cs, sn, z = np.cos(ang), np.sin(ang), np.zeros_like(ang)
        return (np.concatenate([cs, cs], 1), np.concatenate([-sn, z], 1), np.concatenate([z, sn], 1))

    gr, gc = group(rows, MLA_ROPE // 2), group(cols, MLA_ROPE // 2)
    ones = np.ones((t, MLA_NOPE), np.float32)
    zeros = np.zeros((t, MLA_NOPE), np.float32)
    pad1 = np.ones((t, LANES - MLA_NOPE - MLA_ROPE), np.float32)
    pad0 = np.zeros((t, LANES - MLA_NOPE - MLA_ROPE), np.float32)
    mla = (np.concatenate([ones, gr[0], gc[0], pad1], 1),
           np.concatenate([zeros, gr[1], gc[1], pad0], 1),
           np.concatenate([zeros, gr[2], gc[2], pad0], 1))
    gr, gc = group(rows, GQA_DH // 2), group(cols, GQA_DH // 2)
    gqa = tuple(np.concatenate([gr[k], gc[k], gr[k], gc[k]], 1) for k in range(3))
    return tuple(jnp.asarray(a, F32) for a in mla + gqa)


def _prep_weights(p):
    w_in = p["w_in"]
    nl = w_in.shape[0]
    qg = [w_in[:, :, 1968 + GQA_DH * h:1968 + GQA_DH * (h + 1)] for h in (0, 2, 1, 3)]
    gh = GATE_HEADS

    def regroup(g16):
        blocks = []
        for pr in range(ML_HEADS // gh):
            cols = [g16[..., kind * ML_HEADS + gh * pr:kind * ML_HEADS + gh * (pr + 1)] for kind in range(4)]
            blocks += cols + [jnp.zeros(g16.shape[:-1] + (LANES - 4 * gh,), F32)]
        return jnp.concatenate(blocks, axis=-1)

    w_gates = regroup(w_in[:, :, 1952:1968])
    b_gates = regroup(p["b_ml_gates"])
    w_in_p = jnp.concatenate([
        w_in[:, :, 0:384],
        jnp.pad(w_in[:, :, 384:416], ((0, 0), (0, 0), (MLA_NOPE, LANES - MLA_NOPE - MLA_ROPE))),
        w_in[:, :, 416:1952],
        w_gates,
        *qg,
        w_in[:, :, 2224:2480],
    ], axis=2).astype(BF16)
    w_uq = p["w_mla_uq"].reshape(nl, MLA_Q_RANK, MLA_HEADS, MLA_NOPE + MLA_ROPE)
    w_uq = jnp.pad(w_uq, ((0, 0), (0, 0), (0, 0), (0, LANES - MLA_NOPE - MLA_ROPE)))
    w_ukv = p["w_mla_ukv"].reshape(nl, MLA_KV_RANK, MLA_HEADS, MLA_NOPE + MLA_V)
    w_uk = jnp.pad(w_ukv[..., :MLA_NOPE], ((0, 0), (0, 0), (0, 0), (0, LANES - MLA_NOPE)))
    w_ukv_p = jnp.concatenate([w_uk.reshape(nl, MLA_KV_RANK, -1),
                               w_ukv[..., MLA_NOPE:].reshape(nl, MLA_KV_RANK, -1)], axis=2)
    grp = np.arange(2 * LANES) // GQA_DH
    row = lambda a: a[:, None, :]
    return {
        "g_norm1": row(p["g_norm1"]),
        "g_norm2": row(p["g_norm2"]),
        "w_in": w_in_p,
        "g_mla_q": row(p["g_mla_q"]),
        "w_uq": w_uq.reshape(nl, MLA_Q_RANK, -1).astype(BF16),
        "g_mla_kv": row(p["g_mla_kv"]),
        "w_ukv": w_ukv_p.astype(BF16),
        "b_gates": row(b_gates),
        "g_gqa_q": row(jnp.tile(p["g_gqa_q"], (1, GQA_HEADS))),
        "g_gqa_k": row(jnp.tile(p["g_gqa_k"], (1, GQA_KV_HEADS))),
        "gsum": jnp.asarray(grp[:, None] == grp[None, :], BF16),
        "w_ml_conv": p["w_ml_conv"],
        "b_ml_conv": row(p["b_ml_conv"]),
        "w_ml_qt": jnp.swapaxes(p["w_ml_q"], 2, 3).astype(BF16),
        "w_ml_k": p["w_ml_k"].astype(BF16),
        "g_ml_out": row(p["g_ml_out"]),
        "w_out": p["w_out"].astype(BF16),
        "w_ff_up": p["w_ff_up"].astype(BF16),
        "w_ff_conv": p["w_ff_conv"],
        "b_ff_conv": row(p["b_ff_conv"]),
        "w_ff_down": p["w_ff_down"].astype(BF16),
    }


def _layer(x2, mod_all, l, seq, is_ctx, wts, tabs, cache, g_final, final):
    m = x2.shape[0]
    b = m // seq
    r3 = lambda a: a.reshape(b, seq, a.shape[-1])
    outs = _in_proj(x2, mod_all, l, seq, is_ctx, wts, tabs)
    q_m, k_m, v_m, u, v_ml, o_ml, gates, q_g, k_g, v_g = outs[:10]
    segs_a = [(r3(k_m), r3(v_m), None)]
    segs_c = [(r3(k_g), r3(v_g), None)]
    init = None
    if not is_ctx:
        ckv_c, kr_pad, kg_c, vg_c, ct0, n0p, m0p = cache
        kc, vc = _cache_kv(ckv_c, kr_pad, l, wts["w_ukv"])
        segs_a.append((kc, vc, None))
        segs_c.append((kg_c, vg_c, l))
        init = (ct0, n0p, m0p)
    o_a = _attention(True, r3(q_m), segs_a)
    o_c = _attention(False, r3(q_g), segs_c)
    ml = _mlstm(r3(u), r3(v_ml), r3(o_ml), gates, l, wts, init)
    x2 = _ffn(x2, o_a.reshape(m, -1), ml[0].reshape(m, -1), o_c.reshape(m, -1), mod_all, l, seq, is_ctx, wts,
              g_final, final)
    state = None
    if is_ctx:
        ckv_n, kr_raw = outs[10:12]
        sfin = ml[2]
        state = (
            r3(ckv_n),
            r3(kr_raw)[:, :, MLA_NOPE:MLA_NOPE + MLA_ROPE],
            k_g.reshape(b, seq, GQA_KV_HEADS, GQA_DH),
            v_g.reshape(b, seq, GQA_KV_HEADS, GQA_DH),
            ml[1],
            jnp.swapaxes(sfin[:, :, 0:2, :], 1, 2),
            jnp.swapaxes(sfin[:, :, 2:4, 0], 1, 2),
        )
    return x2, state


def kernel(x_prompt, x_sample, cache_mla_ckv, cache_mla_krope, cache_gqa_k, cache_gqa_v, state_mlstm_C, state_mlstm_n, state_mlstm_m, c, c_ctx, w_ada, b_ada, g_norm1, g_norm2, w_in, g_mla_q, w_mla_uq, g_mla_kv, w_mla_ukv, w_ml_conv, b_ml_conv, w_ml_q, w_ml_k, b_ml_gates, g_ml_out, g_gqa_q, g_gqa_k, w_out, w_ff_up, w_ff_conv, b_ff_conv, w_ff_down, g_final):
    params = {
        "g_norm1": g_norm1, "g_norm2": g_norm2, "w_in": w_in, "g_mla_q": g_mla_q, "w_mla_uq": w_mla_uq,
        "g_mla_kv": g_mla_kv, "w_mla_ukv": w_mla_ukv, "w_ml_conv": w_ml_conv, "b_ml_conv": b_ml_conv,
        "w_ml_q": w_ml_q, "w_ml_k": w_ml_k, "b_ml_gates": b_ml_gates, "g_ml_out": g_ml_out,
        "g_gqa_q": g_gqa_q, "g_gqa_k": g_gqa_k, "w_out": w_out, "w_ff_up": w_ff_up, "w_ff_conv": w_ff_conv,
        "b_ff_conv": b_ff_conv, "w_ff_down": w_ff_down,
    }
    depth = w_in.shape[0]
    bp, sp, _ = x_prompt.shape
    bs, ss, _ = x_sample.shape
    assert bs + 1 <= 16 and sp % ML_CHUNK == 0 and ss % TM == 0 and (bp * sp) % TM == 0

    cvec = jnp.concatenate([c_ctx[None, :], c, jnp.zeros((16 - 1 - bs, D_MODEL), F32)], axis=0)
    mod_all = _modulation(cvec, w_ada, b_ada).reshape(depth, 16, 6, D_MODEL)
    tabs = _rope_tables(ss)
    gf = g_final[None, :]
    wts = _prep_weights(params)

    tc = cache_mla_ckv.shape[2]
    pad_rows = ((0, 0), (0, 0), (0, 0), (0, 8 - 2), (0, 0))
    cache = (
        cache_mla_ckv,
        jnp.pad(cache_mla_krope, ((0, 0), (0, 0), (0, 0), (MLA_NOPE, LANES - MLA_NOPE - MLA_ROPE))),
        cache_gqa_k.reshape(bs, depth, tc, GQA_KV_HEADS * GQA_DH),
        cache_gqa_v.reshape(bs, depth, tc, GQA_KV_HEADS * GQA_DH),
        state_mlstm_C,
        jnp.pad(jnp.swapaxes(state_mlstm_n, 2, 3), pad_rows),
        jnp.pad(jnp.broadcast_to(jnp.swapaxes(state_mlstm_m, 2, 3)[..., None], (bs, depth, ML_HEADS, 2, LANES)),
                pad_rows),
    )

    xp = x_prompt.reshape(bp * sp, D_MODEL)
    xs = x_sample.reshape(bs * ss, D_MODEL)
    states = []
    for l in range(depth):
        final = l == depth - 1
        xp, st = _layer(xp, mod_all, l, sp, True, wts, None, None, gf, final)
        states.append(st)
        xs, _ = _layer(xs, mod_all, l, ss, False, wts, tabs, cache, gf, final)
    new_state = tuple(jnp.stack([st[k] for st in states], axis=1) for k in range(7))
    return (xp.reshape(bp, sp, D_MODEL), xs.reshape(bs, ss, D_MODEL)) + new_state
```

```python
import functools

import jax
import jax.numpy as jnp
import numpy as np
from jax import lax
from jax.experimental import pallas as pl
from jax.experimental.pallas import tpu as pltpu

F32 = jnp.float32
BF16 = jnp.bfloat16

D_MODEL = 1024
GRID_W = 64
ROPE_THETA = 10000.0
EPS = 1e-6
MLA_HEADS = 4
MLA_NOPE = 64
MLA_ROPE = 32
MLA_V = 64
MLA_Q_RANK = 256
MLA_KV_RANK = 128
ML_HEADS = 4
ML_DH = 128
ML_WIDTH = ML_HEADS * ML_DH
ML_CHUNK = 128
GQA_HEADS = 4
GQA_KV_HEADS = 2
GQA_DH = 64
D_FF = 2816
MLA_SCALE = (MLA_NOPE + MLA_ROPE) ** -0.5
GQA_SCALE = GQA_DH ** -0.5
ML_K_SCALE = ML_DH ** -0.5
LOG2E = 1.4426950408889634

LANES = 128
HALO = 16
VMEM_LIMIT = 52 * 1024 * 1024

C_CQ, C_CKV, C_KR, C_U, C_V, C_O, C_G, C_QG, C_KG, C_VG, IN_COLS_P = (
    0, 256, 384, 512, 1024, 1536, 2048, 2304, 2560, 2688, 2816)
GATE_COLS = 2 * LANES
FF_CHUNK = 256
TM = 512
TQ = 512


def _cparams(sem):
    return pltpu.CompilerParams(dimension_semantics=sem, vmem_limit_bytes=VMEM_LIMIT)


def _lspec(arr, l):
    nd = arr.ndim - 1
    return pl.BlockSpec((None,) + arr.shape[1:], lambda *_: (l,) + (0,) * nd, pipeline_mode=pl.Buffered(1))


def _dot(a, b):
    return jnp.dot(a, b, preferred_element_type=F32)


def _dot_nt(a, b):
    return lax.dot_general(a, b, (((1,), (1,)), ((), ())), preferred_element_type=F32)


def _dot_tn(a, b):
    return lax.dot_general(a, b, (((0,), (0,)), ((), ())), preferred_element_type=F32)


def _rms(x, g):
    return (x * lax.rsqrt(jnp.mean(x * x, axis=-1, keepdims=True) + EPS)) * g


def _silu(x):
    return x * jax.nn.sigmoid(x)


def _adaln(x, g, shift, scale):
    return _rms(x, g) * (1.0 + scale) + shift


def _split2(x):
    hi = x.astype(BF16)
    lo = (x - hi.astype(F32)).astype(BF16)
    return hi, lo


def _split3(x):
    h1 = x.astype(BF16)
    r1 = x - h1.astype(F32)
    h2 = r1.astype(BF16)
    h3 = (r1 - h2.astype(F32)).astype(BF16)
    return h1, h2, h3


def _rope(x, cos, sa, sb, half):
    w = x.shape[-1]
    return x * cos + pltpu.roll(x, w - half, 1) * sa + pltpu.roll(x, half, 1) * sb


def _group_rms(x, gsum, g):
    hi, lo = _split2(x * x)
    ss = _dot(hi, gsum) + _dot(lo, gsum)
    return (x * lax.rsqrt(ss * (1.0 / GQA_DH) + EPS)) * g


def _mod_kernel(c_ref, w_ref, b_ref, o_ref):
    a = _silu(c_ref[...]).astype(BF16)
    o_ref[...] = _dot(a, w_ref[...].astype(BF16)) + b_ref[...]


def _modulation(cvec, w_ada, b_ada):
    nl = w_ada.shape[0]
    tn = 1536
    return pl.pallas_call(
        _mod_kernel,
        grid=(nl, 6 * D_MODEL // tn),
        in_specs=[
            pl.BlockSpec((16, D_MODEL), lambda l, j: (0, 0)),
            pl.BlockSpec((None, D_MODEL, tn), lambda l, j: (l, 0, j)),
            pl.BlockSpec((None, 1, tn), lambda l, j: (l, 0, j)),
        ],
        out_specs=pl.BlockSpec((None, 16, tn), lambda l, j: (l, 0, j)),
        out_shape=jax.ShapeDtypeStruct((nl, 16, 6 * D_MODEL), F32),
        compiler_params=_cparams(("arbitrary", "arbitrary")),
        name="modulation",
    )(cvec, w_ada, b_ada.reshape(nl, 1, 6 * D_MODEL))


def _in_proj_kernel(is_ctx, x_ref, mod_ref, g1_ref, win_ref, gq_ref, wuq_ref, gkv_ref, wukv_ref,
                    bg_ref, ggq_ref, ggk_ref, gsum_ref, *rest):
    if is_ctx:
        (qm_ref, km_ref, vm_ref, u_ref, vml_ref, oml_ref, gt_ref, qg_ref, kg_ref, vg_ref,
         ckv_ref, kr_ref) = rest
    else:
        (cm_ref, sam_ref, sbm_ref, cg_ref, sag_ref, sbg_ref,
         qm_ref, km_ref, vm_ref, u_ref, vml_ref, oml_ref, gt_ref, qg_ref, kg_ref, vg_ref) = rest

    y = _adaln(x_ref[...], g1_ref[...], mod_ref[0:1, :], mod_ref[1:2, :]).astype(BF16)

    def proj(c0, width):
        return _dot(y, win_ref[:, c0:c0 + width])

    z_cq = proj(C_CQ, MLA_Q_RANK)
    ckv_kr = proj(C_CKV, MLA_KV_RANK + LANES)

    cqn = _rms(z_cq, gq_ref[...]).astype(BF16)
    z_qg = proj(C_QG, GQA_HEADS * GQA_DH)
    qz = _dot(cqn, wuq_ref[...])

    ckvn = _rms(ckv_kr[:, 0:MLA_KV_RANK], gkv_ref[...])
    kr = ckv_kr[:, MLA_KV_RANK:]
    if is_ctx:
        ckv_ref[...] = ckvn
        kr_ref[...] = kr
    else:
        kr = _rope(kr, cm_ref[...], sam_ref[...], sbm_ref[...], 8)
    kv_g = proj(C_KG, 2 * LANES)
    kvz = _dot(ckvn.astype(BF16), wukv_ref[...])

    z_u = proj(C_U, ML_WIDTH)
    for h in range(MLA_HEADS):
        blk = qz[:, LANES * h:LANES * (h + 1)]
        if not is_ctx:
            blk = _rope(blk, cm_ref[...], sam_ref[...], sbm_ref[...], 8)
        qm_ref[:, LANES * h:LANES * (h + 1)] = (blk * (MLA_SCALE * LOG2E)).astype(qm_ref.dtype)

    z_v = proj(C_V, ML_WIDTH)
    for h in range(MLA_HEADS):
        km_ref[:, LANES * h:LANES * (h + 1)] = (kvz[:, LANES * h:LANES * (h + 1)] + kr).astype(km_ref.dtype)
    vm_ref[...] = kvz[:, MLA_HEADS * LANES:].astype(vm_ref.dtype)
    u_ref[...] = z_u.astype(u_ref.dtype)

    z_o = proj(C_O, ML_WIDTH)
    qg = _group_rms(z_qg, gsum_ref[...], ggq_ref[...])
    kg = _group_rms(kv_g[:, 0:LANES], gsum_ref[0:LANES, 0:LANES], ggk_ref[...])
    vml_ref[...] = z_v.astype(vml_ref.dtype)
    gates = proj(C_G, GATE_COLS) + bg_ref[...]
    if not is_ctx:
        qg = jnp.concatenate(
            [_rope(qg[:, 0:LANES], cg_ref[...], sag_ref[...], sbg_ref[...], 16),
             _rope(qg[:, LANES:], cg_ref[...], sag_ref[...], sbg_ref[...], 16)], axis=1)
        kg = _rope(kg, cg_ref[...], sag_ref[...], sbg_ref[...], 16)
    qg_ref[...] = (qg * (GQA_SCALE * LOG2E)).astype(qg_ref.dtype)
    kg_ref[...] = kg.astype(kg_ref.dtype)
    vg_ref[...] = kv_g[:, LANES:].astype(vg_ref.dtype)

    oml_ref[...] = z_o.astype(oml_ref.dtype)
    for pr in range(GATE_COLS // LANES):
        gt_ref[pr] = gates[:, LANES * pr:LANES * (pr + 1)].T[0:8, :]


def _mod_spec(l, seq, tm, is_ctx):
    if is_ctx:
        return pl.BlockSpec((None, None, 6, D_MODEL), lambda i: (l, 0, 0, 0))
    return pl.BlockSpec((None, None, 6, D_MODEL), lambda i: (l, 1 + (i * tm) // seq, 0, 0))


def _in_proj(x2, mod_all, l, seq, is_ctx, wts, tabs):
    m = x2.shape[0]
    tm = TM
    nt = m // tm
    row = lambda i: (i, 0)
    names = ["g_norm1", "w_in", "g_mla_q", "w_uq", "g_mla_kv", "w_ukv", "b_gates", "g_gqa_q", "g_gqa_k"]
    in_specs = ([pl.BlockSpec((tm, D_MODEL), row), _mod_spec(l, seq, tm, is_ctx)]
                + [_lspec(wts[n], l) for n in names]
                + [pl.BlockSpec((2 * LANES, 2 * LANES), lambda i: (0, 0))])
    args = [x2, mod_all] + [wts[n] for n in names] + [wts["gsum"]]
    if not is_ctx:
        tpos = seq // tm
        in_specs += [pl.BlockSpec((tm, LANES), lambda i: (i % tpos, 0))] * 6
        args += list(tabs)
    act = F32 if is_ctx else BF16
    outs = [
        (MLA_HEADS * LANES, BF16),
        (MLA_HEADS * LANES, BF16),
        (MLA_HEADS * MLA_V, BF16),
        (ML_WIDTH, F32),
        (ML_WIDTH, BF16),
        (ML_WIDTH, F32),
        (None, F32),
        (GQA_HEADS * GQA_DH, BF16),
        (LANES, act),
        (LANES, act),
    ]
    if is_ctx:
        outs += [(MLA_KV_RANK, F32), (LANES, F32)]
    return pl.pallas_call(
        functools.partial(_in_proj_kernel, is_ctx),
        grid=(nt,),
        in_specs=in_specs,
        out_specs=[pl.BlockSpec((GATE_COLS // LANES, 8, tm), lambda i: (0, 0, i)) if w is None
                   else pl.BlockSpec((tm, w), row) for w, _ in outs],
        out_shape=[jax.ShapeDtypeStruct((GATE_COLS // LANES, 8, m) if w is None else (m, w), dt)
                   for w, dt in outs],
        compiler_params=_cparams(("parallel",)),
        name="in_proj_ctx" if is_ctx else "in_proj_lat",
    )(*args)


def _cache_kv_kernel(ckv_ref, kr_ref, wukv_ref, k_ref, v_ref):
    for i in range(ckv_ref.shape[0]):
        kvz = _dot(ckv_ref[i].astype(BF16), wukv_ref[...])
        kr = kr_ref[i]
        for h in range(MLA_HEADS):
            k_ref[i, :, LANES * h:LANES * (h + 1)] = (kvz[:, LANES * h:LANES * (h + 1)] + kr).astype(k_ref.dtype)
        v_ref[i] = kvz[:, MLA_HEADS * LANES:].astype(v_ref.dtype)


def _cache_kv(ckv, kr_pad, l, w_ukv):
    b, _, tc, _ = ckv.shape
    cache_blk = pl.BlockSpec((b, None, tc, LANES), lambda i: (0, l, 0, 0))
    return pl.pallas_call(
        _cache_kv_kernel,
        grid=(1,),
        in_specs=[cache_blk, cache_blk, _lspec(w_ukv, l)],
        out_specs=[pl.BlockSpec((b, tc, MLA_HEADS * LANES), lambda i: (0, 0, 0)),
                   pl.BlockSpec((b, tc, MLA_HEADS * MLA_V), lambda i: (0, 0, 0))],
        out_shape=[jax.ShapeDtypeStruct((b, tc, MLA_HEADS * LANES), BF16),
                   jax.ShapeDtypeStruct((b, tc, MLA_HEADS * MLA_V), BF16)],
        compiler_params=_cparams(("arbitrary",)),
        name="cache_kv",
    )(ckv, kr_pad, w_ukv)


def _attn_kernel(nseg, qa_ref, qc_ref, *refs):
    kva, kvc = refs[:2 * nseg], refs[2 * nseg:4 * nseg]
    oa_ref, oc_ref = refs[4 * nseg:]
    nb, tq = qa_ref.shape[0], qa_ref.shape[1]
    lo = lax.broadcasted_iota(jnp.int32, (tq, LANES), 1) < (LANES // 2)

    def qk(unit):
        bi, is_mla, h = unit
        kv = kva if is_mla else kvc
        if is_mla:
            qh = qa_ref[bi, :, LANES * h:LANES * (h + 1)]
            ksl = slice(LANES * h, LANES * (h + 1))
        else:
            blk = qc_ref[bi, :, LANES * (h % 2):LANES * (h % 2 + 1)].astype(F32)
            qh = jnp.where(lo if h < 2 else jnp.logical_not(lo), blk, 0.0).astype(BF16)
            ksl = slice(0, LANES)
        return [_dot_nt(qh, kv[2 * j][bi, :, ksl].astype(BF16)) for j in range(nseg)]

    def softmax(ss):
        mx = jnp.max(ss[0], axis=-1, keepdims=True)
        for s in ss[1:]:
            mx = jnp.maximum(mx, jnp.max(s, axis=-1, keepdims=True))
        es = [jnp.exp2(s - mx) for s in ss]
        den = es[0].sum(axis=-1, keepdims=True)
        for e in es[1:]:
            den = den + e.sum(axis=-1, keepdims=True)
        return [e.astype(BF16) for e in es], den

    def pv(unit, es, den):
        bi, is_mla, h = unit
        kv = kva if is_mla else kvc
        vsl = slice(LANES * (h // 2), LANES * (h // 2 + 1)) if is_mla else slice(0, LANES)
        acc = None
        for j, e in enumerate(es):
            a = _dot(e, kv[2 * j + 1][bi, :, vsl].astype(BF16))
            acc = a if acc is None else acc + a
        return acc / den

    units = [(bi, is_mla, h) for bi in range(nb) for is_mla in (True, False) for h in range(4)]
    out = {}
    ahead = qk(units[0])
    for n, unit in enumerate(units):
        ss = ahead
        if n + 1 < len(units):
            ahead = qk(units[n + 1])
        out[unit] = pv(unit, *softmax(ss))
    half = LANES // 2
    for bi in range(nb):
        a = [out[(bi, True, h)] for h in range(4)]
        c = [out[(bi, False, h)] for h in range(4)]
        oa_ref[bi, :, 0:LANES] = jnp.where(lo, a[0], a[1]).astype(oa_ref.dtype)
        oa_ref[bi, :, LANES:] = jnp.where(lo, a[2], a[3]).astype(oa_ref.dtype)
        oc_ref[bi, :, 0:LANES] = jnp.where(lo, c[0], pltpu.roll(c[1], half, 1)).astype(oc_ref.dtype)
        oc_ref[bi, :, LANES:] = jnp.where(lo, pltpu.roll(c[2], half, 1), c[3]).astype(oc_ref.dtype)


def _attention(q_a, q_c, segs_a, segs_c):
    b, tq_all, _ = q_a.shape
    tq = min(TQ, tq_all)
    nb = max(1, min(b, 2 * TQ // tq_all))
    in_specs = [pl.BlockSpec((nb, tq, q_a.shape[2]), lambda i, j: (i, j, 0)),
                pl.BlockSpec((nb, tq, q_c.shape[2]), lambda i, j: (i, j, 0))]
    args = [q_a, q_c]
    for segs in (segs_a, segs_c):
        for k, v, layer in segs:
            for a in (k, v):
                if layer is None:
                    in_specs.append(pl.BlockSpec((nb,) + a.shape[1:], lambda i, j: (i, 0, 0)))
                else:
                    in_specs.append(pl.BlockSpec((nb, None) + a.shape[2:],
                                                 lambda i, j, layer=layer: (i, layer, 0, 0)))
                args.append(a)
    out_spec = pl.BlockSpec((nb, tq, 2 * LANES), lambda i, j: (i, j, 0))
    return pl.pallas_call(
        functools.partial(_attn_kernel, len(segs_a)),
        grid=(b // nb, tq_all // tq),
        in_specs=in_specs,
        out_specs=[out_spec, out_spec],
        out_shape=[jax.ShapeDtypeStruct((b, tq_all, 2 * LANES), BF16)] * 2,
        compiler_params=_cparams(("parallel", "parallel")),
        name="attn_lat" if len(segs_a) > 1 else "attn_ctx",
    )(*args)


def _log_sigmoid(x):
    return jnp.minimum(x, 0.0) - jnp.log(1.0 + jnp.exp(-jnp.abs(x)))


GATE_HEADS = 2


def _mlstm_kernel(has_init, nc, hps, u_ref, v_ref, o_ref, gt_ref, cw_ref, cb_ref, wqt_ref, wk_ref, go_ref,
                  *rest):
    if has_init:
        c0_ref, n0_ref, m0_ref, out_ref, qt_s, k_s, vt_s, ht_f, ht_b, acol_s, row_s, st_s = rest
    else:
        out_ref, cfin_ref, sfin_ref, qt_s, k_s, vt_s, ht_f, ht_b, acol_s, row_s, st_s = rest
    t = nc * ML_CHUNK
    sq = (ML_CHUNK, ML_CHUNK)
    w2 = hps * LANES

    rows = lax.broadcasted_iota(jnp.int32, (t, LANES), 0)
    r_i = lax.broadcasted_iota(jnp.int32, sq, 0)
    c_i = lax.broadcasted_iota(jnp.int32, sq, 1)
    lower = c_i <= r_i
    upper = c_i >= r_i
    eye = jnp.where(c_i == r_i, 1.0, 0.0).astype(BF16)
    for j in range(hps):
        hs = slice(LANES * j, LANES * (j + 1))
        u = u_ref[:, hs]
        up = jnp.where(rows == 0, 0.0, pltpu.roll(u, 1, 0))
        un = jnp.where(rows == t - 1, 0.0, pltpu.roll(u, t - 1, 0))
        uc = _silu(cw_ref[0:1, hs] * up + cw_ref[1:2, hs] * u + cw_ref[2:3, hs] * un + cb_ref[:, hs]).astype(BF16)
        vt_s[hs, :] = _dot_nt(eye, v_ref[:, hs]).astype(BF16)
        qt_s[hs, :] = _dot_nt(wqt_ref[j], uc).astype(BF16)
        k_s[:, hs] = _dot(uc, wk_ref[j]) * ML_K_SCALE

    tri_lo = jnp.where(lower, 1.0, 0.0).astype(BF16)
    tri_up = jnp.where(upper, 1.0, 0.0).astype(BF16)
    row8 = lax.broadcasted_iota(jnp.int32, (8, ML_CHUNK), 0)
    is_f = lax.broadcasted_iota(jnp.int32, (8, t), 0) % 4 >= 2
    for pb in range(hps // GATE_HEADS):
        gates = gt_ref[pb]
        lg_all = jnp.where(is_f, _log_sigmoid(gates), gates)
        for c in range(nc):
            sl = slice(c * ML_CHUNK, (c + 1) * ML_CHUNK)
            lg = lg_all[:, sl]
            h1, h2, h3 = _split3(lg)
            cs = jnp.where(row8 < 4, _dot(h1, tri_up) + _dot(h2, tri_up) + _dot(h3, tri_up),
                           _dot(h1, tri_lo) + _dot(h2, tri_lo) + _dot(h3, tri_lo)) * LOG2E
            row_s[pb, :, sl] = cs
            a = lg * LOG2E - pltpu.roll(cs, 6, 0)
            for jj in range(GATE_HEADS):
                j = GATE_HEADS * pb + jj
                hs = slice(LANES * j, LANES * (j + 1))
                st_s[j, sl, :] = _dot(k_s[sl, hs].astype(BF16), qt_s[hs, sl])
                for d in range(2):
                    r = 4 * d + jj
                    acol_s[2 * j + d, sl, :] = jnp.broadcast_to(a[r:r + 1, :], sq).T

    def chunk(j, d, c0, ct, n, m2):
        hs = slice(LANES * j, LANES * (j + 1))
        cs_ = pl.ds(c0, ML_CHUNK)
        last = 0 if d else ML_CHUNK - 1
        qt = qt_s[hs, cs_]
        k = k_s[cs_, hs]
        vt = vt_s[hs, cs_]
        acol = acol_s[2 * j + d, cs_, :]
        r = 2 + 4 * d + j % GATE_HEADS
        bc = row_s[j // GATE_HEADS, r:r + 1, cs_]
        g = bc + m2
        dlog = jnp.where(lower if d else upper, acol + bc, -jnp.inf)
        m_t = jnp.maximum(g, jnp.max(dlog, axis=0, keepdims=True))
        w = jnp.exp2(dlog - m_t)
        inter = jnp.exp2(g - m_t)
        st = st_s[j, cs_, :] * w
        qn = _dot(jnp.broadcast_to(n, (8, LANES)).astype(BF16), qt)[0:1, :]
        den = jnp.sum(st, axis=0, keepdims=True) + inter * qn
        numt = _dot(vt, st.astype(BF16)) + inter * _dot(ct.astype(BF16), qt)
        ht = numt * (1.0 / jnp.maximum(jnp.abs(den), jnp.exp2(-m_t)))
        b_last = jnp.broadcast_to(bc[:, last:last + 1], (1, LANES))
        m_new = jnp.broadcast_to(m_t[:, last:last + 1], (1, LANES))
        kw = k * jnp.exp2(acol + (b_last - m_new))
        decay = jnp.exp2(m2 + (b_last - m_new))
        ct_new = decay * ct + _dot(vt, kw.astype(BF16))
        n_new = decay * n + jnp.sum(kw, axis=0, keepdims=True)
        return ht, ct_new, n_new, m_new

    def step(i, carry):
        cf = i * ML_CHUNK
        cb = (nc - 1 - i) * ML_CHUNK
        if not isinstance(i, int):
            cf = pl.multiple_of(cf, ML_CHUNK)
            cb = pl.multiple_of(cb, ML_CHUNK)
        new = []
        for j in range(hps):
            hs = slice(LANES * j, LANES * (j + 1))
            for d, (c0, ht_s) in enumerate(((cf, ht_f), (cb, ht_b))):
                ht, ct, n, m2 = chunk(j, d, c0, *carry[2 * j + d])
                ht_s[hs, pl.ds(c0, ML_CHUNK)] = ht
                new.append((ct, n, m2))
        return tuple(new)

    if has_init:
        init = tuple((c0_ref[d, j].T, n0_ref[j, d:d + 1, :], m0_ref[j, d:d + 1, :] * LOG2E)
                     for j in range(hps) for d in range(2))
    else:
        init = tuple((jnp.zeros(sq, F32), jnp.zeros((1, LANES), F32), jnp.zeros((1, LANES), F32))
                     for j in range(hps) for d in range(2))
    if nc <= 2:
        carry = init
        for i in range(nc):
            carry = step(i, carry)
    else:
        carry = lax.fori_loop(0, nc, step, init, unroll=4)

    for j in range(hps):
        hs = slice(LANES * j, LANES * (j + 1))
        hsum = (ht_f[hs, :] + ht_b[hs, :]).T
        hn = _rms(hsum, go_ref[:, hs])
        out_ref[:, hs] = (hn * jax.nn.sigmoid(o_ref[:, hs])).astype(out_ref.dtype)

    if not has_init:
        for j in range(hps):
            for d in range(2):
                ct, n, m2 = carry[2 * j + d]
                cfin_ref[d, j] = ct.T
            nf, nb = carry[2 * j][1], carry[2 * j + 1][1]
            mf, mb = carry[2 * j][2], carry[2 * j + 1][2]
            sfin_ref[j] = jnp.concatenate(
                [nf, nb, mf * (1.0 / LOG2E), mb * (1.0 / LOG2E), jnp.zeros((4, LANES), F32)], axis=0)


def _mlstm(u, v_ml, o_ml, gates, l, wts, init):
    b, t, _ = u.shape
    nc = t // ML_CHUNK
    has_init = init is not None
    hps = ML_HEADS if t <= 4 * ML_CHUNK else GATE_HEADS
    w2 = hps * LANES
    pair_blk = lambda i, p: (i, 0, p)
    in_specs = [
        pl.BlockSpec((None, t, w2), pair_blk),
        pl.BlockSpec((None, t, w2), pair_blk),
        pl.BlockSpec((None, t, w2), pair_blk),
        pl.BlockSpec((hps // GATE_HEADS, 8, t), lambda i, p: (p, 0, i)),
        pl.BlockSpec((None, 3, w2), lambda i, p: (l, 0, p)),
        pl.BlockSpec((None, 1, w2), lambda i, p: (l, 0, p)),
        pl.BlockSpec((None, hps, ML_DH, ML_DH), lambda i, p: (l, p, 0, 0)),
        pl.BlockSpec((None, hps, ML_DH, ML_DH), lambda i, p: (l, p, 0, 0)),
        pl.BlockSpec((None, 1, w2), lambda i, p: (l, 0, p)),
    ]
    args = [u, v_ml, o_ml, gates, wts["w_ml_conv"], wts["b_ml_conv"], wts["w_ml_qt"], wts["w_ml_k"],
            wts["g_ml_out"]]
    out_specs = [pl.BlockSpec((None, t, w2), pair_blk)]
    out_shape = [jax.ShapeDtypeStruct((b, t, ML_WIDTH), BF16)]
    state_c = pl.BlockSpec((None, 2, hps, ML_DH, ML_DH), lambda i, p: (i, 0, p, 0, 0))
    state_s = pl.BlockSpec((None, hps, 8, LANES), lambda i, p: (i, p, 0, 0))
    if has_init:
        in_specs += [pl.BlockSpec((None, None, 2, hps, ML_DH, ML_DH), lambda i, p: (i, l, 0, p, 0, 0)),
                     pl.BlockSpec((None, None, hps, 8, LANES), lambda i, p: (i, l, p, 0, 0)),
                     pl.BlockSpec((None, None, hps, 8, LANES), lambda i, p: (i, l, p, 0, 0))]
        args += list(init)
    else:
        out_specs += [state_c, state_s]
        out_shape += [
            jax.ShapeDtypeStruct((b, 2, ML_HEADS, ML_DH, ML_DH), F32),
            jax.ShapeDtypeStruct((b, ML_HEADS, 8, LANES), F32),
        ]
    return pl.pallas_call(
        functools.partial(_mlstm_kernel, has_init, nc, hps),
        grid=(b, ML_HEADS // hps),
        in_specs=in_specs,
        out_specs=out_specs,
        out_shape=out_shape,
        scratch_shapes=[
            pltpu.VMEM((w2, t), BF16),
            pltpu.VMEM((t, w2), F32),
            pltpu.VMEM((w2, t), BF16),
            pltpu.VMEM((w2, t), F32),
            pltpu.VMEM((w2, t), F32),
            pltpu.VMEM((2 * hps, t, LANES), F32),
            pltpu.VMEM((hps // GATE_HEADS, 8, t), F32),
            pltpu.VMEM((hps, t, LANES), F32),
        ],
        compiler_params=_cparams(("parallel", "parallel")),
        name="mlstm_lat" if has_init else "mlstm_ctx",
    )(*args)


def _ffn_kernel(seq, tm, final, *refs):
    x_refs, oa_refs, ob_refs, oc_refs = refs[0:3], refs[3:6], refs[6:9], refs[9:12]
    (mod_ref, g2_ref, wout_ref, wup_ref, cw_ref, cb_ref, wdn_ref, gf_ref, y_ref,
     oext_s, yext_s, gext_s, h_s) = refs[12:]
    i = pl.program_id(0)
    a_w = MLA_HEADS * MLA_V
    ext = ((0, HALO), (HALO, tm), (HALO + tm, HALO))
    for (r0, n), k in zip(ext, (1, 0, 2)):
        oext_s[r0:r0 + n, 0:a_w] = oa_refs[k][...]
        oext_s[r0:r0 + n, a_w:a_w + ML_WIDTH] = ob_refs[k][...]
        oext_s[r0:r0 + n, a_w + ML_WIDTH:] = oc_refs[k][...]
    mix = _dot(oext_s[...], wout_ref[...])
    gate1 = mod_ref[2:3, :]
    x = x_refs[0][...] + gate1 * mix[HALO:HALO + tm, :]
    xp = x_refs[1][...] + gate1 * mix[0:HALO, :]
    xn = x_refs[2][...] + gate1 * mix[HALO + tm:, :]

    g2 = g2_ref[...]
    shift = mod_ref[3:4, :]
    scale = mod_ref[4:5, :]
    spans = tm > seq
    if spans:
        keep_p = keep_n = 1.0
        tok = lax.broadcasted_iota(jnp.int32, (tm, FF_CHUNK), 0) % seq
        first, last = tok == 0, tok == seq - 1
    else:
        keep_p = jnp.where((i * tm) % seq == 0, 0.0, 1.0)
        keep_n = jnp.where(((i + 1) * tm) % seq == 0, 0.0, 1.0)
    yext_s[0:HALO, :] = (_adaln(xp, g2, shift, scale) * keep_p).astype(BF16)
    yext_s[HALO:HALO + tm, :] = _adaln(x, g2, shift, scale).astype(BF16)
    yext_s[HALO + tm:, :] = (_adaln(xn, g2, shift, scale) * keep_n).astype(BF16)

    def up(c):
        c0 = c * FF_CHUNK
        gext_s[c % 2] = _dot(yext_s[...], wup_ref[:, D_FF + c0:D_FF + c0 + FF_CHUNK])
        return _dot(yext_s[HALO:HALO + tm, :], wup_ref[:, c0:c0 + FF_CHUNK])

    nchunk = D_FF // FF_CHUNK
    a_next = up(0)
    for c in range(nchunk):
        c0 = c * FF_CHUNK
        a = a_next
        if c + 1 < nchunk:
            a_next = up(c + 1)
        gs = gext_s.at[c % 2]
        g_prev = gs[HALO - 1:HALO - 1 + tm, :]
        g_next = gs[HALO + 1:HALO + 1 + tm, :]
        if spans:
            g_prev = jnp.where(first, 0.0, g_prev)
            g_next = jnp.where(last, 0.0, g_next)
        g = (cw_ref[0:1, c0:c0 + FF_CHUNK] * g_prev
             + cw_ref[1:2, c0:c0 + FF_CHUNK] * gs[HALO:HALO + tm, :]
             + cw_ref[2:3, c0:c0 + FF_CHUNK] * g_next
             + cb_ref[:, c0:c0 + FF_CHUNK])
        h_s[:, c0:c0 + FF_CHUNK] = (_silu(g) * a).astype(BF16)
    y = x + mod_ref[5:6, :] * _dot(h_s[...], wdn_ref[...])
    if final:
        y = _rms(y, gf_ref[...])
    y_ref[...] = y


def _ffn(x2, o_a, o_b, o_c, mod_all, l, seq, is_ctx, wts, g_final, final):
    m = x2.shape[0]
    tm = TM
    assert tm % seq == 0 or seq % tm == 0
    nb = tm // HALO
    last_blk = m // HALO - 1
    row = lambda i: (i, 0)
    prev = lambda i: (jnp.maximum(i * nb - 1, 0), 0)
    nxt = lambda i: (jnp.minimum((i + 1) * nb, last_blk), 0)
    in_specs, args = [], []
    for a in (x2, o_a, o_b, o_c):
        w = a.shape[1]
        in_specs += [pl.BlockSpec((tm, w), row), pl.BlockSpec((HALO, w), prev), pl.BlockSpec((HALO, w), nxt)]
        args += [a, a, a]
    names = ["g_norm2", "w_out", "w_ff_up", "w_ff_conv", "b_ff_conv", "w_ff_down"]
    in_specs += ([_mod_spec(l, seq, tm, is_ctx)] + [_lspec(wts[n], l) for n in names]
                 + [pl.BlockSpec((1, D_MODEL), lambda i: (0, 0))])
    args += [mod_all] + [wts[n] for n in names] + [g_final]
    return pl.pallas_call(
        functools.partial(_ffn_kernel, seq, tm, final),
        grid=(m // tm,),
        in_specs=in_specs,
        out_specs=pl.BlockSpec((tm, D_MODEL), row),
        out_shape=jax.ShapeDtypeStruct((m, D_MODEL), F32),
        scratch_shapes=[pltpu.VMEM((tm + 2 * HALO, D_MODEL), BF16),
                        pltpu.VMEM((tm + 2 * HALO, D_MODEL), BF16),
                        pltpu.VMEM((2, tm + 2 * HALO, FF_CHUNK), F32),
                        pltpu.VMEM((tm, D_FF), BF16)],
        compiler_params=_cparams(("parallel",)),
        name="ffn_ctx" if is_ctx else "ffn_lat",
    )(*args)


def _rope_tables(t):
    pos = np.arange(t)
    rows = (pos // GRID_W).astype(np.float32)
    cols = (pos % GRID_W).astype(np.float32)

    def group(p, d):
        inv = (np.float32(ROPE_THETA) ** (-np.arange(0, d, 2, dtype=np.float32) / np.float32(d))).astype(np.float32)
        ang = (p[:, None] * inv[None, :]).astype(np.float32)
        cs, sn, z = np.cos(ang), np.sin(ang), np.zeros_like(ang)
        return (np.concatenate([cs, cs], 1), np.concatenate([-sn, z], 1), np.concatenate([z, sn], 1))

    gr, gc = group(rows, MLA_ROPE // 2), group(cols, MLA_ROPE // 2)
    ones = np.ones((t, MLA_NOPE), np.float32)
    zeros = np.zeros((t, MLA_NOPE), np.float32)
    pad1 = np.ones((t, LANES - MLA_NOPE - MLA_ROPE), np.float32)
    pad0 = np.zeros((t, LANES - MLA_NOPE - MLA_ROPE), np.float32)
    mla = (np.concatenate([ones, gr[0], gc[0], pad1], 1),
           np.concatenate([zeros, gr[1], gc[1], pad0], 1),
           np.concatenate([zeros, gr[2], gc[2], pad0], 1))
    gr, gc = group(rows, GQA_DH // 2), group(cols, GQA_DH // 2)
    gqa = tuple(np.concatenate([gr[k], gc[k], gr[k], gc[k]], 1) for k in range(3))
    return tuple(jnp.asarray(a, F32) for a in mla + gqa)


def _prep_weights(p):
    w_in = p["w_in"]
    nl = w_in.shape[0]
    qg = [w_in[:, :, 1968 + GQA_DH * h:1968 + GQA_DH * (h + 1)] for h in (0, 2, 1, 3)]
    gh = GATE_HEADS

    def regroup(g16):
        blocks = []
        for pr in range(ML_HEADS // gh):
            cols = [g16[..., kind * ML_HEADS + gh * pr:kind * ML_HEADS + gh * (pr + 1)] for kind in range(4)]
            blocks += cols + [jnp.zeros(g16.shape[:-1] + (LANES - 4 * gh,), F32)]
        return jnp.concatenate(blocks, axis=-1)

    w_gates = regroup(w_in[:, :, 1952:1968])
    b_gates = regroup(p["b_ml_gates"])
    w_in_p = jnp.concatenate([
        w_in[:, :, 0:384],
        jnp.pad(w_in[:, :, 384:416], ((0, 0), (0, 0), (MLA_NOPE, LANES - MLA_NOPE - MLA_ROPE))),
        w_in[:, :, 416:1952],
        w_gates,
        *qg,
        w_in[:, :, 2224:2480],
    ], axis=2).astype(BF16)
    w_uq = p["w_mla_uq"].reshape(nl, MLA_Q_RANK, MLA_HEADS, MLA_NOPE + MLA_ROPE)
    w_uq = jnp.pad(w_uq, ((0, 0), (0, 0), (0, 0), (0, LANES - MLA_NOPE - MLA_ROPE)))
    w_ukv = p["w_mla_ukv"].reshape(nl, MLA_KV_RANK, MLA_HEADS, MLA_NOPE + MLA_V)
    w_uk = jnp.pad(w_ukv[..., :MLA_NOPE], ((0, 0), (0, 0), (0, 0), (0, LANES - MLA_NOPE)))
    w_ukv_p = jnp.concatenate([w_uk.reshape(nl, MLA_KV_RANK, -1),
                               w_ukv[..., MLA_NOPE:].reshape(nl, MLA_KV_RANK, -1)], axis=2)
    grp = np.arange(2 * LANES) // GQA_DH
    row = lambda a: a[:, None, :]
    return {
        "g_norm1": row(p["g_norm1"]),
        "g_norm2": row(p["g_norm2"]),
        "w_in": w_in_p,
        "g_mla_q": row(p["g_mla_q"]),
        "w_uq": w_uq.reshape(nl, MLA_Q_RANK, -1).astype(BF16),
        "g_mla_kv": row(p["g_mla_kv"]),
        "w_ukv": w_ukv_p.astype(BF16),
        "b_gates": row(b_gates),
        "g_gqa_q": row(jnp.tile(p["g_gqa_q"], (1, GQA_HEADS))),
        "g_gqa_k": row(jnp.tile(p["g_gqa_k"], (1, GQA_KV_HEADS))),
        "gsum": jnp.asarray(grp[:, None] == grp[None, :], BF16),
        "w_ml_conv": p["w_ml_conv"],
        "b_ml_conv": row(p["b_ml_conv"]),
        "w_ml_qt": jnp.swapaxes(p["w_ml_q"], 2, 3).astype(BF16),
        "w_ml_k": p["w_ml_k"].astype(BF16),
        "g_ml_out": row(p["g_ml_out"]),
        "w_out": p["w_out"].astype(BF16),
        "w_ff_up": p["w_ff_up"].astype(BF16),
        "w_ff_conv": p["w_ff_conv"],
        "b_ff_conv": row(p["b_ff_conv"]),
        "w_ff_down": p["w_ff_down"].astype(BF16),
    }


def _layer(x2, mod_all, l, seq, is_ctx, wts, tabs, cache, g_final, final):
    m = x2.shape[0]
    b = m // seq
    r3 = lambda a: a.reshape(b, seq, a.shape[-1])
    outs = _in_proj(x2, mod_all, l, seq, is_ctx, wts, tabs)
    q_m, k_m, v_m, u, v_ml, o_ml, gates, q_g, k_g, v_g = outs[:10]
    segs_a = [(r3(k_m), r3(v_m), None)]
    segs_c = [(r3(k_g), r3(v_g), None)]
    init = None
    if not is_ctx:
        ckv_c, kr_pad, kg_c, vg_c, ct0, n0p, m0p = cache
        kc, vc = _cache_kv(ckv_c, kr_pad, l, wts["w_ukv"])
        segs_a.append((kc, vc, None))
        segs_c.append((kg_c, vg_c, l))
        init = (ct0, n0p, m0p)
    o_a, o_c = _attention(r3(q_m), r3(q_g), segs_a, segs_c)
    ml = _mlstm(r3(u), r3(v_ml), r3(o_ml), gates, l, wts, init)
    x2 = _ffn(x2, o_a.reshape(m, -1), ml[0].reshape(m, -1), o_c.reshape(m, -1), mod_all, l, seq, is_ctx, wts,
              g_final, final)
    state = None
    if is_ctx:
        ckv_n, kr_raw = outs[10:12]
        sfin = ml[2]
        state = (
            r3(ckv_n),
            r3(kr_raw)[:, :, MLA_NOPE:MLA_NOPE + MLA_ROPE],
            k_g.reshape(b, seq, GQA_KV_HEADS, GQA_DH),
            v_g.reshape(b, seq, GQA_KV_HEADS, GQA_DH),
            ml[1],
            jnp.swapaxes(sfin[:, :, 0:2, :], 1, 2),
            jnp.swapaxes(sfin[:, :, 2:4, 0], 1, 2),
        )
    return x2, state


def kernel(x_prompt, x_sample, cache_mla_ckv, cache_mla_krope, cache_gqa_k, cache_gqa_v, state_mlstm_C, state_mlstm_n, state_mlstm_m, c, c_ctx, w_ada, b_ada, g_norm1, g_norm2, w_in, g_mla_q, w_mla_uq, g_mla_kv, w_mla_ukv, w_ml_conv, b_ml_conv, w_ml_q, w_ml_k, b_ml_gates, g_ml_out, g_gqa_q, g_gqa_k, w_out, w_ff_up, w_ff_conv, b_ff_conv, w_ff_down, g_final):
    params = {
        "g_norm1": g_norm1, "g_norm2": g_norm2, "w_in": w_in, "g_mla_q": g_mla_q, "w_mla_uq": w_mla_uq,
        "g_mla_kv": g_mla_kv, "w_mla_ukv": w_mla_ukv, "w_ml_conv": w_ml_conv, "b_ml_conv": b_ml_conv,
        "w_ml_q": w_ml_q, "w_ml_k": w_ml_k, "b_ml_gates": b_ml_gates, "g_ml_out": g_ml_out,
        "g_gqa_q": g_gqa_q, "g_gqa_k": g_gqa_k, "w_out": w_out, "w_ff_up": w_ff_up, "w_ff_conv": w_ff_conv,
        "b_ff_conv": b_ff_conv, "w_ff_down": w_ff_down,
    }
    depth = w_in.shape[0]
    bp, sp, _ = x_prompt.shape
    bs, ss, _ = x_sample.shape
    assert bs + 1 <= 16 and sp % ML_CHUNK == 0 and ss % TM == 0 and (bp * sp) % TM == 0

    cvec = jnp.concatenate([c_ctx[None, :], c, jnp.zeros((16 - 1 - bs, D_MODEL), F32)], axis=0)
    mod_all = _modulation(cvec, w_ada, b_ada).reshape(depth, 16, 6, D_MODEL)
    tabs = _rope_tables(ss)
    gf = g_final[None, :]
    wts = _prep_weights(params)

    tc = cache_mla_ckv.shape[2]
    pad_rows = ((0, 0), (0, 0), (0, 0), (0, 8 - 2), (0, 0))
    cache = (
        cache_mla_ckv,
        jnp.pad(cache_mla_krope, ((0, 0), (0, 0), (0, 0), (MLA_NOPE, LANES - MLA_NOPE - MLA_ROPE))),
        cache_gqa_k.reshape(bs, depth, tc, GQA_KV_HEADS * GQA_DH),
        cache_gqa_v.reshape(bs, depth, tc, GQA_KV_HEADS * GQA_DH),
        state_mlstm_C,
        jnp.pad(jnp.swapaxes(state_mlstm_n, 2, 3), pad_rows),
        jnp.pad(jnp.broadcast_to(jnp.swapaxes(state_mlstm_m, 2, 3)[..., None], (bs, depth, ML_HEADS, 2, LANES)),
                pad_rows),
    )

    xp = x_prompt.reshape(bp * sp, D_MODEL)
    xs = x_sample.reshape(bs * ss, D_MODEL)
    states = []
    for l in range(depth):
        final = l == depth - 1
        xp, st = _layer(xp, mod_all, l, sp, True, wts, None, None, gf, final)
        states.append(st)
        xs, _ = _layer(xs, mod_all, l, ss, False, wts, tabs, cache, gf, final)
    new_state = tuple(jnp.stack([st[k] for st in states], axis=1) for k in range(7))
    return (xp.reshape(bp, sp, D_MODEL), xs.reshape(bs, ss, D_MODEL)) + new_state
```

```python
import functools

import jax
import jax.numpy as jnp
import numpy as np
from jax import lax
from jax.experimental import pallas as pl
from jax.experimental.pallas import tpu as pltpu

F32 = jnp.float32
BF16 = jnp.bfloat16

D_MODEL = 1024
GRID_W = 64
ROPE_THETA = 10000.0
EPS = 1e-6
MLA_HEADS = 4
MLA_NOPE = 64
MLA_ROPE = 32
MLA_V = 64
MLA_Q_RANK = 256
MLA_KV_RANK = 128
ML_HEADS = 4
ML_DH = 128
ML_WIDTH = ML_HEADS * ML_DH
ML_CHUNK = 128
GQA_HEADS = 4
GQA_KV_HEADS = 2
GQA_DH = 64
D_FF = 2816
MLA_SCALE = (MLA_NOPE + MLA_ROPE) ** -0.5
GQA_SCALE = GQA_DH ** -0.5
ML_K_SCALE = ML_DH ** -0.5
LOG2E = 1.4426950408889634

LANES = 128
HALO = 16
VMEM_LIMIT = 52 * 1024 * 1024

C_CQ, C_CKV, C_KR, C_U, C_V, C_O, C_G, C_QG, C_KG, C_VG, IN_COLS_P = (
    0, 256, 384, 512, 1024, 1536, 2048, 2304, 2560, 2688, 2816)
GATE_COLS = 2 * LANES
FF_CHUNK = 256
TM = 512
TQ = 512


def _cparams(sem):
    return pltpu.CompilerParams(dimension_semantics=sem, vmem_limit_bytes=VMEM_LIMIT)


def _lspec(arr, l):
    nd = arr.ndim - 1
    return pl.BlockSpec((None,) + arr.shape[1:], lambda *_: (l,) + (0,) * nd, pipeline_mode=pl.Buffered(1))


def _dot(a, b):
    return jnp.dot(a, b, preferred_element_type=F32)


def _dot_nt(a, b):
    return lax.dot_general(a, b, (((1,), (1,)), ((), ())), preferred_element_type=F32)


def _dot_tn(a, b):
    return lax.dot_general(a, b, (((0,), (0,)), ((), ())), preferred_element_type=F32)


def _rms(x, g):
    return (x * lax.rsqrt(jnp.mean(x * x, axis=-1, keepdims=True) + EPS)) * g


def _silu(x):
    return x * jax.nn.sigmoid(x)


def _adaln(x, g, shift, scale):
    return _rms(x, g) * (1.0 + scale) + shift


def _split2(x):
    hi = x.astype(BF16)
    lo = (x - hi.astype(F32)).astype(BF16)
    return hi, lo


def _split3(x):
    h1 = x.astype(BF16)
    r1 = x - h1.astype(F32)
    h2 = r1.astype(BF16)
    h3 = (r1 - h2.astype(F32)).astype(BF16)
    return h1, h2, h3


def _rope(x, cos, sa, sb, half):
    w = x.shape[-1]
    return x * cos + pltpu.roll(x, w - half, 1) * sa + pltpu.roll(x, half, 1) * sb


def _group_rms(x, gsum, g):
    hi, lo = _split2(x * x)
    ss = _dot(hi, gsum) + _dot(lo, gsum)
    return (x * lax.rsqrt(ss * (1.0 / GQA_DH) + EPS)) * g


def _mod_kernel(c_ref, w_ref, b_ref, o_ref):
    a = _silu(c_ref[...]).astype(BF16)
    o_ref[...] = _dot(a, w_ref[...].astype(BF16)) + b_ref[...]


def _modulation(cvec, w_ada, b_ada):
    nl = w_ada.shape[0]
    tn = 1536
    return pl.pallas_call(
        _mod_kernel,
        grid=(nl, 6 * D_MODEL // tn),
        in_specs=[
            pl.BlockSpec((16, D_MODEL), lambda l, j: (0, 0)),
            pl.BlockSpec((None, D_MODEL, tn), lambda l, j: (l, 0, j)),
            pl.BlockSpec((None, 1, tn), lambda l, j: (l, 0, j)),
        ],
        out_specs=pl.BlockSpec((None, 16, tn), lambda l, j: (l, 0, j)),
        out_shape=jax.ShapeDtypeStruct((nl, 16, 6 * D_MODEL), F32),
        compiler_params=_cparams(("arbitrary", "arbitrary")),
        name="modulation",
    )(cvec, w_ada, b_ada.reshape(nl, 1, 6 * D_MODEL))


def _in_proj_kernel(is_ctx, x_ref, mod_ref, g1_ref, win_ref, gq_ref, wuq_ref, gkv_ref, wukv_ref,
                    bg_ref, ggq_ref, ggk_ref, gsum_ref, *rest):
    if is_ctx:
        (qm_ref, km_ref, vm_ref, u_ref, vml_ref, oml_ref, gt_ref, qg_ref, kg_ref, vg_ref,
         ckv_ref, kr_ref) = rest
    else:
        (cm_ref, sam_ref, sbm_ref, cg_ref, sag_ref, sbg_ref,
         qm_ref, km_ref, vm_ref, u_ref, vml_ref, oml_ref, gt_ref, qg_ref, kg_ref, vg_ref) = rest

    def modulated(s):
        rs = slice(s * TM, (s + 1) * TM)
        return _adaln(x_ref[rs, :], g1_ref[...], mod_ref[0:1, :], mod_ref[1:2, :]).astype(BF16)

    nsub = x_ref.shape[0] // TM
    y_next = modulated(0)
    for s in range(nsub):
        rs = slice(s * TM, (s + 1) * TM)
        y = y_next

        def proj(c0, width, y=y):
            return _dot(y, win_ref[:, c0:c0 + width])

        if not is_ctx:
            rope_m = (cm_ref[rs, :], sam_ref[rs, :], sbm_ref[rs, :], 8)
            rope_g = (cg_ref[rs, :], sag_ref[rs, :], sbg_ref[rs, :], 16)

        z_cq = proj(C_CQ, MLA_Q_RANK)
        ckv_kr = proj(C_CKV, MLA_KV_RANK + LANES)
        if s + 1 < nsub:
            y_next = modulated(s + 1)

        cqn = _rms(z_cq, gq_ref[...]).astype(BF16)
        z_qg = proj(C_QG, GQA_HEADS * GQA_DH)
        qz = _dot(cqn, wuq_ref[...])

        ckvn = _rms(ckv_kr[:, 0:MLA_KV_RANK], gkv_ref[...])
        kr = ckv_kr[:, MLA_KV_RANK:]
        if is_ctx:
            ckv_ref[rs, :] = ckvn
            kr_ref[rs, :] = kr
        else:
            kr = _rope(kr, *rope_m)
        kv_g = proj(C_KG, 2 * LANES)
        kvz = _dot(ckvn.astype(BF16), wukv_ref[...])

        z_u = proj(C_U, ML_WIDTH)
        for h in range(MLA_HEADS):
            blk = qz[:, LANES * h:LANES * (h + 1)]
            if not is_ctx:
                blk = _rope(blk, *rope_m)
            qm_ref[rs, LANES * h:LANES * (h + 1)] = (blk * (MLA_SCALE * LOG2E)).astype(qm_ref.dtype)

        z_v = proj(C_V, ML_WIDTH)
        for h in range(MLA_HEADS):
            km_ref[rs, LANES * h:LANES * (h + 1)] = (kvz[:, LANES * h:LANES * (h + 1)] + kr).astype(km_ref.dtype)
        vm_ref[rs, :] = kvz[:, MLA_HEADS * LANES:].astype(vm_ref.dtype)
        u_ref[rs, :] = z_u.astype(u_ref.dtype)

        z_o = proj(C_O, ML_WIDTH)
        qg = _group_rms(z_qg, gsum_ref[...], ggq_ref[...])
        kg = _group_rms(kv_g[:, 0:LANES], gsum_ref[0:LANES, 0:LANES], ggk_ref[...])
        vml_ref[rs, :] = z_v.astype(vml_ref.dtype)
        gates = proj(C_G, GATE_COLS) + bg_ref[...]
        if not is_ctx:
            qg = jnp.concatenate([_rope(qg[:, 0:LANES], *rope_g), _rope(qg[:, LANES:], *rope_g)], axis=1)
            kg = _rope(kg, *rope_g)
        qg_ref[rs, :] = (qg * (GQA_SCALE * LOG2E)).astype(qg_ref.dtype)
        kg_ref[rs, :] = kg.astype(kg_ref.dtype)
        vg_ref[rs, :] = kv_g[:, LANES:].astype(vg_ref.dtype)

        oml_ref[rs, :] = z_o.astype(oml_ref.dtype)
        for pr in range(GATE_COLS // LANES):
            gt_ref[pr, :, rs] = gates[:, LANES * pr:LANES * (pr + 1)].T[0:8, :]


def _mod_spec(l, seq, tm, is_ctx):
    if is_ctx:
        return pl.BlockSpec((None, None, 6, D_MODEL), lambda i: (l, 0, 0, 0))
    return pl.BlockSpec((None, None, 6, D_MODEL), lambda i: (l, 1 + (i * tm) // seq, 0, 0))


def _in_proj(x2, mod_all, l, seq, is_ctx, wts, tabs):
    m = x2.shape[0]
    tm = 2 * TM
    assert m % tm == 0 and (is_ctx or seq % tm == 0)
    nt = m // tm
    row = lambda i: (i, 0)
    names = ["g_norm1", "w_in", "g_mla_q", "w_uq", "g_mla_kv", "w_ukv", "b_gates", "g_gqa_q", "g_gqa_k"]
    in_specs = ([pl.BlockSpec((tm, D_MODEL), row), _mod_spec(l, seq, tm, is_ctx)]
                + [_lspec(wts[n], l) for n in names]
                + [pl.BlockSpec((2 * LANES, 2 * LANES), lambda i: (0, 0))])
    args = [x2, mod_all] + [wts[n] for n in names] + [wts["gsum"]]
    if not is_ctx:
        tpos = seq // tm
        in_specs += [pl.BlockSpec((tm, LANES), lambda i: (i % tpos, 0))] * 6
        args += list(tabs)
    act = F32 if is_ctx else BF16
    outs = [
        (MLA_HEADS * LANES, BF16),
        (MLA_HEADS * LANES, BF16),
        (MLA_HEADS * MLA_V, BF16),
        (ML_WIDTH, F32),
        (ML_WIDTH, BF16),
        (ML_WIDTH, F32),
        (None, F32),
        (GQA_HEADS * GQA_DH, BF16),
        (LANES, act),
        (LANES, act),
    ]
    if is_ctx:
        outs += [(MLA_KV_RANK, F32), (LANES, F32)]
    return pl.pallas_call(
        functools.partial(_in_proj_kernel, is_ctx),
        grid=(nt,),
        in_specs=in_specs,
        out_specs=[pl.BlockSpec((GATE_COLS // LANES, 8, tm), lambda i: (0, 0, i)) if w is None
                   else pl.BlockSpec((tm, w), row) for w, _ in outs],
        out_shape=[jax.ShapeDtypeStruct((GATE_COLS // LANES, 8, m) if w is None else (m, w), dt)
                   for w, dt in outs],
        compiler_params=_cparams(("parallel",)),
        name="in_proj_ctx" if is_ctx else "in_proj_lat",
    )(*args)


def _cache_kv_kernel(ckv_ref, kr_ref, wukv_ref, k_ref, v_ref):
    for i in range(ckv_ref.shape[0]):
        kvz = _dot(ckv_ref[i].astype(BF16), wukv_ref[...])
        kr = kr_ref[i]
        for h in range(MLA_HEADS):
            k_ref[i, :, LANES * h:LANES * (h + 1)] = (kvz[:, LANES * h:LANES * (h + 1)] + kr).astype(k_ref.dtype)
        v_ref[i] = kvz[:, MLA_HEADS * LANES:].astype(v_ref.dtype)


def _cache_kv(ckv, kr_pad, l, w_ukv):
    b, _, tc, _ = ckv.shape
    cache_blk = pl.BlockSpec((b, None, tc, LANES), lambda i: (0, l, 0, 0))
    return pl.pallas_call(
        _cache_kv_kernel,
        grid=(1,),
        in_specs=[cache_blk, cache_blk, _lspec(w_ukv, l)],
        out_specs=[pl.BlockSpec((b, tc, MLA_HEADS * LANES), lambda i: (0, 0, 0)),
                   pl.BlockSpec((b, tc, MLA_HEADS * MLA_V), lambda i: (0, 0, 0))],
        out_shape=[jax.ShapeDtypeStruct((b, tc, MLA_HEADS * LANES), BF16),
                   jax.ShapeDtypeStruct((b, tc, MLA_HEADS * MLA_V), BF16)],
        compiler_params=_cparams(("arbitrary",)),
        name="cache_kv",
    )(ckv, kr_pad, w_ukv)


def _attn_kernel(nseg, qa_ref, qc_ref, *refs):
    kva, kvc = refs[:2 * nseg], refs[2 * nseg:4 * nseg]
    oa_ref, oc_ref = refs[4 * nseg:]
    nb, tq = qa_ref.shape[0], qa_ref.shape[1]
    lo = lax.broadcasted_iota(jnp.int32, (tq, LANES), 1) < (LANES // 2)

    def qk(unit):
        bi, is_mla, h = unit
        kv = kva if is_mla else kvc
        if is_mla:
            qh = qa_ref[bi, :, LANES * h:LANES * (h + 1)]
            ksl = slice(LANES * h, LANES * (h + 1))
        else:
            blk = qc_ref[bi, :, LANES * (h % 2):LANES * (h % 2 + 1)].astype(F32)
            qh = jnp.where(lo if h < 2 else jnp.logical_not(lo), blk, 0.0).astype(BF16)
            ksl = slice(0, LANES)
        return [_dot_nt(qh, kv[2 * j][bi, :, ksl].astype(BF16)) for j in range(nseg)]

    def softmax(ss):
        mx = jnp.max(ss[0], axis=-1, keepdims=True)
        for s in ss[1:]:
            mx = jnp.maximum(mx, jnp.max(s, axis=-1, keepdims=True))
        es = [jnp.exp2(s - mx) for s in ss]
        den = es[0].sum(axis=-1, keepdims=True)
        for e in es[1:]:
            den = den + e.sum(axis=-1, keepdims=True)
        return [e.astype(BF16) for e in es], den

    def pv(unit, es, den):
        bi, is_mla, h = unit
        kv = kva if is_mla else kvc
        vsl = slice(LANES * (h // 2), LANES * (h // 2 + 1)) if is_mla else slice(0, LANES)
        acc = None
        for j, e in enumerate(es):
            a = _dot(e, kv[2 * j + 1][bi, :, vsl].astype(BF16))
            acc = a if acc is None else acc + a
        return acc / den

    units = [(bi, is_mla, h) for bi in range(nb) for is_mla in (True, False) for h in range(4)]
    out = {}
    ahead = qk(units[0])
    for n, unit in enumerate(units):
        ss = ahead
        if n + 1 < len(units):
            ahead = qk(units[n + 1])
        out[unit] = pv(unit, *softmax(ss))
    half = LANES // 2
    for bi in range(nb):
        a = [out[(bi, True, h)] for h in range(4)]
        c = [out[(bi, False, h)] for h in range(4)]
        oa_ref[bi, :, 0:LANES] = jnp.where(lo, a[0], a[1]).astype(oa_ref.dtype)
        oa_ref[bi, :, LANES:] = jnp.where(lo, a[2], a[3]).astype(oa_ref.dtype)
        oc_ref[bi, :, 0:LANES] = jnp.where(lo, c[0], pltpu.roll(c[1], half, 1)).astype(oc_ref.dtype)
        oc_ref[bi, :, LANES:] = jnp.where(lo, pltpu.roll(c[2], half, 1), c[3]).astype(oc_ref.dtype)


def _attention(q_a, q_c, segs_a, segs_c):
    b, tq_all, _ = q_a.shape
    tq = min(TQ, tq_all)
    nb = max(1, min(b, 2 * TQ // tq_all))
    in_specs = [pl.BlockSpec((nb, tq, q_a.shape[2]), lambda i, j: (i, j, 0)),
                pl.BlockSpec((nb, tq, q_c.shape[2]), lambda i, j: (i, j, 0))]
    args = [q_a, q_c]
    for segs in (segs_a, segs_c):
        for k, v, layer in segs:
            for a in (k, v):
                if layer is None:
                    in_specs.append(pl.BlockSpec((nb,) + a.shape[1:], lambda i, j: (i, 0, 0)))
                else:
                    in_specs.append(pl.BlockSpec((nb, None) + a.shape[2:],
                                                 lambda i, j, layer=layer: (i, layer, 0, 0)))
                args.append(a)
    out_spec = pl.BlockSpec((nb, tq, 2 * LANES), lambda i, j: (i, j, 0))
    return pl.pallas_call(
        functools.partial(_attn_kernel, len(segs_a)),
        grid=(b // nb, tq_all // tq),
        in_specs=in_specs,
        out_specs=[out_spec, out_spec],
        out_shape=[jax.ShapeDtypeStruct((b, tq_all, 2 * LANES), BF16)] * 2,
        compiler_params=_cparams(("parallel", "parallel")),
        name="attn_lat" if len(segs_a) > 1 else "attn_ctx",
    )(*args)


def _log_sigmoid(x):
    return jnp.minimum(x, 0.0) - jnp.log(1.0 + jnp.exp(-jnp.abs(x)))


GATE_HEADS = 2


def _mlstm_kernel(has_init, nc, hps, u_ref, v_ref, o_ref, gt_ref, cw_ref, cb_ref, wqt_ref, wk_ref, go_ref,
                  *rest):
    if has_init:
        c0_ref, n0_ref, m0_ref, out_ref, qt_s, k_s, vt_s, ht_f, ht_b, acol_s, row_s, st_s = rest
    else:
        out_ref, cfin_ref, sfin_ref, qt_s, k_s, vt_s, ht_f, ht_b, acol_s, row_s, st_s = rest
    t = nc * ML_CHUNK
    sq = (ML_CHUNK, ML_CHUNK)
    w2 = hps * LANES

    rows = lax.broadcasted_iota(jnp.int32, (t, LANES), 0)
    r_i = lax.broadcasted_iota(jnp.int32, sq, 0)
    c_i = lax.broadcasted_iota(jnp.int32, sq, 1)
    lower = c_i <= r_i
    upper = c_i >= r_i
    eye = jnp.where(c_i == r_i, 1.0, 0.0).astype(BF16)
    for j in range(hps):
        hs = slice(LANES * j, LANES * (j + 1))
        u = u_ref[:, hs]
        up = jnp.where(rows == 0, 0.0, pltpu.roll(u, 1, 0))
        un = jnp.where(rows == t - 1, 0.0, pltpu.roll(u, t - 1, 0))
        uc = _silu(cw_ref[0:1, hs] * up + cw_ref[1:2, hs] * u + cw_ref[2:3, hs] * un + cb_ref[:, hs]).astype(BF16)
        vt_s[hs, :] = _dot_nt(eye, v_ref[:, hs]).astype(BF16)
        qt_s[hs, :] = _dot_nt(wqt_ref[j], uc).astype(BF16)
        k_s[:, hs] = _dot(uc, wk_ref[j]) * ML_K_SCALE

    tri_lo = jnp.where(lower, 1.0, 0.0).astype(BF16)
    tri_up = jnp.where(upper, 1.0, 0.0).astype(BF16)
    row8 = lax.broadcasted_iota(jnp.int32, (8, ML_CHUNK), 0)
    is_f = lax.broadcasted_iota(jnp.int32, (8, t), 0) % 4 >= 2
    for pb in range(hps // GATE_HEADS):
        gates = gt_ref[pb]
        lg_all = jnp.where(is_f, _log_sigmoid(gates), gates)
        for c in range(nc):
            sl = slice(c * ML_CHUNK, (c + 1) * ML_CHUNK)
            lg = lg_all[:, sl]
            h1, h2, h3 = _split3(lg)
            cs = jnp.where(row8 < 4, _dot(h1, tri_up) + _dot(h2, tri_up) + _dot(h3, tri_up),
                           _dot(h1, tri_lo) + _dot(h2, tri_lo) + _dot(h3, tri_lo)) * LOG2E
            row_s[pb, :, sl] = cs
            a = lg * LOG2E - pltpu.roll(cs, 6, 0)
            for jj in range(GATE_HEADS):
                j = GATE_HEADS * pb + jj
                hs = slice(LANES * j, LANES * (j + 1))
                st_s[j, sl, :] = _dot(k_s[sl, hs].astype(BF16), qt_s[hs, sl])
                for d in range(2):
                    r = 4 * d + jj
                    acol_s[2 * j + d, sl, :] = jnp.broadcast_to(a[r:r + 1, :], sq).T

    def chunk(j, d, c0, ct, n, m2):
        hs = slice(LANES * j, LANES * (j + 1))
        cs_ = pl.ds(c0, ML_CHUNK)
        last = 0 if d else ML_CHUNK - 1
        qt = qt_s[hs, cs_]
        k = k_s[cs_, hs]
        vt = vt_s[hs, cs_]
        acol = acol_s[2 * j + d, cs_, :]
        r = 2 + 4 * d + j % GATE_HEADS
        bc = row_s[j // GATE_HEADS, r:r + 1, cs_]
        g = bc + m2
        dlog = jnp.where(lower if d else upper, acol + bc, -jnp.inf)
        m_t = jnp.maximum(g, jnp.max(dlog, axis=0, keepdims=True))
        w = jnp.exp2(dlog - m_t)
        inter = jnp.exp2(g - m_t)
        st = st_s[j, cs_, :] * w
        qn = _dot(jnp.broadcast_to(n, (8, LANES)).astype(BF16), qt)[0:1, :]
        den = jnp.sum(st, axis=0, keepdims=True) + inter * qn
        numt = _dot(vt, st.astype(BF16)) + inter * _dot(ct.astype(BF16), qt)
        ht = numt * (1.0 / jnp.maximum(jnp.abs(den), jnp.exp2(-m_t)))
        b_last = jnp.broadcast_to(bc[:, last:last + 1], (1, LANES))
        m_new = jnp.broadcast_to(m_t[:, last:last + 1], (1, LANES))
        kw = k * jnp.exp2(acol + (b_last - m_new))
        decay = jnp.exp2(m2 + (b_last - m_new))
        ct_new = decay * ct + _dot(vt, kw.astype(BF16))
        n_new = decay * n + jnp.sum(kw, axis=0, keepdims=True)
        return ht, ct_new, n_new, m_new

    def step(i, carry):
        cf = i * ML_CHUNK
        cb = (nc - 1 - i) * ML_CHUNK
        if not isinstance(i, int):
            cf = pl.multiple_of(cf, ML_CHUNK)
            cb = pl.multiple_of(cb, ML_CHUNK)
        new = []
        for j in range(hps):
            hs = slice(LANES * j, LANES * (j + 1))
            for d, (c0, ht_s) in enumerate(((cf, ht_f), (cb, ht_b))):
                ht, ct, n, m2 = chunk(j, d, c0, *carry[2 * j + d])
                ht_s[hs, pl.ds(c0, ML_CHUNK)] = ht
                new.append((ct, n, m2))
        return tuple(new)

    if has_init:
        init = tuple((c0_ref[d, j].T, n0_ref[j, d:d + 1, :], m0_ref[j, d:d + 1, :] * LOG2E)
                     for j in range(hps) for d in range(2))
    else:
        init = tuple((jnp.zeros(sq, F32), jnp.zeros((1, LANES), F32), jnp.zeros((1, LANES), F32))
                     for j in range(hps) for d in range(2))
    if nc <= 16:
        carry = init
        for i in range(nc):
            carry = step(i, carry)
    else:
        carry = lax.fori_loop(0, nc, step, init, unroll=4)

    for j in range(hps):
        hs = slice(LANES * j, LANES * (j + 1))
        hsum = (ht_f[hs, :] + ht_b[hs, :]).T
        hn = _rms(hsum, go_ref[:, hs])
        out_ref[:, hs] = (hn * jax.nn.sigmoid(o_ref[:, hs])).astype(out_ref.dtype)

    if not has_init:
        for j in range(hps):
            for d in range(2):
                ct, n, m2 = carry[2 * j + d]
                cfin_ref[d, j] = ct.T
            nf, nb = carry[2 * j][1], carry[2 * j + 1][1]
            mf, mb = carry[2 * j][2], carry[2 * j + 1][2]
            sfin_ref[j] = jnp.concatenate(
                [nf, nb, mf * (1.0 / LOG2E), mb * (1.0 / LOG2E), jnp.zeros((4, LANES), F32)], axis=0)


def _mlstm(u, v_ml, o_ml, gates, l, wts, init):
    b, t, _ = u.shape
    nc = t // ML_CHUNK
    has_init = init is not None
    hps = ML_HEADS if t <= 4 * ML_CHUNK else GATE_HEADS
    w2 = hps * LANES
    pair_blk = lambda i, p: (i, 0, p)
    in_specs = [
        pl.BlockSpec((None, t, w2), pair_blk),
        pl.BlockSpec((None, t, w2), pair_blk),
        pl.BlockSpec((None, t, w2), pair_blk),
        pl.BlockSpec((hps // GATE_HEADS, 8, t), lambda i, p: (p, 0, i)),
        pl.BlockSpec((None, 3, w2), lambda i, p: (l, 0, p)),
        pl.BlockSpec((None, 1, w2), lambda i, p: (l, 0, p)),
        pl.BlockSpec((None, hps, ML_DH, ML_DH), lambda i, p: (l, p, 0, 0)),
        pl.BlockSpec((None, hps, ML_DH, ML_DH), lambda i, p: (l, p, 0, 0)),
        pl.BlockSpec((None, 1, w2), lambda i, p: (l, 0, p)),
    ]
    args = [u, v_ml, o_ml, gates, wts["w_ml_conv"], wts["b_ml_conv"], wts["w_ml_qt"], wts["w_ml_k"],
            wts["g_ml_out"]]
    out_specs = [pl.BlockSpec((None, t, w2), pair_blk)]
    out_shape = [jax.ShapeDtypeStruct((b, t, ML_WIDTH), BF16)]
    state_c = pl.BlockSpec((None, 2, hps, ML_DH, ML_DH), lambda i, p: (i, 0, p, 0, 0))
    state_s = pl.BlockSpec((None, hps, 8, LANES), lambda i, p: (i, p, 0, 0))
    if has_init:
        in_specs += [pl.BlockSpec((None, None, 2, hps, ML_DH, ML_DH), lambda i, p: (i, l, 0, p, 0, 0)),
                     pl.BlockSpec((None, None, hps, 8, LANES), lambda i, p: (i, l, p, 0, 0)),
                     pl.BlockSpec((None, None, hps, 8, LANES), lambda i, p: (i, l, p, 0, 0))]
        args += list(init)
    else:
        out_specs += [state_c, state_s]
        out_shape += [
            jax.ShapeDtypeStruct((b, 2, ML_HEADS, ML_DH, ML_DH), F32),
            jax.ShapeDtypeStruct((b, ML_HEADS, 8, LANES), F32),
        ]
    return pl.pallas_call(
        functools.partial(_mlstm_kernel, has_init, nc, hps),
        grid=(b, ML_HEADS // hps),
        in_specs=in_specs,
        out_specs=out_specs,
        out_shape=out_shape,
        scratch_shapes=[
            pltpu.VMEM((w2, t), BF16),
            pltpu.VMEM((t, w2), F32),
            pltpu.VMEM((w2, t), BF16),
            pltpu.VMEM((w2, t), F32),
            pltpu.VMEM((w2, t), F32),
            pltpu.VMEM((2 * hps, t, LANES), F32),
            pltpu.VMEM((hps // GATE_HEADS, 8, t), F32),
            pltpu.VMEM((hps, t, LANES), F32),
        ],
        compiler_params=_cparams(("parallel", "parallel")),
        name="mlstm_lat" if has_init else "mlstm_ctx",
    )(*args)


def _ffn_kernel(seq, tm, final, *refs):
    x_refs, oa_refs, ob_refs, oc_refs = refs[0:3], refs[3:6], refs[6:9], refs[9:12]
    (mod_ref, g2_ref, wout_ref, wup_ref, cw_ref, cb_ref, wdn_ref, gf_ref, y_ref,
     oext_s, yext_s, gext_s, h_s) = refs[12:]
    i = pl.program_id(0)
    a_w = MLA_HEADS * MLA_V
    ext = ((0, HALO), (HALO, tm), (HALO + tm, HALO))
    for (r0, n), k in zip(ext, (1, 0, 2)):
        oext_s[r0:r0 + n, 0:a_w] = oa_refs[k][...]
        oext_s[r0:r0 + n, a_w:a_w + ML_WIDTH] = ob_refs[k][...]
        oext_s[r0:r0 + n, a_w + ML_WIDTH:] = oc_refs[k][...]
    mix = _dot(oext_s[...], wout_ref[...])
    gate1 = mod_ref[2:3, :]
    x = x_refs[0][...] + gate1 * mix[HALO:HALO + tm, :]
    xp = x_refs[1][...] + gate1 * mix[0:HALO, :]
    xn = x_refs[2][...] + gate1 * mix[HALO + tm:, :]

    g2 = g2_ref[...]
    shift = mod_ref[3:4, :]
    scale = mod_ref[4:5, :]
    spans = tm > seq
    if spans:
        keep_p = keep_n = 1.0
        tok = lax.broadcasted_iota(jnp.int32, (tm, FF_CHUNK), 0) % seq
        first, last = tok == 0, tok == seq - 1
    else:
        keep_p = jnp.where((i * tm) % seq == 0, 0.0, 1.0)
        keep_n = jnp.where(((i + 1) * tm) % seq == 0, 0.0, 1.0)
    yext_s[0:HALO, :] = (_adaln(xp, g2, shift, scale) * keep_p).astype(BF16)
    yext_s[HALO:HALO + tm, :] = _adaln(x, g2, shift, scale).astype(BF16)
    yext_s[HALO + tm:, :] = (_adaln(xn, g2, shift, scale) * keep_n).astype(BF16)

    def up(c):
        c0 = c * FF_CHUNK
        gext_s[c % 2] = _dot(yext_s[...], wup_ref[:, D_FF + c0:D_FF + c0 + FF_CHUNK])
        return _dot(yext_s[HALO:HALO + tm, :], wup_ref[:, c0:c0 + FF_CHUNK])

    nchunk = D_FF // FF_CHUNK
    a_next = up(0)
    for c in range(nchunk):
        c0 = c * FF_CHUNK
        a = a_next
        if c + 1 < nchunk:
            a_next = up(c + 1)
        gs = gext_s.at[c % 2]
        g_prev = gs[HALO - 1:HALO - 1 + tm, :]
        g_next = gs[HALO + 1:HALO + 1 + tm, :]
        if spans:
            g_prev = jnp.where(first, 0.0, g_prev)
            g_next = jnp.where(last, 0.0, g_next)
        g = (cw_ref[0:1, c0:c0 + FF_CHUNK] * g_prev
             + cw_ref[1:2, c0:c0 + FF_CHUNK] * gs[HALO:HALO + tm, :]
             + cw_ref[2:3, c0:c0 + FF_CHUNK] * g_next
             + cb_ref[:, c0:c0 + FF_CHUNK])
        h_s[:, c0:c0 + FF_CHUNK] = (_silu(g) * a).astype(BF16)
    y = x + mod_ref[5:6, :] * _dot(h_s[...], wdn_ref[...])
    if final:
        y = _rms(y, gf_ref[...])
    y_ref[...] = y


def _ffn(x2, o_a, o_b, o_c, mod_all, l, seq, is_ctx, wts, g_final, final):
    m = x2.shape[0]
    tm = TM
    assert tm % seq == 0 or seq % tm == 0
    nb = tm // HALO
    last_blk = m // HALO - 1
    row = lambda i: (i, 0)
    prev = lambda i: (jnp.maximum(i * nb - 1, 0), 0)
    nxt = lambda i: (jnp.minimum((i + 1) * nb, last_blk), 0)
    in_specs, args = [], []
    for a in (x2, o_a, o_b, o_c):
        w = a.shape[1]
        in_specs += [pl.BlockSpec((tm, w), row), pl.BlockSpec((HALO, w), prev), pl.BlockSpec((HALO, w), nxt)]
        args += [a, a, a]
    names = ["g_norm2", "w_out", "w_ff_up", "w_ff_conv", "b_ff_conv", "w_ff_down"]
    in_specs += ([_mod_spec(l, seq, tm, is_ctx)] + [_lspec(wts[n], l) for n in names]
                 + [pl.BlockSpec((1, D_MODEL), lambda i: (0, 0))])
    args += [mod_all] + [wts[n] for n in names] + [g_final]
    return pl.pallas_call(
        functools.partial(_ffn_kernel, seq, tm, final),
        grid=(m // tm,),
        in_specs=in_specs,
        out_specs=pl.BlockSpec((tm, D_MODEL), row),
        out_shape=jax.ShapeDtypeStruct((m, D_MODEL), F32),
        scratch_shapes=[pltpu.VMEM((tm + 2 * HALO, D_MODEL), BF16),
                        pltpu.VMEM((tm + 2 * HALO, D_MODEL), BF16),
                        pltpu.VMEM((2, tm + 2 * HALO, FF_CHUNK), F32),
                        pltpu.VMEM((tm, D_FF), BF16)],
        compiler_params=_cparams(("parallel",)),
        name="ffn_ctx" if is_ctx else "ffn_lat",
    )(*args)


def _rope_tables(t):
    pos = np.arange(t)
    rows = (pos // GRID_W).astype(np.float32)
    cols = (pos % GRID_W).astype(np.float32)

    def group(p, d):
        inv = (np.float32(ROPE_THETA) ** (-np.arange(0, d, 2, dtype=np.float32) / np.float32(d))).astype(np.float32)
        ang = (p[:, None] * inv[None, :]).astype(np.float32)
        cs, sn, z = np.cos(ang), np.sin(ang), np.zeros_like(ang)
        return (np.concatenate([cs, cs], 1), np.concatenate([-sn, z], 1), np.concatenate([z, sn], 1))

    gr, gc = group(rows, MLA_ROPE // 2), group(cols, MLA_ROPE // 2)
    ones = np.ones((t, MLA_NOPE), np.float32)
    zeros = np.zeros((t, MLA_NOPE), np.float32)
    pad1 = np.ones((t, LANES - MLA_NOPE - MLA_ROPE), np.float32)
    pad0 = np.zeros((t, LANES - MLA_NOPE - MLA_ROPE), np.float32)
    mla = (np.concatenate([ones, gr[0], gc[0], pad1], 1),
           np.concatenate([zeros, gr[1], gc[1], pad0], 1),
           np.concatenate([zeros, gr[2], gc[2], pad0], 1))
    gr, gc = group(rows, GQA_DH // 2), group(cols, GQA_DH // 2)
    gqa = tuple(np.concatenate([gr[k], gc[k], gr[k], gc[k]], 1) for k in range(3))
    return tuple(jnp.asarray(a, F32) for a in mla + gqa)


def _prep_weights(p):
    w_in = p["w_in"]
    nl = w_in.shape[0]
    qg = [w_in[:, :, 1968 + GQA_DH * h:1968 + GQA_DH * (h + 1)] for h in (0, 2, 1, 3)]
    gh = GATE_HEADS

    def regroup(g16):
        blocks = []
        for pr in range(ML_HEADS // gh):
            cols = [g16[..., kind * ML_HEADS + gh * pr:kind * ML_HEADS + gh * (pr + 1)] for kind in range(4)]
            blocks += cols + [jnp.zeros(g16.shape[:-1] + (LANES - 4 * gh,), F32)]
        return jnp.concatenate(blocks, axis=-1)

    w_gates = regroup(w_in[:, :, 1952:1968])
    b_gates = regroup(p["b_ml_gates"])
    w_in_p = jnp.concatenate([
        w_in[:, :, 0:384],
        jnp.pad(w_in[:, :, 384:416], ((0, 0), (0, 0), (MLA_NOPE, LANES - MLA_NOPE - MLA_ROPE))),
        w_in[:, :, 416:1952],
        w_gates,
        *qg,
        w_in[:, :, 2224:2480],
    ], axis=2).astype(BF16)
    w_uq = p["w_mla_uq"].reshape(nl, MLA_Q_RANK, MLA_HEADS, MLA_NOPE + MLA_ROPE)
    w_uq = jnp.pad(w_uq, ((0, 0), (0, 0), (0, 0), (0, LANES - MLA_NOPE - MLA_ROPE)))
    w_ukv = p["w_mla_ukv"].reshape(nl, MLA_KV_RANK, MLA_HEADS, MLA_NOPE + MLA_V)
    w_uk = jnp.pad(w_ukv[..., :MLA_NOPE], ((0, 0), (0, 0), (0, 0), (0, LANES - MLA_NOPE)))
    w_ukv_p = jnp.concatenate([w_uk.reshape(nl, MLA_KV_RANK, -1),
                               w_ukv[..., MLA_NOPE:].reshape(nl, MLA_KV_RANK, -1)], axis=2)
    grp = np.arange(2 * LANES) // GQA_DH
    row = lambda a: a[:, None, :]
    return {
        "g_norm1": row(p["g_norm1"]),
        "g_norm2": row(p["g_norm2"]),
        "w_in": w_in_p,
        "g_mla_q": row(p["g_mla_q"]),
        "w_uq": w_uq.reshape(nl, MLA_Q_RANK, -1).astype(BF16),
        "g_mla_kv": row(p["g_mla_kv"]),
        "w_ukv": w_ukv_p.astype(BF16),
        "b_gates": row(b_gates),
        "g_gqa_q": row(jnp.tile(p["g_gqa_q"], (1, GQA_HEADS))),
        "g_gqa_k": row(jnp.tile(p["g_gqa_k"], (1, GQA_KV_HEADS))),
        "gsum": jnp.asarray(grp[:, None] == grp[None, :], BF16),
        "w_ml_conv": p["w_ml_conv"],
        "b_ml_conv": row(p["b_ml_conv"]),
        "w_ml_qt": jnp.swapaxes(p["w_ml_q"], 2, 3).astype(BF16),
        "w_ml_k": p["w_ml_k"].astype(BF16),
        "g_ml_out": row(p["g_ml_out"]),
        "w_out": p["w_out"].astype(BF16),
        "w_ff_up": p["w_ff_up"].astype(BF16),
        "w_ff_conv": p["w_ff_conv"],
        "b_ff_conv": row(p["b_ff_conv"]),
        "w_ff_down": p["w_ff_down"].astype(BF16),
    }


def _layer(x2, mod_all, l, seq, is_ctx, wts, tabs, cache, g_final, final):
    m = x2.shape[0]
    b = m // seq
    r3 = lambda a: a.reshape(b, seq, a.shape[-1])
    outs = _in_proj(x2, mod_all, l, seq, is_ctx, wts, tabs)
    q_m, k_m, v_m, u, v_ml, o_ml, gates, q_g, k_g, v_g = outs[:10]
    segs_a = [(r3(k_m), r3(v_m), None)]
    segs_c = [(r3(k_g), r3(v_g), None)]
    init = None
    if not is_ctx:
        ckv_c, kr_pad, kg_c, vg_c, ct0, n0p, m0p = cache
        kc, vc = _cache_kv(ckv_c, kr_pad, l, wts["w_ukv"])
        segs_a.append((kc, vc, None))
        segs_c.append((kg_c, vg_c, l))
        init = (ct0, n0p, m0p)
    o_a, o_c = _attention(r3(q_m), r3(q_g), segs_a, segs_c)
    ml = _mlstm(r3(u), r3(v_ml), r3(o_ml), gates, l, wts, init)
    x2 = _ffn(x2, o_a.reshape(m, -1), ml[0].reshape(m, -1), o_c.reshape(m, -1), mod_all, l, seq, is_ctx, wts,
              g_final, final)
    state = None
    if is_ctx:
        ckv_n, kr_raw = outs[10:12]
        sfin = ml[2]
        state = (
            r3(ckv_n),
            r3(kr_raw)[:, :, MLA_NOPE:MLA_NOPE + MLA_ROPE],
            k_g.reshape(b, seq, GQA_KV_HEADS, GQA_DH),
            v_g.reshape(b, seq, GQA_KV_HEADS, GQA_DH),
            ml[1],
            jnp.swapaxes(sfin[:, :, 0:2, :], 1, 2),
            jnp.swapaxes(sfin[:, :, 2:4, 0], 1, 2),
        )
    return x2, state


def kernel(x_prompt, x_sample, cache_mla_ckv, cache_mla_krope, cache_gqa_k, cache_gqa_v, state_mlstm_C, state_mlstm_n, state_mlstm_m, c, c_ctx, w_ada, b_ada, g_norm1, g_norm2, w_in, g_mla_q, w_mla_uq, g_mla_kv, w_mla_ukv, w_ml_conv, b_ml_conv, w_ml_q, w_ml_k, b_ml_gates, g_ml_out, g_gqa_q, g_gqa_k, w_out, w_ff_up, w_ff_conv, b_ff_conv, w_ff_down, g_final):
    params = {
        "g_norm1": g_norm1, "g_norm2": g_norm2, "w_in": w_in, "g_mla_q": g_mla_q, "w_mla_uq": w_mla_uq,
        "g_mla_kv": g_mla_kv, "w_mla_ukv": w_mla_ukv, "w_ml_conv": w_ml_conv, "b_ml_conv": b_ml_conv,
        "w_ml_q": w_ml_q, "w_ml_k": w_ml_k, "b_ml_gates": b_ml_gates, "g_ml_out": g_ml_out,
        "g_gqa_q": g_gqa_q, "g_gqa_k": g_gqa_k, "w_out": w_out, "w_ff_up": w_ff_up, "w_ff_conv": w_ff_conv,
        "b_ff_conv": b_ff_conv, "w_ff_down": w_ff_down,
    }
    depth = w_in.shape[0]
    bp, sp, _ = x_prompt.shape
    bs, ss, _ = x_sample.shape
    assert bs + 1 <= 16 and sp % ML_CHUNK == 0 and ss % TM == 0 and (bp * sp) % TM == 0

    cvec = jnp.concatenate([c_ctx[None, :], c, jnp.zeros((16 - 1 - bs, D_MODEL), F32)], axis=0)
    mod_all = _modulation(cvec, w_ada, b_ada).reshape(depth, 16, 6, D_MODEL)
    tabs = _rope_tables(ss)
    gf = g_final[None, :]
    wts = _prep_weights(params)

    tc = cache_mla_ckv.shape[2]
    pad_rows = ((0, 0), (0, 0), (0, 0), (0, 8 - 2), (0, 0))
    cache = (
        cache_mla_ckv,
        jnp.pad(cache_mla_krope, ((0, 0), (0, 0), (0, 0), (MLA_NOPE, LANES - MLA_NOPE - MLA_ROPE))),
        cache_gqa_k.reshape(bs, depth, tc, GQA_KV_HEADS * GQA_DH),
        cache_gqa_v.reshape(bs, depth, tc, GQA_KV_HEADS * GQA_DH),
        state_mlstm_C,
        jnp.pad(jnp.swapaxes(state_mlstm_n, 2, 3), pad_rows),
        jnp.pad(jnp.broadcast_to(jnp.swapaxes(state_mlstm_m, 2, 3)[..., None], (bs, depth, ML_HEADS, 2, LANES)),
                pad_rows),
    )

    xp = x_prompt.reshape(bp * sp, D_MODEL)
    xs = x_sample.reshape(bs * ss, D_MODEL)
    states = []
    for l in range(depth):
        final = l == depth - 1
        xp, st = _layer(xp, mod_all, l, sp, True, wts, None, None, gf, final)
        states.append(st)
        xs, _ = _layer(xs, mod_all, l, ss, False, wts, tabs, cache, gf, final)
    new_state = tuple(jnp.stack([st[k] for st in states], axis=1) for k in range(7))
    return (xp.reshape(bp, sp, D_MODEL), xs.reshape(bs, ss, D_MODEL)) + new_state
```

```python
import functools

import jax
import jax.numpy as jnp
import numpy as np
from jax import lax
from jax.experimental import pallas as pl
from jax.experimental.pallas import tpu as pltpu

F32 = jnp.float32
BF16 = jnp.bfloat16

D_MODEL = 1024
GRID_W = 64
ROPE_THETA = 10000.0
EPS = 1e-6
MLA_HEADS = 4
MLA_NOPE = 64
MLA_ROPE = 32
MLA_V = 64
MLA_Q_RANK = 256
MLA_KV_RANK = 128
ML_HEADS = 4
ML_DH = 128
ML_WIDTH = ML_HEADS * ML_DH
ML_CHUNK = 128
GQA_HEADS = 4
GQA_KV_HEADS = 2
GQA_DH = 64
D_FF = 2816
MLA_SCALE = (MLA_NOPE + MLA_ROPE) ** -0.5
GQA_SCALE = GQA_DH ** -0.5
ML_K_SCALE = ML_DH ** -0.5
LOG2E = 1.4426950408889634

LANES = 128
HALO = 16
VMEM_LIMIT = 52 * 1024 * 1024

C_CQ, C_CKV, C_X, C_U, C_V, C_O, C_QG, C_KG, C_VG, IN_COLS_P = (
    0, 256, 384, 512, 1024, 1536, 2048, 2304, 2432, 2560)
GATE_BLOCKS = 2
FF_CHUNK = 256
TM = 512
TQ = 512


def _cparams(sem):
    return pltpu.CompilerParams(dimension_semantics=sem, vmem_limit_bytes=VMEM_LIMIT)


def _lspec(arr, l):
    nd = arr.ndim - 1
    return pl.BlockSpec((None,) + arr.shape[1:], lambda *_: (l,) + (0,) * nd, pipeline_mode=pl.Buffered(1))


def _dot(a, b):
    return jnp.dot(a, b, preferred_element_type=F32)


def _dot_nt(a, b):
    return lax.dot_general(a, b, (((1,), (1,)), ((), ())), preferred_element_type=F32)


def _dot_tn(a, b):
    return lax.dot_general(a, b, (((0,), (0,)), ((), ())), preferred_element_type=F32)


def _rms(x, g):
    return (x * lax.rsqrt(jnp.mean(x * x, axis=-1, keepdims=True) + EPS)) * g


def _silu(x):
    return x * jax.nn.sigmoid(x)


def _adaln(x, g, shift, scale):
    return _rms(x, g) * (1.0 + scale) + shift


def _split2(x):
    hi = x.astype(BF16)
    lo = (x - hi.astype(F32)).astype(BF16)
    return hi, lo


def _split3(x):
    h1 = x.astype(BF16)
    r1 = x - h1.astype(F32)
    h2 = r1.astype(BF16)
    h3 = (r1 - h2.astype(F32)).astype(BF16)
    return h1, h2, h3


def _rope(x, cos, sa, sb, half):
    w = x.shape[-1]
    return x * cos + pltpu.roll(x, w - half, 1) * sa + pltpu.roll(x, half, 1) * sb


def _group_rms(x, gsum, g):
    hi, lo = _split2(x * x)
    ss = _dot(hi, gsum) + _dot(lo, gsum)
    return (x * lax.rsqrt(ss * (1.0 / GQA_DH) + EPS)) * g


def _mod_kernel(c_ref, w_ref, b_ref, o_ref):
    a = _silu(c_ref[...]).astype(BF16)
    o_ref[...] = _dot(a, w_ref[...].astype(BF16)) + b_ref[...]


def _modulation(cvec, w_ada, b_ada):
    nl = w_ada.shape[0]
    tn = 1536
    return pl.pallas_call(
        _mod_kernel,
        grid=(nl, 6 * D_MODEL // tn),
        in_specs=[
            pl.BlockSpec((16, D_MODEL), lambda l, j: (0, 0)),
            pl.BlockSpec((None, D_MODEL, tn), lambda l, j: (l, 0, j)),
            pl.BlockSpec((None, 1, tn), lambda l, j: (l, 0, j)),
        ],
        out_specs=pl.BlockSpec((None, 16, tn), lambda l, j: (l, 0, j)),
        out_shape=jax.ShapeDtypeStruct((nl, 16, 6 * D_MODEL), F32),
        compiler_params=_cparams(("arbitrary", "arbitrary")),
        name="modulation",
    )(cvec, w_ada, b_ada.reshape(nl, 1, 6 * D_MODEL))


def _in_proj_kernel(is_ctx, x_ref, mod_ref, g1_ref, win_ref, gq_ref, wuq_ref, gkv_ref, wukv_ref,
                    bg_ref, ggq_ref, ggk_ref, gsum_ref, *rest):
    if is_ctx:
        (qm_ref, km_ref, vm_ref, u_ref, vml_ref, oml_ref, gt_ref, qg_ref, kg_ref, vg_ref,
         ckv_ref, kr_ref) = rest
    else:
        (cm_ref, sam_ref, sbm_ref, cg_ref, sag_ref, sbg_ref,
         qm_ref, km_ref, vm_ref, u_ref, vml_ref, oml_ref, gt_ref, qg_ref, kg_ref, vg_ref) = rest

    def modulated(s):
        rs = slice(s * TM, (s + 1) * TM)
        return _adaln(x_ref[rs, :], g1_ref[...], mod_ref[0:1, :], mod_ref[1:2, :]).astype(BF16)

    nsub = x_ref.shape[0] // TM
    lane = lax.broadcasted_iota(jnp.int32, (TM, LANES), 1)
    kr_lanes = jnp.logical_and(lane >= MLA_NOPE, lane < MLA_NOPE + MLA_ROPE)
    y_next = modulated(0)
    for s in range(nsub):
        rs = slice(s * TM, (s + 1) * TM)
        y = y_next

        def proj(c0, width, y=y):
            return _dot(y, win_ref[:, c0:c0 + width])

        if not is_ctx:
            rope_m = (cm_ref[rs, :], sam_ref[rs, :], sbm_ref[rs, :], 8)
            rope_g = (cg_ref[rs, :], sag_ref[rs, :], sbg_ref[rs, :], 16)

        z_cq = proj(C_CQ, MLA_Q_RANK)
        ckv_kr = proj(C_CKV, MLA_KV_RANK + LANES)
        if s + 1 < nsub:
            y_next = modulated(s + 1)

        cqn = _rms(z_cq, gq_ref[...]).astype(BF16)
        z_qg = proj(C_QG, GQA_HEADS * GQA_DH)
        qz = _dot(cqn, wuq_ref[...])

        ckvn = _rms(ckv_kr[:, 0:MLA_KV_RANK], gkv_ref[...])
        x_blk = ckv_kr[:, MLA_KV_RANK:]
        kr = jnp.where(kr_lanes, x_blk, 0.0)
        if is_ctx:
            ckv_ref[rs, :] = ckvn
            kr_ref[rs, :] = kr
        else:
            kr = _rope(kr, *rope_m)
        kv_g = proj(C_KG, 2 * LANES)
        kvz = _dot(ckvn.astype(BF16), wukv_ref[...])

        z_u = proj(C_U, ML_WIDTH)
        for h in range(MLA_HEADS):
            blk = qz[:, LANES * h:LANES * (h + 1)]
            if not is_ctx:
                blk = _rope(blk, *rope_m)
            qm_ref[rs, LANES * h:LANES * (h + 1)] = (blk * (MLA_SCALE * LOG2E)).astype(qm_ref.dtype)

        z_v = proj(C_V, ML_WIDTH)
        for h in range(MLA_HEADS):
            km_ref[rs, LANES * h:LANES * (h + 1)] = (kvz[:, LANES * h:LANES * (h + 1)] + kr).astype(km_ref.dtype)
        vm_ref[rs, :] = kvz[:, MLA_HEADS * LANES:].astype(vm_ref.dtype)
        u_ref[rs, :] = z_u.astype(u_ref.dtype)

        z_o = proj(C_O, ML_WIDTH)
        qg = _group_rms(z_qg, gsum_ref[...], ggq_ref[...])
        kg = _group_rms(kv_g[:, 0:LANES], gsum_ref[0:LANES, 0:LANES], ggk_ref[...])
        vml_ref[rs, :] = z_v.astype(vml_ref.dtype)
        if not is_ctx:
            qg = jnp.concatenate([_rope(qg[:, 0:LANES], *rope_g), _rope(qg[:, LANES:], *rope_g)], axis=1)
            kg = _rope(kg, *rope_g)
        qg_ref[rs, :] = (qg * (GQA_SCALE * LOG2E)).astype(qg_ref.dtype)
        kg_ref[rs, :] = kg.astype(kg_ref.dtype)
        vg_ref[rs, :] = kv_g[:, LANES:].astype(vg_ref.dtype)

        oml_ref[rs, :] = z_o.astype(oml_ref.dtype)
        gates_t = (x_blk + bg_ref[...]).T
        for pr in range(GATE_BLOCKS):
            gt_ref[pr, :, rs] = gates_t[8 * pr:8 * (pr + 1), :]


def _mod_spec(l, seq, tm, is_ctx):
    if is_ctx:
        return pl.BlockSpec((None, None, 6, D_MODEL), lambda i: (l, 0, 0, 0))
    return pl.BlockSpec((None, None, 6, D_MODEL), lambda i: (l, 1 + (i * tm) // seq, 0, 0))


def _in_proj(x2, mod_all, l, seq, is_ctx, wts, tabs):
    m = x2.shape[0]
    tm = 2 * TM
    assert m % tm == 0 and (is_ctx or seq % tm == 0)
    nt = m // tm
    row = lambda i: (i, 0)
    names = ["g_norm1", "w_in", "g_mla_q", "w_uq", "g_mla_kv", "w_ukv", "b_gates", "g_gqa_q", "g_gqa_k"]
    in_specs = ([pl.BlockSpec((tm, D_MODEL), row), _mod_spec(l, seq, tm, is_ctx)]
                + [_lspec(wts[n], l) for n in names]
                + [pl.BlockSpec((2 * LANES, 2 * LANES), lambda i: (0, 0))])
    args = [x2, mod_all] + [wts[n] for n in names] + [wts["gsum"]]
    if not is_ctx:
        tpos = seq // tm
        in_specs += [pl.BlockSpec((tm, LANES), lambda i: (i % tpos, 0))] * 6
        args += list(tabs)
    act = F32 if is_ctx else BF16
    outs = [
        (MLA_HEADS * LANES, BF16),
        (MLA_HEADS * LANES, BF16),
        (MLA_HEADS * MLA_V, BF16),
        (ML_WIDTH, F32),
        (ML_WIDTH, BF16),
        (ML_WIDTH, F32),
        (None, F32),
        (GQA_HEADS * GQA_DH, BF16),
        (LANES, act),
        (LANES, act),
    ]
    if is_ctx:
        outs += [(MLA_KV_RANK, F32), (LANES, F32)]
    return pl.pallas_call(
        functools.partial(_in_proj_kernel, is_ctx),
        grid=(nt,),
        in_specs=in_specs,
        out_specs=[pl.BlockSpec((GATE_BLOCKS, 8, tm), lambda i: (0, 0, i)) if w is None
                   else pl.BlockSpec((tm, w), row) for w, _ in outs],
        out_shape=[jax.ShapeDtypeStruct((GATE_BLOCKS, 8, m) if w is None else (m, w), dt)
                   for w, dt in outs],
        compiler_params=_cparams(("parallel",)),
        name="in_proj_ctx" if is_ctx else "in_proj_lat",
    )(*args)


def _cache_kv_kernel(ckv_ref, kr_ref, wukv_ref, k_ref, v_ref):
    for i in range(ckv_ref.shape[0]):
        kvz = _dot(ckv_ref[i].astype(BF16), wukv_ref[...])
        kr = kr_ref[i]
        for h in range(MLA_HEADS):
            k_ref[i, :, LANES * h:LANES * (h + 1)] = (kvz[:, LANES * h:LANES * (h + 1)] + kr).astype(k_ref.dtype)
        v_ref[i] = kvz[:, MLA_HEADS * LANES:].astype(v_ref.dtype)


def _cache_kv(ckv, kr_pad, l, w_ukv):
    b, _, tc, _ = ckv.shape
    cache_blk = pl.BlockSpec((b, None, tc, LANES), lambda i: (0, l, 0, 0))
    return pl.pallas_call(
        _cache_kv_kernel,
        grid=(1,),
        in_specs=[cache_blk, cache_blk, _lspec(w_ukv, l)],
        out_specs=[pl.BlockSpec((b, tc, MLA_HEADS * LANES), lambda i: (0, 0, 0)),
                   pl.BlockSpec((b, tc, MLA_HEADS * MLA_V), lambda i: (0, 0, 0))],
        out_shape=[jax.ShapeDtypeStruct((b, tc, MLA_HEADS * LANES), BF16),
                   jax.ShapeDtypeStruct((b, tc, MLA_HEADS * MLA_V), BF16)],
        compiler_params=_cparams(("arbitrary",)),
        name="cache_kv",
    )(ckv, kr_pad, w_ukv)


def _attn_kernel(nseg, qa_ref, qc_ref, *refs):
    kva, kvc = refs[:2 * nseg], refs[2 * nseg:4 * nseg]
    oa_ref, oc_ref = refs[4 * nseg:]
    nb, tq = qa_ref.shape[0], qa_ref.shape[1]
    lo = lax.broadcasted_iota(jnp.int32, (tq, LANES), 1) < (LANES // 2)

    def qk(unit):
        bi, is_mla, h = unit
        kv = kva if is_mla else kvc
        if is_mla:
            qh = qa_ref[bi, :, LANES * h:LANES * (h + 1)]
            ksl = slice(LANES * h, LANES * (h + 1))
        else:
            blk = qc_ref[bi, :, LANES * (h % 2):LANES * (h % 2 + 1)].astype(F32)
            qh = jnp.where(lo if h < 2 else jnp.logical_not(lo), blk, 0.0).astype(BF16)
            ksl = slice(0, LANES)
        return [_dot_nt(qh, kv[2 * j][bi, :, ksl].astype(BF16)) for j in range(nseg)]

    def softmax(ss):
        mx = jnp.max(ss[0], axis=-1, keepdims=True)
        for s in ss[1:]:
            mx = jnp.maximum(mx, jnp.max(s, axis=-1, keepdims=True))
        es = [jnp.exp2(s - mx) for s in ss]
        den = es[0].sum(axis=-1, keepdims=True)
        for e in es[1:]:
            den = den + e.sum(axis=-1, keepdims=True)
        return [e.astype(BF16) for e in es], den

    def pv(unit, es, den):
        bi, is_mla, h = unit
        kv = kva if is_mla else kvc
        vsl = slice(LANES * (h // 2), LANES * (h // 2 + 1)) if is_mla else slice(0, LANES)
        acc = None
        for j, e in enumerate(es):
            a = _dot(e, kv[2 * j + 1][bi, :, vsl].astype(BF16))
            acc = a if acc is None else acc + a
        return acc / den

    units = [(bi, is_mla, h) for bi in range(nb) for is_mla in (True, False) for h in range(4)]
    out = {}
    ahead = qk(units[0])
    for n, unit in enumerate(units):
        ss = ahead
        if n + 1 < len(units):
            ahead = qk(units[n + 1])
        out[unit] = pv(unit, *softmax(ss))
    half = LANES // 2
    for bi in range(nb):
        a = [out[(bi, True, h)] for h in range(4)]
        c = [out[(bi, False, h)] for h in range(4)]
        oa_ref[bi, :, 0:LANES] = jnp.where(lo, a[0], a[1]).astype(oa_ref.dtype)
        oa_ref[bi, :, LANES:] = jnp.where(lo, a[2], a[3]).astype(oa_ref.dtype)
        oc_ref[bi, :, 0:LANES] = jnp.where(lo, c[0], pltpu.roll(c[1], half, 1)).astype(oc_ref.dtype)
        oc_ref[bi, :, LANES:] = jnp.where(lo, pltpu.roll(c[2], half, 1), c[3]).astype(oc_ref.dtype)


def _attention(q_a, q_c, segs_a, segs_c):
    b, tq_all, _ = q_a.shape
    tq = min(TQ, tq_all)
    nb = max(1, min(b, 2 * TQ // tq_all))
    in_specs = [pl.BlockSpec((nb, tq, q_a.shape[2]), lambda i, j: (i, j, 0)),
                pl.BlockSpec((nb, tq, q_c.shape[2]), lambda i, j: (i, j, 0))]
    args = [q_a, q_c]
    for segs in (segs_a, segs_c):
        for k, v, layer in segs:
            for a in (k, v):
                if layer is None:
                    in_specs.append(pl.BlockSpec((nb,) + a.shape[1:], lambda i, j: (i, 0, 0)))
                else:
                    in_specs.append(pl.BlockSpec((nb, None) + a.shape[2:],
                                                 lambda i, j, layer=layer: (i, layer, 0, 0)))
                args.append(a)
    out_spec = pl.BlockSpec((nb, tq, 2 * LANES), lambda i, j: (i, j, 0))
    return pl.pallas_call(
        functools.partial(_attn_kernel, len(segs_a)),
        grid=(b // nb, tq_all // tq),
        in_specs=in_specs,
        out_specs=[out_spec, out_spec],
        out_shape=[jax.ShapeDtypeStruct((b, tq_all, 2 * LANES), BF16)] * 2,
        compiler_params=_cparams(("parallel", "parallel")),
        name="attn_lat" if len(segs_a) > 1 else "attn_ctx",
    )(*args)


def _log_sigmoid(x):
    return jnp.minimum(x, 0.0) - jnp.log(1.0 + jnp.exp(-jnp.abs(x)))


GATE_HEADS = 2


def _mlstm_kernel(has_init, nc, hps, u_ref, v_ref, o_ref, gt_ref, cw_ref, cb_ref, wqt_ref, wk_ref, go_ref,
                  *rest):
    if has_init:
        c0_ref, n0_ref, m0_ref, out_ref, qt_s, k_s, vt_s, ht_f, ht_b, acol_s, row_s, st_s = rest
    else:
        out_ref, cfin_ref, sfin_ref, qt_s, k_s, vt_s, ht_f, ht_b, acol_s, row_s, st_s = rest
    t = nc * ML_CHUNK
    sq = (ML_CHUNK, ML_CHUNK)
    w2 = hps * LANES

    rows = lax.broadcasted_iota(jnp.int32, (t, LANES), 0)
    r_i = lax.broadcasted_iota(jnp.int32, sq, 0)
    c_i = lax.broadcasted_iota(jnp.int32, sq, 1)
    lower = c_i <= r_i
    upper = c_i >= r_i
    eye = jnp.where(c_i == r_i, 1.0, 0.0).astype(BF16)
    tri_lo = jnp.where(lower, 1.0, 0.0).astype(BF16)
    tri_up = jnp.where(upper, 1.0, 0.0).astype(BF16)
    row8 = lax.broadcasted_iota(jnp.int32, (8, ML_CHUNK), 0)
    is_f = lax.broadcasted_iota(jnp.int32, (8, t), 0) % 4 >= 2
    lg_all = []
    for pb in range(hps // GATE_HEADS):
        gates = gt_ref[pb]
        lg_all.append(jnp.where(is_f, _log_sigmoid(gates), gates))

    def gate_prep(pb, c):
        sl = slice(c * ML_CHUNK, (c + 1) * ML_CHUNK)
        lg = lg_all[pb][:, sl]
        h1, h2, h3 = _split3(lg)
        cs = jnp.where(row8 < 4, _dot(h1, tri_up) + _dot(h2, tri_up) + _dot(h3, tri_up),
                       _dot(h1, tri_lo) + _dot(h2, tri_lo) + _dot(h3, tri_lo)) * LOG2E
        row_s[pb, :, sl] = cs
        a = lg * LOG2E - pltpu.roll(cs, 6, 0)
        for jj in range(GATE_HEADS):
            for d in range(2):
                r = 4 * d + jj
                acol_s[2 * (GATE_HEADS * pb + jj) + d, sl, :] = jnp.broadcast_to(a[r:r + 1, :], sq).T

    gate_units = [(pb, c) for pb in range(hps // GATE_HEADS) for c in range(nc)]
    share = -(-len(gate_units) // hps)
    for j in range(hps):
        hs = slice(LANES * j, LANES * (j + 1))
        u = u_ref[:, hs]
        up = jnp.where(rows == 0, 0.0, pltpu.roll(u, 1, 0))
        un = jnp.where(rows == t - 1, 0.0, pltpu.roll(u, t - 1, 0))
        uc = _silu(cw_ref[0:1, hs] * up + cw_ref[1:2, hs] * u + cw_ref[2:3, hs] * un + cb_ref[:, hs]).astype(BF16)
        for pb, c in gate_units[j * share:(j + 1) * share]:
            gate_prep(pb, c)
        vt_s[hs, :] = _dot_nt(eye, v_ref[:, hs]).astype(BF16)
        qt_s[hs, :] = _dot_nt(wqt_ref[j], uc).astype(BF16)
        k_s[:, hs] = _dot(uc, wk_ref[j]) * ML_K_SCALE

    for c in range(nc):
        sl = slice(c * ML_CHUNK, (c + 1) * ML_CHUNK)
        for j in range(hps):
            hs = slice(LANES * j, LANES * (j + 1))
            st_s[j, sl, :] = _dot(k_s[sl, hs].astype(BF16), qt_s[hs, sl])

    def chunk(j, d, c0, ct, n, m2):
        hs = slice(LANES * j, LANES * (j + 1))
        cs_ = pl.ds(c0, ML_CHUNK)
        last = 0 if d else ML_CHUNK - 1
        qt = qt_s[hs, cs_]
        k = k_s[cs_, hs]
        vt = vt_s[hs, cs_]
        acol = acol_s[2 * j + d, cs_, :]
        r = 2 + 4 * d + j % GATE_HEADS
        bc = row_s[j // GATE_HEADS, r:r + 1, cs_]
        g = bc + m2
        dlog = jnp.where(lower if d else upper, acol + bc, -jnp.inf)
        m_t = jnp.maximum(g, jnp.max(dlog, axis=0, keepdims=True))
        w = jnp.exp2(dlog - m_t)
        inter = jnp.exp2(g - m_t)
        st = st_s[j, cs_, :] * w
        qn = _dot(jnp.broadcast_to(n, (8, LANES)).astype(BF16), qt)[0:1, :]
        den = jnp.sum(st, axis=0, keepdims=True) + inter * qn
        numt = _dot(vt, st.astype(BF16)) + inter * _dot(ct.astype(BF16), qt)
        ht = numt * (1.0 / jnp.maximum(jnp.abs(den), jnp.exp2(-m_t)))
        b_last = jnp.broadcast_to(bc[:, last:last + 1], (1, LANES))
        m_new = jnp.broadcast_to(m_t[:, last:last + 1], (1, LANES))
        kw = k * jnp.exp2(acol + (b_last - m_new))
        decay = jnp.exp2(m2 + (b_last - m_new))
        ct_new = decay * ct + _dot(vt, kw.astype(BF16))
        n_new = decay * n + jnp.sum(kw, axis=0, keepdims=True)
        return ht, ct_new, n_new, m_new

    def step(i, carry):
        cf = i * ML_CHUNK
        cb = (nc - 1 - i) * ML_CHUNK
        if not isinstance(i, int):
            cf = pl.multiple_of(cf, ML_CHUNK)
            cb = pl.multiple_of(cb, ML_CHUNK)
        new = []
        for j in range(hps):
            hs = slice(LANES * j, LANES * (j + 1))
            for d, (c0, ht_s) in enumerate(((cf, ht_f), (cb, ht_b))):
                ht, ct, n, m2 = chunk(j, d, c0, *carry[2 * j + d])
                ht_s[hs, pl.ds(c0, ML_CHUNK)] = ht
                new.append((ct, n, m2))
        return tuple(new)

    if has_init:
        init = tuple((c0_ref[d, j].T, n0_ref[j, d:d + 1, :], m0_ref[j, d:d + 1, :] * LOG2E)
                     for j in range(hps) for d in range(2))
    else:
        init = tuple((jnp.zeros(sq, F32), jnp.zeros((1, LANES), F32), jnp.zeros((1, LANES), F32))
                     for j in range(hps) for d in range(2))
    if nc <= 16:
        carry = init
        for i in range(nc):
            carry = step(i, carry)
    else:
        carry = lax.fori_loop(0, nc, step, init, unroll=4)

    for j in range(hps):
        hs = slice(LANES * j, LANES * (j + 1))
        hsum = (ht_f[hs, :] + ht_b[hs, :]).T
        hn = _rms(hsum, go_ref[:, hs])
        out_ref[:, hs] = (hn * jax.nn.sigmoid(o_ref[:, hs])).astype(out_ref.dtype)

    if not has_init:
        for j in range(hps):
            for d in range(2):
                ct, n, m2 = carry[2 * j + d]
                cfin_ref[d, j] = ct.T
            nf, nb = carry[2 * j][1], carry[2 * j + 1][1]
            mf, mb = carry[2 * j][2], carry[2 * j + 1][2]
            sfin_ref[j] = jnp.concatenate(
                [nf, nb, mf * (1.0 / LOG2E), mb * (1.0 / LOG2E), jnp.zeros((4, LANES), F32)], axis=0)


def _mlstm(u, v_ml, o_ml, gates, l, wts, init):
    b, t, _ = u.shape
    nc = t // ML_CHUNK
    has_init = init is not None
    hps = ML_HEADS if t <= 4 * ML_CHUNK else GATE_HEADS
    w2 = hps * LANES
    pair_blk = lambda i, p: (i, 0, p)
    in_specs = [
        pl.BlockSpec((None, t, w2), pair_blk),
        pl.BlockSpec((None, t, w2), pair_blk),
        pl.BlockSpec((None, t, w2), pair_blk),
        pl.BlockSpec((hps // GATE_HEADS, 8, t), lambda i, p: (p, 0, i)),
        pl.BlockSpec((None, 3, w2), lambda i, p: (l, 0, p)),
        pl.BlockSpec((None, 1, w2), lambda i, p: (l, 0, p)),
        pl.BlockSpec((None, hps, ML_DH, ML_DH), lambda i, p: (l, p, 0, 0)),
        pl.BlockSpec((None, hps, ML_DH, ML_DH), lambda i, p: (l, p, 0, 0)),
        pl.BlockSpec((None, 1, w2), lambda i, p: (l, 0, p)),
    ]
    args = [u, v_ml, o_ml, gates, wts["w_ml_conv"], wts["b_ml_conv"], wts["w_ml_qt"], wts["w_ml_k"],
            wts["g_ml_out"]]
    out_specs = [pl.BlockSpec((None, t, w2), pair_blk)]
    out_shape = [jax.ShapeDtypeStruct((b, t, ML_WIDTH), BF16)]
    state_c = pl.BlockSpec((None, 2, hps, ML_DH, ML_DH), lambda i, p: (i, 0, p, 0, 0))
    state_s = pl.BlockSpec((None, hps, 8, LANES), lambda i, p: (i, p, 0, 0))
    if has_init:
        in_specs += [pl.BlockSpec((None, None, 2, hps, ML_DH, ML_DH), lambda i, p: (i, l, 0, p, 0, 0)),
                     pl.BlockSpec((None, None, hps, 8, LANES), lambda i, p: (i, l, p, 0, 0)),
                     pl.BlockSpec((None, None, hps, 8, LANES), lambda i, p: (i, l, p, 0, 0))]
        args += list(init)
    else:
        out_specs += [state_c, state_s]
        out_shape += [
            jax.ShapeDtypeStruct((b, 2, ML_HEADS, ML_DH, ML_DH), F32),
            jax.ShapeDtypeStruct((b, ML_HEADS, 8, LANES), F32),
        ]
    return pl.pallas_call(
        functools.partial(_mlstm_kernel, has_init, nc, hps),
        grid=(b, ML_HEADS // hps),
        in_specs=in_specs,
        out_specs=out_specs,
        out_shape=out_shape,
        scratch_shapes=[
            pltpu.VMEM((w2, t), BF16),
            pltpu.VMEM((t, w2), F32),
            pltpu.VMEM((w2, t), BF16),
            pltpu.VMEM((w2, t), F32),
            pltpu.VMEM((w2, t), F32),
            pltpu.VMEM((2 * hps, t, LANES), F32),
            pltpu.VMEM((hps // GATE_HEADS, 8, t), F32),
            pltpu.VMEM((hps, t, LANES), F32),
        ],
        compiler_params=_cparams(("parallel", "parallel")),
        name="mlstm_lat" if has_init else "mlstm_ctx",
    )(*args)


def _ffn_kernel(seq, tm, final, *refs):
    x_refs, oa_refs, ob_refs, oc_refs = refs[0:3], refs[3:6], refs[6:9], refs[9:12]
    (mod_ref, g2_ref, wout_ref, wup_ref, cw_ref, cb_ref, wdn_ref, gf_ref, y_ref,
     oext_s, yext_s, gext_s, h_s) = refs[12:]
    i = pl.program_id(0)
    a_w = MLA_HEADS * MLA_V
    ext = ((0, HALO), (HALO, tm), (HALO + tm, HALO))
    for (r0, n), k in zip(ext, (1, 0, 2)):
        oext_s[r0:r0 + n, 0:a_w] = oa_refs[k][...]
        oext_s[r0:r0 + n, a_w:a_w + ML_WIDTH] = ob_refs[k][...]
        oext_s[r0:r0 + n, a_w + ML_WIDTH:] = oc_refs[k][...]
    mix = _dot(oext_s[...], wout_ref[...])
    gate1 = mod_ref[2:3, :]
    x = x_refs[0][...] + gate1 * mix[HALO:HALO + tm, :]
    xp = x_refs[1][...] + gate1 * mix[0:HALO, :]
    xn = x_refs[2][...] + gate1 * mix[HALO + tm:, :]

    g2 = g2_ref[...]
    shift = mod_ref[3:4, :]
    scale = mod_ref[4:5, :]
    spans = tm > seq
    if spans:
        keep_p = keep_n = 1.0
        tok = lax.broadcasted_iota(jnp.int32, (tm, FF_CHUNK), 0) % seq
        first, last = tok == 0, tok == seq - 1
    else:
        keep_p = jnp.where((i * tm) % seq == 0, 0.0, 1.0)
        keep_n = jnp.where(((i + 1) * tm) % seq == 0, 0.0, 1.0)
    yext_s[0:HALO, :] = (_adaln(xp, g2, shift, scale) * keep_p).astype(BF16)
    yext_s[HALO:HALO + tm, :] = _adaln(x, g2, shift, scale).astype(BF16)
    yext_s[HALO + tm:, :] = (_adaln(xn, g2, shift, scale) * keep_n).astype(BF16)

    def up(c):
        c0 = c * FF_CHUNK
        gext_s[c % 2] = _dot(yext_s[...], wup_ref[:, D_FF + c0:D_FF + c0 + FF_CHUNK])
        return _dot(yext_s[HALO:HALO + tm, :], wup_ref[:, c0:c0 + FF_CHUNK])

    nchunk = D_FF // FF_CHUNK
    a_next = up(0)
    for c in range(nchunk):
        c0 = c * FF_CHUNK
        a = a_next
        if c + 1 < nchunk:
            a_next = up(c + 1)
        gs = gext_s.at[c % 2]
        g_prev = gs[HALO - 1:HALO - 1 + tm, :]
        g_next = gs[HALO + 1:HALO + 1 + tm, :]
        if spans:
            g_prev = jnp.where(first, 0.0, g_prev)
            g_next = jnp.where(last, 0.0, g_next)
        g = (cw_ref[0:1, c0:c0 + FF_CHUNK] * g_prev
             + cw_ref[1:2, c0:c0 + FF_CHUNK] * gs[HALO:HALO + tm, :]
             + cw_ref[2:3, c0:c0 + FF_CHUNK] * g_next
             + cb_ref[:, c0:c0 + FF_CHUNK])
        h_s[:, c0:c0 + FF_CHUNK] = (_silu(g) * a).astype(BF16)
    y = x + mod_ref[5:6, :] * _dot(h_s[...], wdn_ref[...])
    if final:
        y = _rms(y, gf_ref[...])
    y_ref[...] = y


def _ffn(x2, o_a, o_b, o_c, mod_all, l, seq, is_ctx, wts, g_final, final):
    m = x2.shape[0]
    tm = TM
    assert tm % seq == 0 or seq % tm == 0
    nb = tm // HALO
    last_blk = m // HALO - 1
    row = lambda i: (i, 0)
    prev = lambda i: (jnp.maximum(i * nb - 1, 0), 0)
    nxt = lambda i: (jnp.minimum((i + 1) * nb, last_blk), 0)
    in_specs, args = [], []
    for a in (x2, o_a, o_b, o_c):
        w = a.shape[1]
        in_specs += [pl.BlockSpec((tm, w), row), pl.BlockSpec((HALO, w), prev), pl.BlockSpec((HALO, w), nxt)]
        args += [a, a, a]
    names = ["g_norm2", "w_out", "w_ff_up", "w_ff_conv", "b_ff_conv", "w_ff_down"]
    in_specs += ([_mod_spec(l, seq, tm, is_ctx)] + [_lspec(wts[n], l) for n in names]
                 + [pl.BlockSpec((1, D_MODEL), lambda i: (0, 0))])
    args += [mod_all] + [wts[n] for n in names] + [g_final]
    return pl.pallas_call(
        functools.partial(_ffn_kernel, seq, tm, final),
        grid=(m // tm,),
        in_specs=in_specs,
        out_specs=pl.BlockSpec((tm, D_MODEL), row),
        out_shape=jax.ShapeDtypeStruct((m, D_MODEL), F32),
        scratch_shapes=[pltpu.VMEM((tm + 2 * HALO, D_MODEL), BF16),
                        pltpu.VMEM((tm + 2 * HALO, D_MODEL), BF16),
                        pltpu.VMEM((2, tm + 2 * HALO, FF_CHUNK), F32),
                        pltpu.VMEM((tm, D_FF), BF16)],
        compiler_params=_cparams(("parallel",)),
        name="ffn_ctx" if is_ctx else "ffn_lat",
    )(*args)


def _rope_tables(t):
    pos = np.arange(t)
    rows = (pos // GRID_W).astype(np.float32)
    cols = (pos % GRID_W).astype(np.float32)

    def group(p, d):
        inv = (np.float32(ROPE_THETA) ** (-np.arange(0, d, 2, dtype=np.float32) / np.float32(d))).astype(np.float32)
        ang = (p[:, None] * inv[None, :]).astype(np.float32)
        cs, sn, z = np.cos(ang), np.sin(ang), np.zeros_like(ang)
        return (np.concatenate([cs, cs], 1), np.concatenate([-sn, z], 1), np.concatenate([z, sn], 1))

    gr, gc = group(rows, MLA_ROPE // 2), group(cols, MLA_ROPE // 2)
    ones = np.ones((t, MLA_NOPE), np.float32)
    zeros = np.zeros((t, MLA_NOPE), np.float32)
    pad1 = np.ones((t, LANES - MLA_NOPE - MLA_ROPE), np.float32)
    pad0 = np.zeros((t, LANES - MLA_NOPE - MLA_ROPE), np.float32)
    mla = (np.concatenate([ones, gr[0], gc[0], pad1], 1),
           np.concatenate([zeros, gr[1], gc[1], pad0], 1),
           np.concatenate([zeros, gr[2], gc[2], pad0], 1))
    gr, gc = group(rows, GQA_DH // 2), group(cols, GQA_DH // 2)
    gqa = tuple(np.concatenate([gr[k], gc[k], gr[k], gc[k]], 1) for k in range(3))
    return tuple(jnp.asarray(a, F32) for a in mla + gqa)


def _prep_weights(p):
    w_in = p["w_in"]
    nl = w_in.shape[0]
    qg = [w_in[:, :, 1968 + GQA_DH * h:1968 + GQA_DH * (h + 1)] for h in (0, 2, 1, 3)]
    gh = GATE_HEADS

    def regroup(g16):
        return [g16[..., kind * ML_HEADS + gh * pr:kind * ML_HEADS + gh * (pr + 1)]
                for pr in range(GATE_BLOCKS) for kind in range(4)]

    zeros = lambda a, n: jnp.zeros(a.shape[:-1] + (n,), F32)
    b16 = p["b_ml_gates"]
    b_gates = jnp.concatenate(regroup(b16) + [zeros(b16, LANES - 4 * ML_HEADS)], axis=-1)
    w_in_p = jnp.concatenate([
        w_in[:, :, 0:384],
        *regroup(w_in[:, :, 1952:1968]), zeros(w_in, MLA_NOPE - 4 * ML_HEADS),
        w_in[:, :, 384:416], zeros(w_in, LANES - MLA_NOPE - MLA_ROPE),
        w_in[:, :, 416:1952],
        *qg,
        w_in[:, :, 2224:2480],
    ], axis=2).astype(BF16)
    w_uq = p["w_mla_uq"].reshape(nl, MLA_Q_RANK, MLA_HEADS, MLA_NOPE + MLA_ROPE)
    w_uq = jnp.pad(w_uq, ((0, 0), (0, 0), (0, 0), (0, LANES - MLA_NOPE - MLA_ROPE)))
    w_ukv = p["w_mla_ukv"].reshape(nl, MLA_KV_RANK, MLA_HEADS, MLA_NOPE + MLA_V)
    w_uk = jnp.pad(w_ukv[..., :MLA_NOPE], ((0, 0), (0, 0), (0, 0), (0, LANES - MLA_NOPE)))
    w_ukv_p = jnp.concatenate([w_uk.reshape(nl, MLA_KV_RANK, -1),
                               w_ukv[..., MLA_NOPE:].reshape(nl, MLA_KV_RANK, -1)], axis=2)
    grp = np.arange(2 * LANES) // GQA_DH
    row = lambda a: a[:, None, :]
    return {
        "g_norm1": row(p["g_norm1"]),
        "g_norm2": row(p["g_norm2"]),
        "w_in": w_in_p,
        "g_mla_q": row(p["g_mla_q"]),
        "w_uq": w_uq.reshape(nl, MLA_Q_RANK, -1).astype(BF16),
        "g_mla_kv": row(p["g_mla_kv"]),
        "w_ukv": w_ukv_p.astype(BF16),
        "b_gates": row(b_gates),
        "g_gqa_q": row(jnp.tile(p["g_gqa_q"], (1, GQA_HEADS))),
        "g_gqa_k": row(jnp.tile(p["g_gqa_k"], (1, GQA_KV_HEADS))),
        "gsum": jnp.asarray(grp[:, None] == grp[None, :], BF16),
        "w_ml_conv": p["w_ml_conv"],
        "b_ml_conv": row(p["b_ml_conv"]),
        "w_ml_qt": jnp.swapaxes(p["w_ml_q"], 2, 3).astype(BF16),
        "w_ml_k": p["w_ml_k"].astype(BF16),
        "g_ml_out": row(p["g_ml_out"]),
        "w_out": p["w_out"].astype(BF16),
        "w_ff_up": p["w_ff_up"].astype(BF16),
        "w_ff_conv": p["w_ff_conv"],
        "b_ff_conv": row(p["b_ff_conv"]),
        "w_ff_down": p["w_ff_down"].astype(BF16),
    }


def _layer(x2, mod_all, l, seq, is_ctx, wts, tabs, cache, g_final, final):
    m = x2.shape[0]
    b = m // seq
    r3 = lambda a: a.reshape(b, seq, a.shape[-1])
    outs = _in_proj(x2, mod_all, l, seq, is_ctx, wts, tabs)
    q_m, k_m, v_m, u, v_ml, o_ml, gates, q_g, k_g, v_g = outs[:10]
    segs_a = [(r3(k_m), r3(v_m), None)]
    segs_c = [(r3(k_g), r3(v_g), None)]
    init = None
    if not is_ctx:
        ckv_c, kr_pad, kg_c, vg_c, ct0, n0p, m0p = cache
        kc, vc = _cache_kv(ckv_c, kr_pad, l, wts["w_ukv"])
        segs_a.append((kc, vc, None))
        segs_c.append((kg_c, vg_c, l))
        init = (ct0, n0p, m0p)
    o_a, o_c = _attention(r3(q_m), r3(q_g), segs_a, segs_c)
    ml = _mlstm(r3(u), r3(v_ml), r3(o_ml), gates, l, wts, init)
    x2 = _ffn(x2, o_a.reshape(m, -1), ml[0].reshape(m, -1), o_c.reshape(m, -1), mod_all, l, seq, is_ctx, wts,
              g_final, final)
    state = None
    if is_ctx:
        ckv_n, kr_raw = outs[10:12]
        sfin = ml[2]
        state = (
            r3(ckv_n),
            r3(kr_raw)[:, :, MLA_NOPE:MLA_NOPE + MLA_ROPE],
            k_g.reshape(b, seq, GQA_KV_HEADS, GQA_DH),
            v_g.reshape(b, seq, GQA_KV_HEADS, GQA_DH),
            ml[1],
            jnp.swapaxes(sfin[:, :, 0:2, :], 1, 2),
            jnp.swapaxes(sfin[:, :, 2:4, 0], 1, 2),
        )
    return x2, state


def kernel(x_prompt, x_sample, cache_mla_ckv, cache_mla_krope, cache_gqa_k, cache_gqa_v, state_mlstm_C, state_mlstm_n, state_mlstm_m, c, c_ctx, w_ada, b_ada, g_norm1, g_norm2, w_in, g_mla_q, w_mla_uq, g_mla_kv, w_mla_ukv, w_ml_conv, b_ml_conv, w_ml_q, w_ml_k, b_ml_gates, g_ml_out, g_gqa_q, g_gqa_k, w_out, w_ff_up, w_ff_conv, b_ff_conv, w_ff_down, g_final):
    params = {
        "g_norm1": g_norm1, "g_norm2": g_norm2, "w_in": w_in, "g_mla_q": g_mla_q, "w_mla_uq": w_mla_uq,
        "g_mla_kv": g_mla_kv, "w_mla_ukv": w_mla_ukv, "w_ml_conv": w_ml_conv, "b_ml_conv": b_ml_conv,
        "w_ml_q": w_ml_q, "w_ml_k": w_ml_k, "b_ml_gates": b_ml_gates, "g_ml_out": g_ml_out,
        "g_gqa_q": g_gqa_q, "g_gqa_k": g_gqa_k, "w_out": w_out, "w_ff_up": w_ff_up, "w_ff_conv": w_ff_conv,
        "b_ff_conv": b_ff_conv, "w_ff_down": w_ff_down,
    }
    depth = w_in.shape[0]
    bp, sp, _ = x_prompt.shape
    bs, ss, _ = x_sample.shape
    assert bs + 1 <= 16 and sp % ML_CHUNK == 0 and ss % TM == 0 and (bp * sp) % TM == 0

    cvec = jnp.concatenate([c_ctx[None, :], c, jnp.zeros((16 - 1 - bs, D_MODEL), F32)], axis=0)
    mod_all = _modulation(cvec, w_ada, b_ada).reshape(depth, 16, 6, D_MODEL)
    tabs = _rope_tables(ss)
    gf = g_final[None, :]
    wts = _prep_weights(params)

    tc = cache_mla_ckv.shape[2]
    pad_rows = ((0, 0), (0, 0), (0, 0), (0, 8 - 2), (0, 0))
    cache = (
        cache_mla_ckv,
        jnp.pad(cache_mla_krope, ((0, 0), (0, 0), (0, 0), (MLA_NOPE, LANES - MLA_NOPE - MLA_ROPE))),
        cache_gqa_k.reshape(bs, depth, tc, GQA_KV_HEADS * GQA_DH),
        cache_gqa_v.reshape(bs, depth, tc, GQA_KV_HEADS * GQA_DH),
        state_mlstm_C,
        jnp.pad(jnp.swapaxes(state_mlstm_n, 2, 3), pad_rows),
        jnp.pad(jnp.broadcast_to(jnp.swapaxes(state_mlstm_m, 2, 3)[..., None], (bs, depth, ML_HEADS, 2, LANES)),
                pad_rows),
    )

    xp = x_prompt.reshape(bp * sp, D_MODEL)
    xs = x_sample.reshape(bs * ss, D_MODEL)
    states = []
    for l in range(depth):
        final = l == depth - 1
        xp, st = _layer(xp, mod_all, l, sp, True, wts, None, None, gf, final)
        states.append(st)
        xs, _ = _layer(xs, mod_all, l, ss, False, wts, tabs, cache, gf, final)
    new_state = tuple(jnp.stack([st[k] for st in states], axis=1) for k in range(7))
    return (xp.reshape(bp, sp, D_MODEL), xs.reshape(bs, ss, D_MODEL)) + new_state
```

```python
import functools

import jax
import jax.numpy as jnp
import numpy as np
from jax import lax
from jax.experimental import pallas as pl
from jax.experimental.pallas import tpu as pltpu

F32 = jnp.float32
BF16 = jnp.bfloat16

D_MODEL = 1024
GRID_W = 64
ROPE_THETA = 10000.0
EPS = 1e-6
MLA_HEADS = 4
MLA_NOPE = 64
MLA_ROPE = 32
MLA_V = 64
MLA_Q_RANK = 256
MLA_KV_RANK = 128
ML_HEADS = 4
ML_DH = 128
ML_WIDTH = ML_HEADS * ML_DH
ML_CHUNK = 128
GQA_HEADS = 4
GQA_KV_HEADS = 2
GQA_DH = 64
D_FF = 2816
MLA_SCALE = (MLA_NOPE + MLA_ROPE) ** -0.5
GQA_SCALE = GQA_DH ** -0.5
ML_K_SCALE = ML_DH ** -0.5
LOG2E = 1.4426950408889634

LANES = 128
HALO = 16
VMEM_LIMIT = 52 * 1024 * 1024

C_CQ, C_CKV, C_X, C_U, C_V, C_O, C_QG, C_KG, C_VG, IN_COLS_P = (
    0, 256, 384, 512, 1024, 1536, 2048, 2304, 2432, 2560)
GATE_BLOCKS = 2
FF_CHUNK = 256
TM = 512
TQ = 512


def _cparams(sem):
    return pltpu.CompilerParams(dimension_semantics=sem, vmem_limit_bytes=VMEM_LIMIT)


def _lspec(arr, l):
    nd = arr.ndim - 1
    return pl.BlockSpec((None,) + arr.shape[1:], lambda *_: (l,) + (0,) * nd, pipeline_mode=pl.Buffered(1))


def _dot(a, b):
    return jnp.dot(a, b, preferred_element_type=F32)


def _dot_nt(a, b):
    return lax.dot_general(a, b, (((1,), (1,)), ((), ())), preferred_element_type=F32)


def _dot_tn(a, b):
    return lax.dot_general(a, b, (((0,), (0,)), ((), ())), preferred_element_type=F32)


def _rms(x, g):
    return (x * lax.rsqrt(jnp.mean(x * x, axis=-1, keepdims=True) + EPS)) * g


def _silu(x):
    return x * jax.nn.sigmoid(x)


def _adaln(x, g, shift, scale):
    return _rms(x, g) * (1.0 + scale) + shift


def _split2(x):
    hi = x.astype(BF16)
    lo = (x - hi.astype(F32)).astype(BF16)
    return hi, lo


def _split3(x):
    h1 = x.astype(BF16)
    r1 = x - h1.astype(F32)
    h2 = r1.astype(BF16)
    h3 = (r1 - h2.astype(F32)).astype(BF16)
    return h1, h2, h3


def _rope(x, cos, sa, sb, half):
    w = x.shape[-1]
    return x * cos + pltpu.roll(x, w - half, 1) * sa + pltpu.roll(x, half, 1) * sb


def _group_rms(x, gsum, g):
    hi, lo = _split2(x * x)
    ss = _dot(hi, gsum) + _dot(lo, gsum)
    return (x * lax.rsqrt(ss * (1.0 / GQA_DH) + EPS)) * g


def _mod_kernel(c_ref, w_ref, b_ref, o_ref):
    a = _silu(c_ref[...]).astype(BF16)
    o_ref[...] = _dot(a, w_ref[...].astype(BF16)) + b_ref[...]


def _modulation(cvec, w_ada, b_ada):
    nl = w_ada.shape[0]
    tn = 1536
    return pl.pallas_call(
        _mod_kernel,
        grid=(nl, 6 * D_MODEL // tn),
        in_specs=[
            pl.BlockSpec((16, D_MODEL), lambda l, j: (0, 0)),
            pl.BlockSpec((None, D_MODEL, tn), lambda l, j: (l, 0, j)),
            pl.BlockSpec((None, 1, tn), lambda l, j: (l, 0, j)),
        ],
        out_specs=pl.BlockSpec((None, 16, tn), lambda l, j: (l, 0, j)),
        out_shape=jax.ShapeDtypeStruct((nl, 16, 6 * D_MODEL), F32),
        compiler_params=_cparams(("arbitrary", "arbitrary")),
        name="modulation",
    )(cvec, w_ada, b_ada.reshape(nl, 1, 6 * D_MODEL))


def _in_proj_kernel(is_ctx, x_ref, mod_ref, g1_ref, win_ref, gq_ref, wuq_ref, gkv_ref, wukv_ref,
                    bg_ref, ggq_ref, ggk_ref, gsum_ref, *rest):
    if is_ctx:
        (qm_ref, km_ref, vm_ref, u_ref, vml_ref, oml_ref, gt_ref, qg_ref, kg_ref, vg_ref,
         ckv_ref, kr_ref) = rest
    else:
        (cm_ref, sam_ref, sbm_ref, cg_ref, sag_ref, sbg_ref,
         qm_ref, km_ref, vm_ref, u_ref, vml_ref, oml_ref, gt_ref, qg_ref, kg_ref, vg_ref) = rest

    def modulated(s):
        rs = slice(s * TM, (s + 1) * TM)
        return _adaln(x_ref[rs, :], g1_ref[...], mod_ref[0:1, :], mod_ref[1:2, :]).astype(BF16)

    nsub = x_ref.shape[0] // TM
    lane = lax.broadcasted_iota(jnp.int32, (TM, LANES), 1)
    kr_lanes = jnp.logical_and(lane >= MLA_NOPE, lane < MLA_NOPE + MLA_ROPE)
    y_next = modulated(0)
    for s in range(nsub):
        rs = slice(s * TM, (s + 1) * TM)
        y = y_next

        def proj(c0, width, y=y):
            return _dot(y, win_ref[:, c0:c0 + width])

        if not is_ctx:
            rope_m = (cm_ref[rs, :], sam_ref[rs, :], sbm_ref[rs, :], 8)
            rope_g = (cg_ref[rs, :], sag_ref[rs, :], sbg_ref[rs, :], 16)

        z_cq = proj(C_CQ, MLA_Q_RANK)
        ckv_kr = proj(C_CKV, MLA_KV_RANK + LANES)
        if s + 1 < nsub:
            y_next = modulated(s + 1)

        cqn = _rms(z_cq, gq_ref[...]).astype(BF16)
        z_qg = proj(C_QG, GQA_HEADS * GQA_DH)
        qz = _dot(cqn, wuq_ref[...])

        ckvn = _rms(ckv_kr[:, 0:MLA_KV_RANK], gkv_ref[...])
        x_blk = ckv_kr[:, MLA_KV_RANK:]
        kr = jnp.where(kr_lanes, x_blk, 0.0)
        if is_ctx:
            ckv_ref[rs, :] = ckvn
            kr_ref[rs, :] = kr
        else:
            kr = _rope(kr, *rope_m)
        kv_g = proj(C_KG, 2 * LANES)
        kvz = _dot(ckvn.astype(BF16), wukv_ref[...])

        z_u = proj(C_U, ML_WIDTH)
        for h in range(MLA_HEADS):
            blk = qz[:, LANES * h:LANES * (h + 1)]
            if not is_ctx:
                blk = _rope(blk, *rope_m)
            qm_ref[rs, LANES * h:LANES * (h + 1)] = (blk * (MLA_SCALE * LOG2E)).astype(qm_ref.dtype)

        z_v = proj(C_V, ML_WIDTH)
        for h in range(MLA_HEADS):
            km_ref[rs, LANES * h:LANES * (h + 1)] = (kvz[:, LANES * h:LANES * (h + 1)] + kr).astype(km_ref.dtype)
        vm_ref[rs, :] = kvz[:, MLA_HEADS * LANES:].astype(vm_ref.dtype)
        u_ref[rs, :] = z_u.astype(u_ref.dtype)

        z_o = proj(C_O, ML_WIDTH)
        qg = _group_rms(z_qg, gsum_ref[...], ggq_ref[...])
        kg = _group_rms(kv_g[:, 0:LANES], gsum_ref[0:LANES, 0:LANES], ggk_ref[...])
        vml_ref[rs, :] = z_v.astype(vml_ref.dtype)
        if not is_ctx:
            qg = jnp.concatenate([_rope(qg[:, 0:LANES], *rope_g), _rope(qg[:, LANES:], *rope_g)], axis=1)
            kg = _rope(kg, *rope_g)
        qg_ref[rs, :] = (qg * (GQA_SCALE * LOG2E)).astype(qg_ref.dtype)
        kg_ref[rs, :] = kg.astype(kg_ref.dtype)
        vg_ref[rs, :] = kv_g[:, LANES:].astype(vg_ref.dtype)

        oml_ref[rs, :] = z_o.astype(oml_ref.dtype)
        gates_t = (x_blk + bg_ref[...]).T
        for pr in range(GATE_BLOCKS):
            gt_ref[pr, :, rs] = gates_t[8 * pr:8 * (pr + 1), :]


def _mod_spec(l, seq, tm, is_ctx):
    if is_ctx:
        return pl.BlockSpec((None, None, 6, D_MODEL), lambda i: (l, 0, 0, 0))
    return pl.BlockSpec((None, None, 6, D_MODEL), lambda i: (l, 1 + (i * tm) // seq, 0, 0))


def _in_proj(x2, mod_all, l, seq, is_ctx, wts, tabs):
    m = x2.shape[0]
    tm = 2 * TM
    assert m % tm == 0 and (is_ctx or seq % tm == 0)
    nt = m // tm
    row = lambda i: (i, 0)
    names = ["g_norm1", "w_in", "g_mla_q", "w_uq", "g_mla_kv", "w_ukv", "b_gates", "g_gqa_q", "g_gqa_k"]
    in_specs = ([pl.BlockSpec((tm, D_MODEL), row), _mod_spec(l, seq, tm, is_ctx)]
                + [_lspec(wts[n], l) for n in names]
                + [pl.BlockSpec((2 * LANES, 2 * LANES), lambda i: (0, 0))])
    args = [x2, mod_all] + [wts[n] for n in names] + [wts["gsum"]]
    if not is_ctx:
        tpos = seq // tm
        in_specs += [pl.BlockSpec((tm, LANES), lambda i: (i % tpos, 0))] * 6
        args += list(tabs)
    act = F32 if is_ctx else BF16
    outs = [
        (MLA_HEADS * LANES, BF16),
        (MLA_HEADS * LANES, BF16),
        (MLA_HEADS * MLA_V, BF16),
        (ML_WIDTH, F32),
        (ML_WIDTH, BF16),
        (ML_WIDTH, F32),
        (None, F32),
        (GQA_HEADS * GQA_DH, BF16),
        (LANES, act),
        (LANES, act),
    ]
    if is_ctx:
        outs += [(MLA_KV_RANK, F32), (LANES, F32)]
    return pl.pallas_call(
        functools.partial(_in_proj_kernel, is_ctx),
        grid=(nt,),
        in_specs=in_specs,
        out_specs=[pl.BlockSpec((GATE_BLOCKS, 8, tm), lambda i: (0, 0, i)) if w is None
                   else pl.BlockSpec((tm, w), row) for w, _ in outs],
        out_shape=[jax.ShapeDtypeStruct((GATE_BLOCKS, 8, m) if w is None else (m, w), dt)
                   for w, dt in outs],
        compiler_params=_cparams(("parallel",)),
        name="in_proj_ctx" if is_ctx else "in_proj_lat",
    )(*args)


def _cache_kv_kernel(ckv_ref, kr_ref, wukv_ref, k_ref, v_ref):
    for i in range(ckv_ref.shape[0]):
        kvz = _dot(ckv_ref[i].astype(BF16), wukv_ref[...])
        kr = kr_ref[i]
        for h in range(MLA_HEADS):
            k_ref[i, :, LANES * h:LANES * (h + 1)] = (kvz[:, LANES * h:LANES * (h + 1)] + kr).astype(k_ref.dtype)
        v_ref[i] = kvz[:, MLA_HEADS * LANES:].astype(v_ref.dtype)


def _cache_kv(ckv, kr_pad, l, w_ukv):
    b, _, tc, _ = ckv.shape
    cache_blk = pl.BlockSpec((b, None, tc, LANES), lambda i: (0, l, 0, 0))
    return pl.pallas_call(
        _cache_kv_kernel,
        grid=(1,),
        in_specs=[cache_blk, cache_blk, _lspec(w_ukv, l)],
        out_specs=[pl.BlockSpec((b, tc, MLA_HEADS * LANES), lambda i: (0, 0, 0)),
                   pl.BlockSpec((b, tc, MLA_HEADS * MLA_V), lambda i: (0, 0, 0))],
        out_shape=[jax.ShapeDtypeStruct((b, tc, MLA_HEADS * LANES), BF16),
                   jax.ShapeDtypeStruct((b, tc, MLA_HEADS * MLA_V), BF16)],
        compiler_params=_cparams(("arbitrary",)),
        name="cache_kv",
    )(ckv, kr_pad, w_ukv)


def _attn_kernel(nseg, qa_ref, qc_ref, *refs):
    kva, kvc = refs[:2 * nseg], refs[2 * nseg:4 * nseg]
    oa_ref, oc_ref = refs[4 * nseg:]
    nb, tq = qa_ref.shape[0], qa_ref.shape[1]
    lo = lax.broadcasted_iota(jnp.int32, (tq, LANES), 1) < (LANES // 2)

    def qk(unit):
        bi, is_mla, h = unit
        kv = kva if is_mla else kvc
        if is_mla:
            qh = qa_ref[bi, :, LANES * h:LANES * (h + 1)]
            ksl = slice(LANES * h, LANES * (h + 1))
        else:
            blk = qc_ref[bi, :, LANES * (h % 2):LANES * (h % 2 + 1)].astype(F32)
            qh = jnp.where(lo if h < 2 else jnp.logical_not(lo), blk, 0.0).astype(BF16)
            ksl = slice(0, LANES)
        return [_dot_nt(qh, kv[2 * j][bi, :, ksl].astype(BF16)) for j in range(nseg)]

    def softmax(ss):
        mx = jnp.max(ss[0], axis=-1, keepdims=True)
        for s in ss[1:]:
            mx = jnp.maximum(mx, jnp.max(s, axis=-1, keepdims=True))
        es = [jnp.exp2(s - mx) for s in ss]
        den = es[0].sum(axis=-1, keepdims=True)
        for e in es[1:]:
            den = den + e.sum(axis=-1, keepdims=True)
        return [e.astype(BF16) for e in es], den

    def pv(unit, es, den):
        bi, is_mla, h = unit
        kv = kva if is_mla else kvc
        vsl = slice(LANES * (h // 2), LANES * (h // 2 + 1)) if is_mla else slice(0, LANES)
        acc = None
        for j, e in enumerate(es):
            a = _dot(e, kv[2 * j + 1][bi, :, vsl].astype(BF16))
            acc = a if acc is None else acc + a
        return acc / den

    units = [(bi, is_mla, h) for bi in range(nb) for is_mla in (True, False) for h in range(4)]
    out = {}
    ahead = qk(units[0])
    for n, unit in enumerate(units):
        ss = ahead
        if n + 1 < len(units):
            ahead = qk(units[n + 1])
        out[unit] = pv(unit, *softmax(ss))
    half = LANES // 2
    for bi in range(nb):
        a = [out[(bi, True, h)] for h in range(4)]
        c = [out[(bi, False, h)] for h in range(4)]
        oa_ref[bi, :, 0:LANES] = jnp.where(lo, a[0], a[1]).astype(oa_ref.dtype)
        oa_ref[bi, :, LANES:] = jnp.where(lo, a[2], a[3]).astype(oa_ref.dtype)
        oc_ref[bi, :, 0:LANES] = jnp.where(lo, c[0], pltpu.roll(c[1], half, 1)).astype(oc_ref.dtype)
        oc_ref[bi, :, LANES:] = jnp.where(lo, pltpu.roll(c[2], half, 1), c[3]).astype(oc_ref.dtype)


def _attention(q_a, q_c, segs_a, segs_c):
    b, tq_all, _ = q_a.shape
    tq = min(TQ, tq_all)
    nb = max(1, min(b, 2 * TQ // tq_all))
    in_specs = [pl.BlockSpec((nb, tq, q_a.shape[2]), lambda i, j: (i, j, 0)),
                pl.BlockSpec((nb, tq, q_c.shape[2]), lambda i, j: (i, j, 0))]
    args = [q_a, q_c]
    for segs in (segs_a, segs_c):
        for k, v, layer in segs:
            for a in (k, v):
                if layer is None:
                    in_specs.append(pl.BlockSpec((nb,) + a.shape[1:], lambda i, j: (i, 0, 0)))
                else:
                    in_specs.append(pl.BlockSpec((nb, None) + a.shape[2:],
                                                 lambda i, j, layer=layer: (i, layer, 0, 0)))
                args.append(a)
    out_spec = pl.BlockSpec((nb, tq, 2 * LANES), lambda i, j: (i, j, 0))
    return pl.pallas_call(
        functools.partial(_attn_kernel, len(segs_a)),
        grid=(b // nb, tq_all // tq),
        in_specs=in_specs,
        out_specs=[out_spec, out_spec],
        out_shape=[jax.ShapeDtypeStruct((b, tq_all, 2 * LANES), BF16)] * 2,
        compiler_params=_cparams(("parallel", "parallel")),
        name="attn_lat" if len(segs_a) > 1 else "attn_ctx",
    )(*args)


def _log_sigmoid(x):
    return jnp.minimum(x, 0.0) - jnp.log(1.0 + jnp.exp(-jnp.abs(x)))


GATE_HEADS = 2


def _mlstm_kernel(has_init, nc, hps, u_ref, v_ref, o_ref, gt_ref, cw_ref, cb_ref, wqt_ref, wk_ref, go_ref,
                  *rest):
    if has_init:
        c0_ref, n0_ref, m0_ref, out_ref, qt_s, k_s, vt_s, ht_f, ht_b, acol_s, row_s, st_s = rest
    else:
        out_ref, cfin_ref, sfin_ref, qt_s, k_s, vt_s, ht_f, ht_b, acol_s, row_s, st_s = rest
    t = nc * ML_CHUNK
    sq = (ML_CHUNK, ML_CHUNK)
    w2 = hps * LANES

    rows = lax.broadcasted_iota(jnp.int32, (t, LANES), 0)
    r_i = lax.broadcasted_iota(jnp.int32, sq, 0)
    c_i = lax.broadcasted_iota(jnp.int32, sq, 1)
    lower = c_i <= r_i
    upper = c_i >= r_i
    eye = jnp.where(c_i == r_i, 1.0, 0.0).astype(BF16)
    tri_lo = jnp.where(lower, 1.0, 0.0).astype(BF16)
    tri_up = jnp.where(upper, 1.0, 0.0).astype(BF16)
    row8 = lax.broadcasted_iota(jnp.int32, (8, ML_CHUNK), 0)
    is_f = lax.broadcasted_iota(jnp.int32, (8, t), 0) % 4 >= 2
    lg_all = []
    for pb in range(hps // GATE_HEADS):
        gates = gt_ref[pb]
        lg_all.append(jnp.where(is_f, _log_sigmoid(gates), gates))

    def gate_prep(pb, c):
        sl = slice(c * ML_CHUNK, (c + 1) * ML_CHUNK)
        lg = lg_all[pb][:, sl]
        h1, h2, h3 = _split3(lg)
        cs = jnp.where(row8 < 4, _dot(h1, tri_up) + _dot(h2, tri_up) + _dot(h3, tri_up),
                       _dot(h1, tri_lo) + _dot(h2, tri_lo) + _dot(h3, tri_lo)) * LOG2E
        row_s[pb, :, sl] = cs
        a = lg * LOG2E - pltpu.roll(cs, 6, 0)
        for jj in range(GATE_HEADS):
            for d in range(2):
                r = 4 * d + jj
                acol_s[2 * (GATE_HEADS * pb + jj) + d, sl, :] = jnp.broadcast_to(a[r:r + 1, :], sq).T

    gate_units = [(pb, c) for pb in range(hps // GATE_HEADS) for c in range(nc)]
    share = -(-len(gate_units) // hps)
    for j in range(hps):
        hs = slice(LANES * j, LANES * (j + 1))
        u = u_ref[:, hs]
        up = jnp.where(rows == 0, 0.0, pltpu.roll(u, 1, 0))
        un = jnp.where(rows == t - 1, 0.0, pltpu.roll(u, t - 1, 0))
        uc = _silu(cw_ref[0:1, hs] * up + cw_ref[1:2, hs] * u + cw_ref[2:3, hs] * un + cb_ref[:, hs]).astype(BF16)
        for pb, c in gate_units[j * share:(j + 1) * share]:
            gate_prep(pb, c)
        vt_s[hs, :] = _dot_nt(eye, v_ref[:, hs]).astype(BF16)
        qt_s[hs, :] = _dot_nt(wqt_ref[j], uc).astype(BF16)
        k_s[:, hs] = _dot(uc, wk_ref[j]) * ML_K_SCALE

    for c in range(nc):
        sl = slice(c * ML_CHUNK, (c + 1) * ML_CHUNK)
        for j in range(hps):
            hs = slice(LANES * j, LANES * (j + 1))
            st_s[j, sl, :] = _dot(k_s[sl, hs].astype(BF16), qt_s[hs, sl])

    def chunk(j, d, c0, ct, n, m2):
        hs = slice(LANES * j, LANES * (j + 1))
        cs_ = pl.ds(c0, ML_CHUNK)
        last = 0 if d else ML_CHUNK - 1
        qt = qt_s[hs, cs_]
        k = k_s[cs_, hs]
        vt = vt_s[hs, cs_]
        acol = acol_s[2 * j + d, cs_, :]
        r = 2 + 4 * d + j % GATE_HEADS
        bc = row_s[j // GATE_HEADS, r:r + 1, cs_]
        g = bc + m2
        dlog = jnp.where(lower if d else upper, acol + bc, -jnp.inf)
        m_t = jnp.maximum(g, jnp.max(dlog, axis=0, keepdims=True))
        w = jnp.exp2(dlog - m_t)
        inter = jnp.exp2(g - m_t)
        st = st_s[j, cs_, :] * w
        qn = _dot(jnp.broadcast_to(n, (8, LANES)).astype(BF16), qt)[0:1, :]
        den = jnp.sum(st, axis=0, keepdims=True) + inter * qn
        numt = _dot(vt, st.astype(BF16)) + inter * _dot(ct.astype(BF16), qt)
        ht = numt * (1.0 / jnp.maximum(jnp.abs(den), jnp.exp2(-m_t)))
        b_last = jnp.broadcast_to(bc[:, last:last + 1], (1, LANES))
        m_new = jnp.broadcast_to(m_t[:, last:last + 1], (1, LANES))
        kw = k * jnp.exp2(acol + (b_last - m_new))
        decay = jnp.exp2(m2 + (b_last - m_new))
        ct_new = decay * ct + _dot(vt, kw.astype(BF16))
        n_new = decay * n + jnp.sum(kw, axis=0, keepdims=True)
        return ht, ct_new, n_new, m_new

    def step(i, carry):
        cf = i * ML_CHUNK
        cb = (nc - 1 - i) * ML_CHUNK
        if not isinstance(i, int):
            cf = pl.multiple_of(cf, ML_CHUNK)
            cb = pl.multiple_of(cb, ML_CHUNK)
        new = []
        for j in range(hps):
            hs = slice(LANES * j, LANES * (j + 1))
            for d, (c0, ht_s) in enumerate(((cf, ht_f), (cb, ht_b))):
                ht, ct, n, m2 = chunk(j, d, c0, *carry[2 * j + d])
                ht_s[hs, pl.ds(c0, ML_CHUNK)] = ht
                new.append((ct, n, m2))
        return tuple(new)

    if has_init:
        init = tuple((c0_ref[d, j].T, n0_ref[j, d:d + 1, :], m0_ref[j, d:d + 1, :] * LOG2E)
                     for j in range(hps) for d in range(2))
    else:
        init = tuple((jnp.zeros(sq, F32), jnp.zeros((1, LANES), F32), jnp.zeros((1, LANES), F32))
                     for j in range(hps) for d in range(2))
    if nc <= 16:
        carry = init
        for i in range(nc):
            carry = step(i, carry)
    else:
        carry = lax.fori_loop(0, nc, step, init, unroll=4)

    for j in range(hps):
        hs = slice(LANES * j, LANES * (j + 1))
        hsum = (ht_f[hs, :] + ht_b[hs, :]).T
        hn = _rms(hsum, go_ref[:, hs])
        out_ref[:, hs] = (hn * jax.nn.sigmoid(o_ref[:, hs])).astype(out_ref.dtype)

    if not has_init:
        for j in range(hps):
            for d in range(2):
                ct, n, m2 = carry[2 * j + d]
                cfin_ref[d, j] = ct.T
            nf, nb = carry[2 * j][1], carry[2 * j + 1][1]
            mf, mb = carry[2 * j][2], carry[2 * j + 1][2]
            sfin_ref[j] = jnp.concatenate(
                [nf, nb, mf * (1.0 / LOG2E), mb * (1.0 / LOG2E), jnp.zeros((4, LANES), F32)], axis=0)


def _mlstm(u, v_ml, o_ml, gates, l, wts, init):
    b, t, _ = u.shape
    nc = t // ML_CHUNK
    has_init = init is not None
    hps = ML_HEADS if t <= 4 * ML_CHUNK else GATE_HEADS
    w2 = hps * LANES
    pair_blk = lambda i, p: (i, 0, p)
    in_specs = [
        pl.BlockSpec((None, t, w2), pair_blk),
        pl.BlockSpec((None, t, w2), pair_blk),
        pl.BlockSpec((None, t, w2), pair_blk),
        pl.BlockSpec((hps // GATE_HEADS, 8, t), lambda i, p: (p, 0, i)),
        pl.BlockSpec((None, 3, w2), lambda i, p: (l, 0, p)),
        pl.BlockSpec((None, 1, w2), lambda i, p: (l, 0, p)),
        pl.BlockSpec((None, hps, ML_DH, ML_DH), lambda i, p: (l, p, 0, 0)),
        pl.BlockSpec((None, hps, ML_DH, ML_DH), lambda i, p: (l, p, 0, 0)),
        pl.BlockSpec((None, 1, w2), lambda i, p: (l, 0, p)),
    ]
    args = [u, v_ml, o_ml, gates, wts["w_ml_conv"], wts["b_ml_conv"], wts["w_ml_qt"], wts["w_ml_k"],
            wts["g_ml_out"]]
    out_specs = [pl.BlockSpec((None, t, w2), pair_blk)]
    out_shape = [jax.ShapeDtypeStruct((b, t, ML_WIDTH), BF16)]
    state_c = pl.BlockSpec((None, 2, hps, ML_DH, ML_DH), lambda i, p: (i, 0, p, 0, 0))
    state_s = pl.BlockSpec((None, hps, 8, LANES), lambda i, p: (i, p, 0, 0))
    if has_init:
        in_specs += [pl.BlockSpec((None, None, 2, hps, ML_DH, ML_DH), lambda i, p: (i, l, 0, p, 0, 0)),
                     pl.BlockSpec((None, None, hps, 8, LANES), lambda i, p: (i, l, p, 0, 0)),
                     pl.BlockSpec((None, None, hps, 8, LANES), lambda i, p: (i, l, p, 0, 0))]
        args += list(init)
    else:
        out_specs += [state_c, state_s]
        out_shape += [
            jax.ShapeDtypeStruct((b, 2, ML_HEADS, ML_DH, ML_DH), F32),
            jax.ShapeDtypeStruct((b, ML_HEADS, 8, LANES), F32),
        ]
    return pl.pallas_call(
        functools.partial(_mlstm_kernel, has_init, nc, hps),
        grid=(b, ML_HEADS // hps),
        in_specs=in_specs,
        out_specs=out_specs,
        out_shape=out_shape,
        scratch_shapes=[
            pltpu.VMEM((w2, t), BF16),
            pltpu.VMEM((t, w2), F32),
            pltpu.VMEM((w2, t), BF16),
            pltpu.VMEM((w2, t), F32),
            pltpu.VMEM((w2, t), F32),
            pltpu.VMEM((2 * hps, t, LANES), F32),
            pltpu.VMEM((hps // GATE_HEADS, 8, t), F32),
            pltpu.VMEM((hps, t, LANES), F32),
        ],
        compiler_params=_cparams(("parallel", "parallel")),
        name="mlstm_lat" if has_init else "mlstm_ctx",
    )(*args)


def _ffn_kernel(seq, tm, final, *refs):
    x_refs, oa_refs, ob_refs, oc_refs = refs[0:3], refs[3:6], refs[6:9], refs[9:12]
    (mod_ref, g2_ref, wout_ref, wup_ref, cw_ref, cb_ref, wdn_ref, gf_ref, y_ref,
     oext_s, yext_s, gext_s, h_s) = refs[12:]
    i = pl.program_id(0)
    a_w = MLA_HEADS * MLA_V
    ext = ((0, HALO), (HALO, tm), (HALO + tm, HALO))
    for (r0, n), k in zip(ext, (1, 0, 2)):
        oext_s[r0:r0 + n, 0:a_w] = oa_refs[k][...]
        oext_s[r0:r0 + n, a_w:a_w + ML_WIDTH] = ob_refs[k][...]
        oext_s[r0:r0 + n, a_w + ML_WIDTH:] = oc_refs[k][...]
    hm = tm // 2
    mix_a = _dot(oext_s[0:HALO + hm, :], wout_ref[...])
    mix_b = _dot(oext_s[HALO + hm:, :], wout_ref[...])
    gate1 = mod_ref[2:3, :]
    g2 = g2_ref[...]
    shift = mod_ref[3:4, :]
    scale = mod_ref[4:5, :]
    xp = x_refs[1][...] + gate1 * mix_a[0:HALO, :]
    x_lo = x_refs[0][0:hm, :] + gate1 * mix_a[HALO:, :]
    x_hi = x_refs[0][hm:, :] + gate1 * mix_b[0:hm, :]
    xn = x_refs[2][...] + gate1 * mix_b[hm:, :]
    spans = tm > seq
    if spans:
        keep_p = keep_n = 1.0
        tok = lax.broadcasted_iota(jnp.int32, (tm, FF_CHUNK), 0) % seq
        first, last = tok == 0, tok == seq - 1
    else:
        keep_p = jnp.where((i * tm) % seq == 0, 0.0, 1.0)
        keep_n = jnp.where(((i + 1) * tm) % seq == 0, 0.0, 1.0)
    yext_s[0:HALO, :] = (_adaln(xp, g2, shift, scale) * keep_p).astype(BF16)
    yext_s[HALO:HALO + hm, :] = _adaln(x_lo, g2, shift, scale).astype(BF16)
    yext_s[HALO + hm:HALO + tm, :] = _adaln(x_hi, g2, shift, scale).astype(BF16)
    yext_s[HALO + tm:, :] = (_adaln(xn, g2, shift, scale) * keep_n).astype(BF16)

    def up(c):
        c0 = c * FF_CHUNK
        gext_s[c % 2] = _dot(yext_s[...], wup_ref[:, D_FF + c0:D_FF + c0 + FF_CHUNK])
        return _dot(yext_s[HALO:HALO + tm, :], wup_ref[:, c0:c0 + FF_CHUNK])

    nchunk = D_FF // FF_CHUNK
    a_next = up(0)
    for c in range(nchunk):
        c0 = c * FF_CHUNK
        a = a_next
        if c + 1 < nchunk:
            a_next = up(c + 1)
        gs = gext_s.at[c % 2]
        g_prev = gs[HALO - 1:HALO - 1 + tm, :]
        g_next = gs[HALO + 1:HALO + 1 + tm, :]
        if spans:
            g_prev = jnp.where(first, 0.0, g_prev)
            g_next = jnp.where(last, 0.0, g_next)
        g = (cw_ref[0:1, c0:c0 + FF_CHUNK] * g_prev
             + cw_ref[1:2, c0:c0 + FF_CHUNK] * gs[HALO:HALO + tm, :]
             + cw_ref[2:3, c0:c0 + FF_CHUNK] * g_next
             + cb_ref[:, c0:c0 + FF_CHUNK])
        h_s[:, c0:c0 + FF_CHUNK] = (_silu(g) * a).astype(BF16)
    down = [_dot(h_s[0:hm, :], wdn_ref[...]), _dot(h_s[hm:, :], wdn_ref[...])]
    for r0, xh, dn in ((0, x_lo, down[0]), (hm, x_hi, down[1])):
        y = xh + mod_ref[5:6, :] * dn
        if final:
            y = _rms(y, gf_ref[...])
        y_ref[r0:r0 + hm, :] = y


def _ffn(x2, o_a, o_b, o_c, mod_all, l, seq, is_ctx, wts, g_final, final):
    m = x2.shape[0]
    tm = TM
    assert tm % seq == 0 or seq % tm == 0
    nb = tm // HALO
    last_blk = m // HALO - 1
    row = lambda i: (i, 0)
    prev = lambda i: (jnp.maximum(i * nb - 1, 0), 0)
    nxt = lambda i: (jnp.minimum((i + 1) * nb, last_blk), 0)
    in_specs, args = [], []
    for a in (x2, o_a, o_b, o_c):
        w = a.shape[1]
        in_specs += [pl.BlockSpec((tm, w), row), pl.BlockSpec((HALO, w), prev), pl.BlockSpec((HALO, w), nxt)]
        args += [a, a, a]
    names = ["g_norm2", "w_out", "w_ff_up", "w_ff_conv", "b_ff_conv", "w_ff_down"]
    in_specs += ([_mod_spec(l, seq, tm, is_ctx)] + [_lspec(wts[n], l) for n in names]
                 + [pl.BlockSpec((1, D_MODEL), lambda i: (0, 0))])
    args += [mod_all] + [wts[n] for n in names] + [g_final]
    return pl.pallas_call(
        functools.partial(_ffn_kernel, seq, tm, final),
        grid=(m // tm,),
        in_specs=in_specs,
        out_specs=pl.BlockSpec((tm, D_MODEL), row),
        out_shape=jax.ShapeDtypeStruct((m, D_MODEL), F32),
        scratch_shapes=[pltpu.VMEM((tm + 2 * HALO, D_MODEL), BF16),
                        pltpu.VMEM((tm + 2 * HALO, D_MODEL), BF16),
                        pltpu.VMEM((2, tm + 2 * HALO, FF_CHUNK), F32),
                        pltpu.VMEM((tm, D_FF), BF16)],
        compiler_params=_cparams(("parallel",)),
        name="ffn_ctx" if is_ctx else "ffn_lat",
    )(*args)


def _rope_tables(t):
    pos = np.arange(t)
    rows = (pos // GRID_W).astype(np.float32)
    cols = (pos % GRID_W).astype(np.float32)

    def group(p, d):
        inv = (np.float32(ROPE_THETA) ** (-np.arange(0, d, 2, dtype=np.float32) / np.float32(d))).astype(np.float32)
        ang = (p[:, None] * inv[None, :]).astype(np.float32)
        cs, sn, z = np.cos(ang), np.sin(ang), np.zeros_like(ang)
        return (np.concatenate([cs, cs], 1), np.concatenate([-sn, z], 1), np.concatenate([z, sn], 1))

    gr, gc = group(rows, MLA_ROPE // 2), group(cols, MLA_ROPE // 2)
    ones = np.ones((t, MLA_NOPE), np.float32)
    zeros = np.zeros((t, MLA_NOPE), np.float32)
    pad1 = np.ones((t, LANES - MLA_NOPE - MLA_ROPE), np.float32)
    pad0 = np.zeros((t, LANES - MLA_NOPE - MLA_ROPE), np.float32)
    mla = (np.concatenate([ones, gr[0], gc[0], pad1], 1),
           np.concatenate([zeros, gr[1], gc[1], pad0], 1),
           np.concatenate([zeros, gr[2], gc[2], pad0], 1))
    gr, gc = group(rows, GQA_DH // 2), group(cols, GQA_DH // 2)
    gqa = tuple(np.concatenate([gr[k], gc[k], gr[k], gc[k]], 1) for k in range(3))
    return tuple(jnp.asarray(a, F32) for a in mla + gqa)


def _prep_weights(p):
    w_in = p["w_in"]
    nl = w_in.shape[0]
    qg = [w_in[:, :, 1968 + GQA_DH * h:1968 + GQA_DH * (h + 1)] for h in (0, 2, 1, 3)]
    gh = GATE_HEADS

    def regroup(g16):
        return [g16[..., kind * ML_HEADS + gh * pr:kind * ML_HEADS + gh * (pr + 1)]
                for pr in range(GATE_BLOCKS) for kind in range(4)]

    zeros = lambda a, n: jnp.zeros(a.shape[:-1] + (n,), F32)
    b16 = p["b_ml_gates"]
    b_gates = jnp.concatenate(regroup(b16) + [zeros(b16, LANES - 4 * ML_HEADS)], axis=-1)
    w_in_p = jnp.concatenate([
        w_in[:, :, 0:384],
        *regroup(w_in[:, :, 1952:1968]), zeros(w_in, MLA_NOPE - 4 * ML_HEADS),
        w_in[:, :, 384:416], zeros(w_in, LANES - MLA_NOPE - MLA_ROPE),
        w_in[:, :, 416:1952],
        *qg,
        w_in[:, :, 2224:2480],
    ], axis=2).astype(BF16)
    w_uq = p["w_mla_uq"].reshape(nl, MLA_Q_RANK, MLA_HEADS, MLA_NOPE + MLA_ROPE)
    w_uq = jnp.pad(w_uq, ((0, 0), (0, 0), (0, 0), (0, LANES - MLA_NOPE - MLA_ROPE)))
    w_ukv = p["w_mla_ukv"].reshape(nl, MLA_KV_RANK, MLA_HEADS, MLA_NOPE + MLA_V)
    w_uk = jnp.pad(w_ukv[..., :MLA_NOPE], ((0, 0), (0, 0), (0, 0), (0, LANES - MLA_NOPE)))
    w_ukv_p = jnp.concatenate([w_uk.reshape(nl, MLA_KV_RANK, -1),
                               w_ukv[..., MLA_NOPE:].reshape(nl, MLA_KV_RANK, -1)], axis=2)
    grp = np.arange(2 * LANES) // GQA_DH
    row = lambda a: a[:, None, :]
    return {
        "g_norm1": row(p["g_norm1"]),
        "g_norm2": row(p["g_norm2"]),
        "w_in": w_in_p,
        "g_mla_q": row(p["g_mla_q"]),
        "w_uq": w_uq.reshape(nl, MLA_Q_RANK, -1).astype(BF16),
        "g_mla_kv": row(p["g_mla_kv"]),
        "w_ukv": w_ukv_p.astype(BF16),
        "b_gates": row(b_gates),
        "g_gqa_q": row(jnp.tile(p["g_gqa_q"], (1, GQA_HEADS))),
        "g_gqa_k": row(jnp.tile(p["g_gqa_k"], (1, GQA_KV_HEADS))),
        "gsum": jnp.asarray(grp[:, None] == grp[None, :], BF16),
        "w_ml_conv": p["w_ml_conv"],
        "b_ml_conv": row(p["b_ml_conv"]),
        "w_ml_qt": jnp.swapaxes(p["w_ml_q"], 2, 3).astype(BF16),
        "w_ml_k": p["w_ml_k"].astype(BF16),
        "g_ml_out": row(p["g_ml_out"]),
        "w_out": p["w_out"].astype(BF16),
        "w_ff_up": p["w_ff_up"].astype(BF16),
        "w_ff_conv": p["w_ff_conv"],
        "b_ff_conv": row(p["b_ff_conv"]),
        "w_ff_down": p["w_ff_down"].astype(BF16),
    }


def _layer(x2, mod_all, l, seq, is_ctx, wts, tabs, cache, g_final, final):
    m = x2.shape[0]
    b = m // seq
    r3 = lambda a: a.reshape(b, seq, a.shape[-1])
    outs = _in_proj(x2, mod_all, l, seq, is_ctx, wts, tabs)
    q_m, k_m, v_m, u, v_ml, o_ml, gates, q_g, k_g, v_g = outs[:10]
    segs_a = [(r3(k_m), r3(v_m), None)]
    segs_c = [(r3(k_g), r3(v_g), None)]
    init = None
    if not is_ctx:
        ckv_c, kr_pad, kg_c, vg_c, ct0, n0p, m0p = cache
        kc, vc = _cache_kv(ckv_c, kr_pad, l, wts["w_ukv"])
        segs_a.append((kc, vc, None))
        segs_c.append((kg_c, vg_c, l))
        init = (ct0, n0p, m0p)
    o_a, o_c = _attention(r3(q_m), r3(q_g), segs_a, segs_c)
    ml = _mlstm(r3(u), r3(v_ml), r3(o_ml), gates, l, wts, init)
    x2 = _ffn(x2, o_a.reshape(m, -1), ml[0].reshape(m, -1), o_c.reshape(m, -1), mod_all, l, seq, is_ctx, wts,
              g_final, final)
    state = None
    if is_ctx:
        ckv_n, kr_raw = outs[10:12]
        sfin = ml[2]
        state = (
            r3(ckv_n),
            r3(kr_raw)[:, :, MLA_NOPE:MLA_NOPE + MLA_ROPE],
            k_g.reshape(b, seq, GQA_KV_HEADS, GQA_DH),
            v_g.reshape(b, seq, GQA_KV_HEADS, GQA_DH),
            ml[1],
            jnp.swapaxes(sfin[:, :, 0:2, :], 1, 2),
            jnp.swapaxes(sfin[:, :, 2:4, 0], 1, 2),
        )
    return x2, state


def kernel(x_prompt, x_sample, cache_mla_ckv, cache_mla_krope, cache_gqa_k, cache_gqa_v, state_mlstm_C, state_mlstm_n, state_mlstm_m, c, c_ctx, w_ada, b_ada, g_norm1, g_norm2, w_in, g_mla_q, w_mla_uq, g_mla_kv, w_mla_ukv, w_ml_conv, b_ml_conv, w_ml_q, w_ml_k, b_ml_gates, g_ml_out, g_gqa_q, g_gqa_k, w_out, w_ff_up, w_ff_conv, b_ff_conv, w_ff_down, g_final):
    params = {
        "g_norm1": g_norm1, "g_norm2": g_norm2, "w_in": w_in, "g_mla_q": g_mla_q, "w_mla_uq": w_mla_uq,
        "g_mla_kv": g_mla_kv, "w_mla_ukv": w_mla_ukv, "w_ml_conv": w_ml_conv, "b_ml_conv": b_ml_conv,
        "w_ml_q": w_ml_q, "w_ml_k": w_ml_k, "b_ml_gates": b_ml_gates, "g_ml_out": g_ml_out,
        "g_gqa_q": g_gqa_q, "g_gqa_k": g_gqa_k, "w_out": w_out, "w_ff_up": w_ff_up, "w_ff_conv": w_ff_conv,
        "b_ff_conv": b_ff_conv, "w_ff_down": w_ff_down,
    }
    depth = w_in.shape[0]
    bp, sp, _ = x_prompt.shape
    bs, ss, _ = x_sample.shape
    assert bs + 1 <= 16 and sp % ML_CHUNK == 0 and ss % TM == 0 and (bp * sp) % TM == 0

    cvec = jnp.concatenate([c_ctx[None, :], c, jnp.zeros((16 - 1 - bs, D_MODEL), F32)], axis=0)
    mod_all = _modulation(cvec, w_ada, b_ada).reshape(depth, 16, 6, D_MODEL)
    tabs = _rope_tables(ss)
    gf = g_final[None, :]
    wts = _prep_weights(params)

    tc = cache_mla_ckv.shape[2]
    pad_rows = ((0, 0), (0, 0), (0, 0), (0, 8 - 2), (0, 0))
    cache = (
        cache_mla_ckv,
        jnp.pad(cache_mla_krope, ((0, 0), (0, 0), (0, 0), (MLA_NOPE, LANES - MLA_NOPE - MLA_ROPE))),
        cache_gqa_k.reshape(bs, depth, tc, GQA_KV_HEADS * GQA_DH),
        cache_gqa_v.reshape(bs, depth, tc, GQA_KV_HEADS * GQA_DH),
        state_mlstm_C,
        jnp.pad(jnp.swapaxes(state_mlstm_n, 2, 3), pad_rows),
        jnp.pad(jnp.broadcast_to(jnp.swapaxes(state_mlstm_m, 2, 3)[..., None], (bs, depth, ML_HEADS, 2, LANES)),
                pad_rows),
    )

    xp = x_prompt.reshape(bp * sp, D_MODEL)
    xs = x_sample.reshape(bs * ss, D_MODEL)
    states = []
    for l in range(depth):
        final = l == depth - 1
        xp, st = _layer(xp, mod_all, l, sp, True, wts, None, None, gf, final)
        states.append(st)
        xs, _ = _layer(xs, mod_all, l, ss, False, wts, tabs, cache, gf, final)
    new_state = tuple(jnp.stack([st[k] for st in states], axis=1) for k in range(7))
    return (xp.reshape(bp, sp, D_MODEL), xs.reshape(bs, ss, D_MODEL)) + new_state
```

```python
import functools

import jax
import jax.numpy as jnp
import numpy as np
from jax import lax
from jax.experimental import pallas as pl
from jax.experimental.pallas import tpu as pltpu

F32 = jnp.float32
BF16 = jnp.bfloat16

D_MODEL = 1024
GRID_W = 64
ROPE_THETA = 10000.0
EPS = 1e-6
MLA_HEADS = 4
MLA_NOPE = 64
MLA_ROPE = 32
MLA_V = 64
MLA_Q_RANK = 256
MLA_KV_RANK = 128
ML_HEADS = 4
ML_DH = 128
ML_WIDTH = ML_HEADS * ML_DH
ML_CHUNK = 128
GQA_HEADS = 4
GQA_KV_HEADS = 2
GQA_DH = 64
D_FF = 2816
MLA_SCALE = (MLA_NOPE + MLA_ROPE) ** -0.5
GQA_SCALE = GQA_DH ** -0.5
ML_K_SCALE = ML_DH ** -0.5
LOG2E = 1.4426950408889634

LANES = 128
HALO = 16
VMEM_LIMIT = 52 * 1024 * 1024

C_CQ, C_CKV, C_X, C_U, C_V, C_O, C_QG, C_KG, C_VG, IN_COLS_P = (
    0, 256, 384, 512, 1024, 1536, 2048, 2304, 2432, 2560)
GATE_BLOCKS = 2
FF_CHUNK = 256
TM = 512
TQ = 512


def _cparams(sem):
    return pltpu.CompilerParams(dimension_semantics=sem, vmem_limit_bytes=VMEM_LIMIT)


def _lspec(arr, l):
    nd = arr.ndim - 1
    return pl.BlockSpec((None,) + arr.shape[1:], lambda *_: (l,) + (0,) * nd, pipeline_mode=pl.Buffered(1))


def _dot(a, b):
    return jnp.dot(a, b, preferred_element_type=F32)


def _dot_nt(a, b):
    return lax.dot_general(a, b, (((1,), (1,)), ((), ())), preferred_element_type=F32)


def _rms(x, g):
    return (x * lax.rsqrt(jnp.mean(x * x, axis=-1, keepdims=True) + EPS)) * g


def _silu(x):
    return x * jax.nn.sigmoid(x)


def _adaln(x, g, shift, scale):
    return _rms(x, g) * (1.0 + scale) + shift


def _split2(x):
    hi = x.astype(BF16)
    lo = (x - hi.astype(F32)).astype(BF16)
    return hi, lo


def _split3(x):
    h1 = x.astype(BF16)
    r1 = x - h1.astype(F32)
    h2 = r1.astype(BF16)
    h3 = (r1 - h2.astype(F32)).astype(BF16)
    return h1, h2, h3


def _rope(x, cos, sa, sb, half):
    w = x.shape[-1]
    return x * cos + pltpu.roll(x, w - half, 1) * sa + pltpu.roll(x, half, 1) * sb


def _group_rms(x, gsum, g):
    hi, lo = _split2(x * x)
    ss = _dot(hi, gsum) + _dot(lo, gsum)
    return (x * lax.rsqrt(ss * (1.0 / GQA_DH) + EPS)) * g


def _mod_kernel(c_ref, w_ref, b_ref, o_ref):
    a = _silu(c_ref[...]).astype(BF16)
    o_ref[...] = _dot(a, w_ref[...].astype(BF16)) + b_ref[...]


def _modulation(cvec, w_ada, b_ada):
    nl = w_ada.shape[0]
    tn = 1536
    return pl.pallas_call(
        _mod_kernel,
        grid=(nl, 6 * D_MODEL // tn),
        in_specs=[
            pl.BlockSpec((16, D_MODEL), lambda l, j: (0, 0)),
            pl.BlockSpec((None, D_MODEL, tn), lambda l, j: (l, 0, j)),
            pl.BlockSpec((None, 1, tn), lambda l, j: (l, 0, j)),
        ],
        out_specs=pl.BlockSpec((None, 16, tn), lambda l, j: (l, 0, j)),
        out_shape=jax.ShapeDtypeStruct((nl, 16, 6 * D_MODEL), F32),
        compiler_params=_cparams(("arbitrary", "arbitrary")),
        name="modulation",
    )(cvec, w_ada, b_ada.reshape(nl, 1, 6 * D_MODEL))


def _in_proj_kernel(is_ctx, x_ref, mod_ref, g1_ref, win_ref, gq_ref, wuq_ref, gkv_ref, wukv_ref,
                    bg_ref, ggq_ref, ggk_ref, gsum_ref, *rest):
    if is_ctx:
        (qm_ref, km_ref, vm_ref, u_ref, vml_ref, oml_ref, gt_ref, qg_ref, kg_ref, vg_ref,
         ckv_ref, kr_ref) = rest
    else:
        (cm_ref, sam_ref, sbm_ref, cg_ref, sag_ref, sbg_ref,
         qm_ref, km_ref, vm_ref, u_ref, vml_ref, oml_ref, gt_ref, qg_ref, kg_ref, vg_ref) = rest

    def modulated(s):
        rs = slice(s * TM, (s + 1) * TM)
        return _adaln(x_ref[rs, :], g1_ref[...], mod_ref[0:1, :], mod_ref[1:2, :]).astype(BF16)

    nsub = x_ref.shape[0] // TM
    lane = lax.broadcasted_iota(jnp.int32, (TM, LANES), 1)
    kr_lanes = jnp.logical_and(lane >= MLA_NOPE, lane < MLA_NOPE + MLA_ROPE)
    y_next = modulated(0)
    for s in range(nsub):
        rs = slice(s * TM, (s + 1) * TM)
        y = y_next

        def proj(c0, width, y=y):
            return _dot(y, win_ref[:, c0:c0 + width])

        if not is_ctx:
            rope_m = (cm_ref[rs, :], sam_ref[rs, :], sbm_ref[rs, :], 8)
            rope_g = (cg_ref[rs, :], sag_ref[rs, :], sbg_ref[rs, :], 16)

        z_cq = proj(C_CQ, MLA_Q_RANK)
        ckv_kr = proj(C_CKV, MLA_KV_RANK + LANES)
        if s + 1 < nsub:
            y_next = modulated(s + 1)

        cqn = _rms(z_cq, gq_ref[...]).astype(BF16)
        z_qg = proj(C_QG, GQA_HEADS * GQA_DH)
        qz = _dot(cqn, wuq_ref[...])

        ckvn = _rms(ckv_kr[:, 0:MLA_KV_RANK], gkv_ref[...])
        x_blk = ckv_kr[:, MLA_KV_RANK:]
        kr = jnp.where(kr_lanes, x_blk, 0.0)
        if is_ctx:
            ckv_ref[rs, :] = ckvn
            kr_ref[rs, :] = kr
        else:
            kr = _rope(kr, *rope_m)
        kv_g = proj(C_KG, 2 * LANES)
        kvz = _dot(ckvn.astype(BF16), wukv_ref[...])

        z_u = proj(C_U, ML_WIDTH)
        for h in range(MLA_HEADS):
            blk = qz[:, LANES * h:LANES * (h + 1)]
            if not is_ctx:
                blk = _rope(blk, *rope_m)
            qm_ref[rs, LANES * h:LANES * (h + 1)] = (blk * (MLA_SCALE * LOG2E)).astype(qm_ref.dtype)

        z_v = proj(C_V, ML_WIDTH)
        for h in range(MLA_HEADS):
            km_ref[rs, LANES * h:LANES * (h + 1)] = (kvz[:, LANES * h:LANES * (h + 1)] + kr).astype(km_ref.dtype)
        vm_ref[rs, :] = kvz[:, MLA_HEADS * LANES:].astype(vm_ref.dtype)
        u_ref[rs, :] = z_u.astype(u_ref.dtype)

        z_o = proj(C_O, ML_WIDTH)
        qg = _group_rms(z_qg, gsum_ref[...], ggq_ref[...])
        kg = _group_rms(kv_g[:, 0:LANES], gsum_ref[0:LANES, 0:LANES], ggk_ref[...])
        vml_ref[rs, :] = z_v.astype(vml_ref.dtype)
        if not is_ctx:
            qg = jnp.concatenate([_rope(qg[:, 0:LANES], *rope_g), _rope(qg[:, LANES:], *rope_g)], axis=1)
            kg = _rope(kg, *rope_g)
        qg_ref[rs, :] = (qg * (GQA_SCALE * LOG2E)).astype(qg_ref.dtype)
        kg_ref[rs, :] = kg.astype(kg_ref.dtype)
        vg_ref[rs, :] = kv_g[:, LANES:].astype(vg_ref.dtype)

        oml_ref[rs, :] = z_o.astype(oml_ref.dtype)
        gates_t = (x_blk + bg_ref[...]).T
        for pr in range(GATE_BLOCKS):
            gt_ref[pr, :, rs] = gates_t[8 * pr:8 * (pr + 1), :]


def _mod_spec(l, seq, tm, is_ctx):
    if is_ctx:
        return pl.BlockSpec((None, None, 6, D_MODEL), lambda i: (l, 0, 0, 0))
    return pl.BlockSpec((None, None, 6, D_MODEL), lambda i: (l, 1 + (i * tm) // seq, 0, 0))


def _in_proj(x2, mod_all, l, seq, is_ctx, wts, tabs):
    m = x2.shape[0]
    tm = 2 * TM
    assert m % tm == 0 and (is_ctx or seq % tm == 0)
    nt = m // tm
    row = lambda i: (i, 0)
    names = ["g_norm1", "w_in", "g_mla_q", "w_uq", "g_mla_kv", "w_ukv", "b_gates", "g_gqa_q", "g_gqa_k"]
    in_specs = ([pl.BlockSpec((tm, D_MODEL), row), _mod_spec(l, seq, tm, is_ctx)]
                + [_lspec(wts[n], l) for n in names]
                + [pl.BlockSpec((2 * LANES, 2 * LANES), lambda i: (0, 0))])
    args = [x2, mod_all] + [wts[n] for n in names] + [wts["gsum"]]
    if not is_ctx:
        tpos = seq // tm
        in_specs += [pl.BlockSpec((tm, LANES), lambda i: (i % tpos, 0))] * 6
        args += list(tabs)
    act = F32 if is_ctx else BF16
    outs = [
        (MLA_HEADS * LANES, BF16),
        (MLA_HEADS * LANES, BF16),
        (MLA_HEADS * MLA_V, BF16),
        (ML_WIDTH, F32),
        (ML_WIDTH, BF16),
        (ML_WIDTH, F32),
        (None, F32),
        (GQA_HEADS * GQA_DH, BF16),
        (LANES, act),
        (LANES, act),
    ]
    if is_ctx:
        outs += [(MLA_KV_RANK, F32), (LANES, F32)]
    return pl.pallas_call(
        functools.partial(_in_proj_kernel, is_ctx),
        grid=(nt,),
        in_specs=in_specs,
        out_specs=[pl.BlockSpec((GATE_BLOCKS, 8, tm), lambda i: (0, 0, i)) if w is None
                   else pl.BlockSpec((tm, w), row) for w, _ in outs],
        out_shape=[jax.ShapeDtypeStruct((GATE_BLOCKS, 8, m) if w is None else (m, w), dt)
                   for w, dt in outs],
        compiler_params=_cparams(("parallel",)),
        name="in_proj_ctx" if is_ctx else "in_proj_lat",
    )(*args)


def _cache_kv_kernel(ckv_ref, kr_ref, wukv_ref, k_ref, v_ref):
    for i in range(ckv_ref.shape[0]):
        kvz = _dot(ckv_ref[i].astype(BF16), wukv_ref[...])
        kr = kr_ref[i]
        for h in range(MLA_HEADS):
            k_ref[i, :, LANES * h:LANES * (h + 1)] = (kvz[:, LANES * h:LANES * (h + 1)] + kr).astype(k_ref.dtype)
        v_ref[i] = kvz[:, MLA_HEADS * LANES:].astype(v_ref.dtype)


def _cache_kv(ckv, kr_pad, l, w_ukv):
    b, _, tc, _ = ckv.shape
    cache_blk = pl.BlockSpec((b, None, tc, LANES), lambda i: (0, l, 0, 0))
    return pl.pallas_call(
        _cache_kv_kernel,
        grid=(1,),
        in_specs=[cache_blk, cache_blk, _lspec(w_ukv, l)],
        out_specs=[pl.BlockSpec((b, tc, MLA_HEADS * LANES), lambda i: (0, 0, 0)),
                   pl.BlockSpec((b, tc, MLA_HEADS * MLA_V), lambda i: (0, 0, 0))],
        out_shape=[jax.ShapeDtypeStruct((b, tc, MLA_HEADS * LANES), BF16),
                   jax.ShapeDtypeStruct((b, tc, MLA_HEADS * MLA_V), BF16)],
        compiler_params=_cparams(("arbitrary",)),
        name="cache_kv",
    )(ckv, kr_pad, w_ukv)


def _attn_kernel(nseg, qa_ref, qc_ref, *refs):
    kva, kvc = refs[:2 * nseg], refs[2 * nseg:4 * nseg]
    oa_ref, oc_ref = refs[4 * nseg:]
    nb, tq = qa_ref.shape[0], qa_ref.shape[1]
    lo = lax.broadcasted_iota(jnp.int32, (tq, LANES), 1) < (LANES // 2)

    def qk(unit):
        bi, is_mla, h = unit
        kv = kva if is_mla else kvc
        if is_mla:
            qh = qa_ref[bi, :, LANES * h:LANES * (h + 1)]
            ksl = slice(LANES * h, LANES * (h + 1))
        else:
            blk = qc_ref[bi, :, LANES * (h % 2):LANES * (h % 2 + 1)].astype(F32)
            qh = jnp.where(lo if h < 2 else jnp.logical_not(lo), blk, 0.0).astype(BF16)
            ksl = slice(0, LANES)
        return [_dot_nt(qh, kv[2 * j][bi, :, ksl].astype(BF16)) for j in range(nseg)]

    def softmax(ss):
        mx = jnp.max(ss[0], axis=-1, keepdims=True)
        for s in ss[1:]:
            mx = jnp.maximum(mx, jnp.max(s, axis=-1, keepdims=True))
        es = [jnp.exp2(s - mx) for s in ss]
        den = es[0].sum(axis=-1, keepdims=True)
        for e in es[1:]:
            den = den + e.sum(axis=-1, keepdims=True)
        return [e.astype(BF16) for e in es], den

    def pv(unit, es, den):
        bi, is_mla, h = unit
        kv = kva if is_mla else kvc
        vsl = slice(LANES * (h // 2), LANES * (h // 2 + 1)) if is_mla else slice(0, LANES)
        acc = None
        for j, e in enumerate(es):
            a = _dot(e, kv[2 * j + 1][bi, :, vsl].astype(BF16))
            acc = a if acc is None else acc + a
        return acc / den

    units = [(bi, is_mla, h) for bi in range(nb) for is_mla in (True, False) for h in range(4)]
    out = {}
    ahead = qk(units[0])
    for n, unit in enumerate(units):
        ss = ahead
        if n + 1 < len(units):
            ahead = qk(units[n + 1])
        out[unit] = pv(unit, *softmax(ss))
    half = LANES // 2
    for bi in range(nb):
        a = [out[(bi, True, h)] for h in range(4)]
        c = [out[(bi, False, h)] for h in range(4)]
        oa_ref[bi, :, 0:LANES] = jnp.where(lo, a[0], a[1]).astype(oa_ref.dtype)
        oa_ref[bi, :, LANES:] = jnp.where(lo, a[2], a[3]).astype(oa_ref.dtype)
        oc_ref[bi, :, 0:LANES] = jnp.where(lo, c[0], pltpu.roll(c[1], half, 1)).astype(oc_ref.dtype)
        oc_ref[bi, :, LANES:] = jnp.where(lo, pltpu.roll(c[2], half, 1), c[3]).astype(oc_ref.dtype)


def _attention(q_a, q_c, segs_a, segs_c):
    b, tq_all, _ = q_a.shape
    tq = min(TQ, tq_all)
    nb = max(1, min(b, 2 * TQ // tq_all))
    in_specs = [pl.BlockSpec((nb, tq, q_a.shape[2]), lambda i, j: (i, j, 0)),
                pl.BlockSpec((nb, tq, q_c.shape[2]), lambda i, j: (i, j, 0))]
    args = [q_a, q_c]
    for segs in (segs_a, segs_c):
        for k, v, layer in segs:
            for a in (k, v):
                if layer is None:
                    in_specs.append(pl.BlockSpec((nb,) + a.shape[1:], lambda i, j: (i, 0, 0)))
                else:
                    in_specs.append(pl.BlockSpec((nb, None) + a.shape[2:],
                                                 lambda i, j, layer=layer: (i, layer, 0, 0)))
                args.append(a)
    out_spec = pl.BlockSpec((nb, tq, 2 * LANES), lambda i, j: (i, j, 0))
    return pl.pallas_call(
        functools.partial(_attn_kernel, len(segs_a)),
        grid=(b // nb, tq_all // tq),
        in_specs=in_specs,
        out_specs=[out_spec, out_spec],
        out_shape=[jax.ShapeDtypeStruct((b, tq_all, 2 * LANES), BF16)] * 2,
        compiler_params=_cparams(("parallel", "parallel")),
        name="attn_lat" if len(segs_a) > 1 else "attn_ctx",
    )(*args)


def _log_sigmoid(x):
    return jnp.minimum(x, 0.0) - jnp.log(1.0 + jnp.exp(-jnp.abs(x)))


GATE_HEADS = 2


def _mlstm_kernel(has_init, nc, hps, u_ref, v_ref, o_ref, gt_ref, cw_ref, cb_ref, wqt_ref, wk_ref, go_ref,
                  *rest):
    if has_init:
        c0_ref, n0_ref, m0_ref, out_ref, qt_s, k_s, vt_s, ht_f, ht_b, acol_s, row_s, st_s = rest
    else:
        out_ref, cfin_ref, sfin_ref, qt_s, k_s, vt_s, ht_f, ht_b, acol_s, row_s, st_s = rest
    t = nc * ML_CHUNK
    sq = (ML_CHUNK, ML_CHUNK)

    rows = lax.broadcasted_iota(jnp.int32, (t, LANES), 0)
    r_i = lax.broadcasted_iota(jnp.int32, sq, 0)
    c_i = lax.broadcasted_iota(jnp.int32, sq, 1)
    lower = c_i <= r_i
    upper = c_i >= r_i
    eye = jnp.where(c_i == r_i, 1.0, 0.0).astype(BF16)
    tri_lo = jnp.where(lower, 1.0, 0.0).astype(BF16)
    tri_up = jnp.where(upper, 1.0, 0.0).astype(BF16)
    row8 = lax.broadcasted_iota(jnp.int32, (8, ML_CHUNK), 0)
    is_f = lax.broadcasted_iota(jnp.int32, (8, t), 0) % 4 >= 2
    lg_all = []
    for pb in range(hps // GATE_HEADS):
        gates = gt_ref[pb]
        lg_all.append(jnp.where(is_f, _log_sigmoid(gates), gates))

    def gate_prep(pb, c):
        sl = slice(c * ML_CHUNK, (c + 1) * ML_CHUNK)
        lg = lg_all[pb][:, sl]
        h1, h2, h3 = _split3(lg)
        cs = jnp.where(row8 < 4, _dot(h1, tri_up) + _dot(h2, tri_up) + _dot(h3, tri_up),
                       _dot(h1, tri_lo) + _dot(h2, tri_lo) + _dot(h3, tri_lo)) * LOG2E
        row_s[pb, :, sl] = cs
        a = lg * LOG2E - pltpu.roll(cs, 6, 0)
        for jj in range(GATE_HEADS):
            for d in range(2):
                r = 4 * d + jj
                acol_s[2 * (GATE_HEADS * pb + jj) + d, sl, :] = jnp.broadcast_to(a[r:r + 1, :], sq).T

    gate_units = [(pb, c) for pb in range(hps // GATE_HEADS) for c in range(nc)]
    share = -(-len(gate_units) // hps)
    for j in range(hps):
        hs = slice(LANES * j, LANES * (j + 1))
        u = u_ref[:, hs]
        up = jnp.where(rows == 0, 0.0, pltpu.roll(u, 1, 0))
        un = jnp.where(rows == t - 1, 0.0, pltpu.roll(u, t - 1, 0))
        uc = _silu(cw_ref[0:1, hs] * up + cw_ref[1:2, hs] * u + cw_ref[2:3, hs] * un + cb_ref[:, hs]).astype(BF16)
        for pb, c in gate_units[j * share:(j + 1) * share]:
            gate_prep(pb, c)
        vt_s[hs, :] = _dot_nt(eye, v_ref[:, hs]).astype(BF16)
        qt_s[hs, :] = _dot_nt(wqt_ref[j], uc).astype(BF16)
        k_s[:, hs] = _dot(uc, wk_ref[j]) * ML_K_SCALE

    for c in range(nc):
        sl = slice(c * ML_CHUNK, (c + 1) * ML_CHUNK)
        for j in range(hps):
            hs = slice(LANES * j, LANES * (j + 1))
            st_s[j, sl, :] = _dot(k_s[sl, hs].astype(BF16), qt_s[hs, sl])

    def chunk(j, d, c0, ct, n, m2):
        hs = slice(LANES * j, LANES * (j + 1))
        cs_ = pl.ds(c0, ML_CHUNK)
        last = 0 if d else ML_CHUNK - 1
        qt = qt_s[hs, cs_]
        k = k_s[cs_, hs]
        vt = vt_s[hs, cs_]
        acol = acol_s[2 * j + d, cs_, :]
        r = 2 + 4 * d + j % GATE_HEADS
        bc = row_s[j // GATE_HEADS, r:r + 1, cs_]
        g = bc + m2
        dlog = jnp.where(lower if d else upper, acol + bc, -jnp.inf)
        m_t = jnp.maximum(g, jnp.max(dlog, axis=0, keepdims=True))
        w = jnp.exp2(dlog - m_t)
        inter = jnp.exp2(g - m_t)
        st = st_s[j, cs_, :] * w
        qn = _dot(jnp.broadcast_to(n, (8, LANES)).astype(BF16), qt)[0:1, :]
        den = jnp.sum(st, axis=0, keepdims=True) + inter * qn
        numt = _dot(vt, st.astype(BF16)) + inter * _dot(ct.astype(BF16), qt)
        ht = numt * (1.0 / jnp.maximum(jnp.abs(den), jnp.exp2(-m_t)))
        b_last = jnp.broadcast_to(bc[:, last:last + 1], (1, LANES))
        m_new = jnp.broadcast_to(m_t[:, last:last + 1], (1, LANES))
        kw = k * jnp.exp2(acol + (b_last - m_new))
        decay = jnp.exp2(m2 + (b_last - m_new))
        ct_new = decay * ct + _dot(vt, kw.astype(BF16))
        n_new = decay * n + jnp.sum(kw, axis=0, keepdims=True)
        return ht, ct_new, n_new, m_new

    def step(i, carry):
        cf = i * ML_CHUNK
        cb = (nc - 1 - i) * ML_CHUNK
        if not isinstance(i, int):
            cf = pl.multiple_of(cf, ML_CHUNK)
            cb = pl.multiple_of(cb, ML_CHUNK)
        new = []
        for j in range(hps):
            hs = slice(LANES * j, LANES * (j + 1))
            for d, (c0, ht_s) in enumerate(((cf, ht_f), (cb, ht_b))):
                ht, ct, n, m2 = chunk(j, d, c0, *carry[2 * j + d])
                ht_s[hs, pl.ds(c0, ML_CHUNK)] = ht
                new.append((ct, n, m2))
        return tuple(new)

    if has_init:
        init = tuple((c0_ref[d, j].T, n0_ref[j, d:d + 1, :], m0_ref[j, d:d + 1, :] * LOG2E)
                     for j in range(hps) for d in range(2))
    else:
        init = tuple((jnp.zeros(sq, F32), jnp.zeros((1, LANES), F32), jnp.zeros((1, LANES), F32))
                     for j in range(hps) for d in range(2))
    if nc <= 16:
        carry = init
        for i in range(nc):
            carry = step(i, carry)
    else:
        carry = lax.fori_loop(0, nc, step, init, unroll=4)

    for j in range(hps):
        hs = slice(LANES * j, LANES * (j + 1))
        hsum = (ht_f[hs, :] + ht_b[hs, :]).T
        hn = _rms(hsum, go_ref[:, hs])
        out_ref[:, hs] = (hn * jax.nn.sigmoid(o_ref[:, hs])).astype(out_ref.dtype)

    if not has_init:
        for j in range(hps):
            for d in range(2):
                ct, n, m2 = carry[2 * j + d]
                cfin_ref[d, j] = ct.T
            nf, nb = carry[2 * j][1], carry[2 * j + 1][1]
            mf, mb = carry[2 * j][2], carry[2 * j + 1][2]
            sfin_ref[j] = jnp.concatenate(
                [nf, nb, mf * (1.0 / LOG2E), mb * (1.0 / LOG2E), jnp.zeros((4, LANES), F32)], axis=0)


def _mlstm(u, v_ml, o_ml, gates, l, wts, init):
    b, t, _ = u.shape
    nc = t // ML_CHUNK
    has_init = init is not None
    hps = ML_HEADS if t <= 4 * ML_CHUNK else GATE_HEADS
    w2 = hps * LANES
    pair_blk = lambda i, p: (i, 0, p)
    in_specs = [
        pl.BlockSpec((None, t, w2), pair_blk),
        pl.BlockSpec((None, t, w2), pair_blk),
        pl.BlockSpec((None, t, w2), pair_blk),
        pl.BlockSpec((hps // GATE_HEADS, 8, t), lambda i, p: (p, 0, i)),
        pl.BlockSpec((None, 3, w2), lambda i, p: (l, 0, p)),
        pl.BlockSpec((None, 1, w2), lambda i, p: (l, 0, p)),
        pl.BlockSpec((None, hps, ML_DH, ML_DH), lambda i, p: (l, p, 0, 0)),
        pl.BlockSpec((None, hps, ML_DH, ML_DH), lambda i, p: (l, p, 0, 0)),
        pl.BlockSpec((None, 1, w2), lambda i, p: (l, 0, p)),
    ]
    args = [u, v_ml, o_ml, gates, wts["w_ml_conv"], wts["b_ml_conv"], wts["w_ml_qt"], wts["w_ml_k"],
            wts["g_ml_out"]]
    out_specs = [pl.BlockSpec((None, t, w2), pair_blk)]
    out_shape = [jax.ShapeDtypeStruct((b, t, ML_WIDTH), BF16)]
    state_c = pl.BlockSpec((None, 2, hps, ML_DH, ML_DH), lambda i, p: (i, 0, p, 0, 0))
    state_s = pl.BlockSpec((None, hps, 8, LANES), lambda i, p: (i, p, 0, 0))
    if has_init:
        in_specs += [pl.BlockSpec((None, None, 2, hps, ML_DH, ML_DH), lambda i, p: (i, l, 0, p, 0, 0)),
                     pl.BlockSpec((None, None, hps, 8, LANES), lambda i, p: (i, l, p, 0, 0)),
                     pl.BlockSpec((None, None, hps, 8, LANES), lambda i, p: (i, l, p, 0, 0))]
        args += list(init)
    else:
        out_specs += [state_c, state_s]
        out_shape += [
            jax.ShapeDtypeStruct((b, 2, ML_HEADS, ML_DH, ML_DH), F32),
            jax.ShapeDtypeStruct((b, ML_HEADS, 8, LANES), F32),
        ]
    return pl.pallas_call(
        functools.partial(_mlstm_kernel, has_init, nc, hps),
        grid=(b, ML_HEADS // hps),
        in_specs=in_specs,
        out_specs=out_specs,
        out_shape=out_shape,
        scratch_shapes=[
            pltpu.VMEM((w2, t), BF16),
            pltpu.VMEM((t, w2), F32),
            pltpu.VMEM((w2, t), BF16),
            pltpu.VMEM((w2, t), F32),
            pltpu.VMEM((w2, t), F32),
            pltpu.VMEM((2 * hps, t, LANES), F32),
            pltpu.VMEM((hps // GATE_HEADS, 8, t), F32),
            pltpu.VMEM((hps, t, LANES), F32),
        ],
        compiler_params=_cparams(("parallel", "parallel")),
        name="mlstm_lat" if has_init else "mlstm_ctx",
    )(*args)


def _ffn_kernel(seq, tm, final, *refs):
    x_refs, oa_refs, ob_refs, oc_refs = refs[0:3], refs[3:6], refs[6:9], refs[9:12]
    (mod_ref, g2_ref, wout_ref, wup_ref, cw_ref, cb_ref, wdn_ref, gf_ref, y_ref,
     oext_s, yext_s, gext_s, h_s) = refs[12:]
    i = pl.program_id(0)
    a_w = MLA_HEADS * MLA_V
    ext = ((0, HALO), (HALO, tm), (HALO + tm, HALO))
    for (r0, n), k in zip(ext, (1, 0, 2)):
        oext_s[r0:r0 + n, 0:a_w] = oa_refs[k][...]
        oext_s[r0:r0 + n, a_w:a_w + ML_WIDTH] = ob_refs[k][...]
        oext_s[r0:r0 + n, a_w + ML_WIDTH:] = oc_refs[k][...]
    mix = _dot(oext_s[...], wout_ref[...])
    gate1 = mod_ref[2:3, :]
    x = x_refs[0][...] + gate1 * mix[HALO:HALO + tm, :]
    xp = x_refs[1][...] + gate1 * mix[0:HALO, :]
    xn = x_refs[2][...] + gate1 * mix[HALO + tm:, :]

    g2 = g2_ref[...]
    shift = mod_ref[3:4, :]
    scale = mod_ref[4:5, :]
    spans = tm > seq
    if spans:
        keep_p = keep_n = 1.0
        tok = lax.broadcasted_iota(jnp.int32, (tm, FF_CHUNK), 0) % seq
        first, last = tok == 0, tok == seq - 1
    else:
        keep_p = jnp.where((i * tm) % seq == 0, 0.0, 1.0)
        keep_n = jnp.where(((i + 1) * tm) % seq == 0, 0.0, 1.0)
    yext_s[0:HALO, :] = (_adaln(xp, g2, shift, scale) * keep_p).astype(BF16)
    yext_s[HALO:HALO + tm, :] = _adaln(x, g2, shift, scale).astype(BF16)
    yext_s[HALO + tm:, :] = (_adaln(xn, g2, shift, scale) * keep_n).astype(BF16)

    def up(c):
        c0 = c * FF_CHUNK
        gext_s[c % 2] = _dot(yext_s[...], wup_ref[:, D_FF + c0:D_FF + c0 + FF_CHUNK])
        return _dot(yext_s[HALO:HALO + tm, :], wup_ref[:, c0:c0 + FF_CHUNK])

    nchunk = D_FF // FF_CHUNK
    a_next = up(0)
    for c in range(nchunk):
        c0 = c * FF_CHUNK
        a = a_next
        if c + 1 < nchunk:
            a_next = up(c + 1)
        gs = gext_s.at[c % 2]
        g_prev = gs[HALO - 1:HALO - 1 + tm, :]
        g_next = gs[HALO + 1:HALO + 1 + tm, :]
        if spans:
            g_prev = jnp.where(first, 0.0, g_prev)
            g_next = jnp.where(last, 0.0, g_next)
        g = (cw_ref[0:1, c0:c0 + FF_CHUNK] * g_prev
             + cw_ref[1:2, c0:c0 + FF_CHUNK] * gs[HALO:HALO + tm, :]
             + cw_ref[2:3, c0:c0 + FF_CHUNK] * g_next
             + cb_ref[:, c0:c0 + FF_CHUNK])
        h_s[:, c0:c0 + FF_CHUNK] = (_silu(g) * a).astype(BF16)
    y = x + mod_ref[5:6, :] * _dot(h_s[...], wdn_ref[...])
    if final:
        y = _rms(y, gf_ref[...])
    y_ref[...] = y


def _ffn(x2, o_a, o_b, o_c, mod_all, l, seq, is_ctx, wts, g_final, final):
    m = x2.shape[0]
    tm = TM
    assert tm % seq == 0 or seq % tm == 0
    nb = tm // HALO
    last_blk = m // HALO - 1
    row = lambda i: (i, 0)
    prev = lambda i: (jnp.maximum(i * nb - 1, 0), 0)
    nxt = lambda i: (jnp.minimum((i + 1) * nb, last_blk), 0)
    in_specs, args = [], []
    for a in (x2, o_a, o_b, o_c):
        w = a.shape[1]
        in_specs += [pl.BlockSpec((tm, w), row), pl.BlockSpec((HALO, w), prev), pl.BlockSpec((HALO, w), nxt)]
        args += [a, a, a]
    names = ["g_norm2", "w_out", "w_ff_up", "w_ff_conv", "b_ff_conv", "w_ff_down"]
    in_specs += ([_mod_spec(l, seq, tm, is_ctx)] + [_lspec(wts[n], l) for n in names]
                 + [pl.BlockSpec((1, D_MODEL), lambda i: (0, 0))])
    args += [mod_all] + [wts[n] for n in names] + [g_final]
    return pl.pallas_call(
        functools.partial(_ffn_kernel, seq, tm, final),
        grid=(m // tm,),
        in_specs=in_specs,
        out_specs=pl.BlockSpec((tm, D_MODEL), row),
        out_shape=jax.ShapeDtypeStruct((m, D_MODEL), F32),
        scratch_shapes=[pltpu.VMEM((tm + 2 * HALO, D_MODEL), BF16),
                        pltpu.VMEM((tm + 2 * HALO, D_MODEL), BF16),
                        pltpu.VMEM((2, tm + 2 * HALO, FF_CHUNK), F32),
                        pltpu.VMEM((tm, D_FF), BF16)],
        compiler_params=_cparams(("parallel",)),
        name="ffn_ctx" if is_ctx else "ffn_lat",
    )(*args)


def _rope_tables(t):
    pos = np.arange(t)
    rows = (pos // GRID_W).astype(np.float32)
    cols = (pos % GRID_W).astype(np.float32)

    def group(p, d):
        inv = (np.float32(ROPE_THETA) ** (-np.arange(0, d, 2, dtype=np.float32) / np.float32(d))).astype(np.float32)
        ang = (p[:, None] * inv[None, :]).astype(np.float32)
        cs, sn, z = np.cos(ang), np.sin(ang), np.zeros_like(ang)
        return (np.concatenate([cs, cs], 1), np.concatenate([-sn, z], 1), np.concatenate([z, sn], 1))

    gr, gc = group(rows, MLA_ROPE // 2), group(cols, MLA_ROPE // 2)
    ones = np.ones((t, MLA_NOPE), np.float32)
    zeros = np.zeros((t, MLA_NOPE), np.float32)
    pad1 = np.ones((t, LANES - MLA_NOPE - MLA_ROPE), np.float32)
    pad0 = np.zeros((t, LANES - MLA_NOPE - MLA_ROPE), np.float32)
    mla = (np.concatenate([ones, gr[0], gc[0], pad1], 1),
           np.concatenate([zeros, gr[1], gc[1], pad0], 1),
           np.concatenate([zeros, gr[2], gc[2], pad0], 1))
    gr, gc = group(rows, GQA_DH // 2), group(cols, GQA_DH // 2)
    gqa = tuple(np.concatenate([gr[k], gc[k], gr[k], gc[k]], 1) for k in range(3))
    return tuple(jnp.asarray(a, F32) for a in mla + gqa)


def _prep_weights(p):
    w_in = p["w_in"]
    nl = w_in.shape[0]
    qg = [w_in[:, :, 1968 + GQA_DH * h:1968 + GQA_DH * (h + 1)] for h in (0, 2, 1, 3)]
    gh = GATE_HEADS

    def regroup(g16):
        return [g16[..., kind * ML_HEADS + gh * pr:kind * ML_HEADS + gh * (pr + 1)]
                for pr in range(GATE_BLOCKS) for kind in range(4)]

    zeros = lambda a, n: jnp.zeros(a.shape[:-1] + (n,), F32)
    b16 = p["b_ml_gates"]
    b_gates = jnp.concatenate(regroup(b16) + [zeros(b16, LANES - 4 * ML_HEADS)], axis=-1)
    w_in_p = jnp.concatenate([
        w_in[:, :, 0:384],
        *regroup(w_in[:, :, 1952:1968]), zeros(w_in, MLA_NOPE - 4 * ML_HEADS),
        w_in[:, :, 384:416], zeros(w_in, LANES - MLA_NOPE - MLA_ROPE),
        w_in[:, :, 416:1952],
        *qg,
        w_in[:, :, 2224:2480],
    ], axis=2).astype(BF16)
    w_uq = p["w_mla_uq"].reshape(nl, MLA_Q_RANK, MLA_HEADS, MLA_NOPE + MLA_ROPE)
    w_uq = jnp.pad(w_uq, ((0, 0), (0, 0), (0, 0), (0, LANES - MLA_NOPE - MLA_ROPE)))
    w_ukv = p["w_mla_ukv"].reshape(nl, MLA_KV_RANK, MLA_HEADS, MLA_NOPE + MLA_V)
    w_uk = jnp.pad(w_ukv[..., :MLA_NOPE], ((0, 0), (0, 0), (0, 0), (0, LANES - MLA_NOPE)))
    w_ukv_p = jnp.concatenate([w_uk.reshape(nl, MLA_KV_RANK, -1),
                               w_ukv[..., MLA_NOPE:].reshape(nl, MLA_KV_RANK, -1)], axis=2)
    grp = np.arange(2 * LANES) // GQA_DH
    row = lambda a: a[:, None, :]
    return {
        "g_norm1": row(p["g_norm1"]),
        "g_norm2": row(p["g_norm2"]),
        "w_in": w_in_p,
        "g_mla_q": row(p["g_mla_q"]),
        "w_uq": w_uq.reshape(nl, MLA_Q_RANK, -1).astype(BF16),
        "g_mla_kv": row(p["g_mla_kv"]),
        "w_ukv": w_ukv_p.astype(BF16),
        "b_gates": row(b_gates),
        "g_gqa_q": row(jnp.tile(p["g_gqa_q"], (1, GQA_HEADS))),
        "g_gqa_k": row(jnp.tile(p["g_gqa_k"], (1, GQA_KV_HEADS))),
        "gsum": jnp.asarray(grp[:, None] == grp[None, :], BF16),
        "w_ml_conv": p["w_ml_conv"],
        "b_ml_conv": row(p["b_ml_conv"]),
        "w_ml_qt": jnp.swapaxes(p["w_ml_q"], 2, 3).astype(BF16),
        "w_ml_k": p["w_ml_k"].astype(BF16),
        "g_ml_out": row(p["g_ml_out"]),
        "w_out": p["w_out"].astype(BF16),
        "w_ff_up": p["w_ff_up"].astype(BF16),
        "w_ff_conv": p["w_ff_conv"],
        "b_ff_conv": row(p["b_ff_conv"]),
        "w_ff_down": p["w_ff_down"].astype(BF16),
    }


def _layer(x2, mod_all, l, seq, is_ctx, wts, tabs, cache, g_final, final):
    m = x2.shape[0]
    b = m // seq
    r3 = lambda a: a.reshape(b, seq, a.shape[-1])
    outs = _in_proj(x2, mod_all, l, seq, is_ctx, wts, tabs)
    q_m, k_m, v_m, u, v_ml, o_ml, gates, q_g, k_g, v_g = outs[:10]
    segs_a = [(r3(k_m), r3(v_m), None)]
    segs_c = [(r3(k_g), r3(v_g), None)]
    init = None
    if not is_ctx:
        ckv_c, kr_pad, kg_c, vg_c, ct0, n0p, m0p = cache
        kc, vc = _cache_kv(ckv_c, kr_pad, l, wts["w_ukv"])
        segs_a.append((kc, vc, None))
        segs_c.append((kg_c, vg_c, l))
        init = (ct0, n0p, m0p)
    o_a, o_c = _attention(r3(q_m), r3(q_g), segs_a, segs_c)
    ml = _mlstm(r3(u), r3(v_ml), r3(o_ml), gates, l, wts, init)
    x2 = _ffn(x2, o_a.reshape(m, -1), ml[0].reshape(m, -1), o_c.reshape(m, -1), mod_all, l, seq, is_ctx, wts,
              g_final, final)
    state = None
    if is_ctx:
        ckv_n, kr_raw = outs[10:12]
        sfin = ml[2]
        state = (
            r3(ckv_n),
            r3(kr_raw)[:, :, MLA_NOPE:MLA_NOPE + MLA_ROPE],
            k_g.reshape(b, seq, GQA_KV_HEADS, GQA_DH),
            v_g.reshape(b, seq, GQA_KV_HEADS, GQA_DH),
            ml[1],
            jnp.swapaxes(sfin[:, :, 0:2, :], 1, 2),
            jnp.swapaxes(sfin[:, :, 2:4, 0], 1, 2),
        )
    return x2, state


def kernel(x_prompt, x_sample, cache_mla_ckv, cache_mla_krope, cache_gqa_k, cache_gqa_v, state_mlstm_C, state_mlstm_n, state_mlstm_m, c, c_ctx, w_ada, b_ada, g_norm1, g_norm2, w_in, g_mla_q, w_mla_uq, g_mla_kv, w_mla_ukv, w_ml_conv, b_ml_conv, w_ml_q, w_ml_k, b_ml_gates, g_ml_out, g_gqa_q, g_gqa_k, w_out, w_ff_up, w_ff_conv, b_ff_conv, w_ff_down, g_final):
    params = {
        "g_norm1": g_norm1, "g_norm2": g_norm2, "w_in": w_in, "g_mla_q": g_mla_q, "w_mla_uq": w_mla_uq,
        "g_mla_kv": g_mla_kv, "w_mla_ukv": w_mla_ukv, "w_ml_conv": w_ml_conv, "b_ml_conv": b_ml_conv,
        "w_ml_q": w_ml_q, "w_ml_k": w_ml_k, "b_ml_gates": b_ml_gates, "g_ml_out": g_ml_out,
        "g_gqa_q": g_gqa_q, "g_gqa_k": g_gqa_k, "w_out": w_out, "w_ff_up": w_ff_up, "w_ff_conv": w_ff_conv,
        "b_ff_conv": b_ff_conv, "w_ff_down": w_ff_down,
    }
    depth = w_in.shape[0]
    bp, sp, _ = x_prompt.shape
    bs, ss, _ = x_sample.shape
    assert bs + 1 <= 16 and sp % ML_CHUNK == 0 and ss % TM == 0 and (bp * sp) % TM == 0

    cvec = jnp.concatenate([c_ctx[None, :], c, jnp.zeros((16 - 1 - bs, D_MODEL), F32)], axis=0)
    mod_all = _modulation(cvec, w_ada, b_ada).reshape(depth, 16, 6, D_MODEL)
    tabs = _rope_tables(ss)
    gf = g_final[None, :]
    wts = _prep_weights(params)

    tc = cache_mla_ckv.shape[2]
    pad_rows = ((0, 0), (0, 0), (0, 0), (0, 8 - 2), (0, 0))
    cache = (
        cache_mla_ckv,
        jnp.pad(cache_mla_krope, ((0, 0), (0, 0), (0, 0), (MLA_NOPE, LANES - MLA_NOPE - MLA_ROPE))),
        cache_gqa_k.reshape(bs, depth, tc, GQA_KV_HEADS * GQA_DH),
        cache_gqa_v.reshape(bs, depth, tc, GQA_KV_HEADS * GQA_DH),
        state_mlstm_C,
        jnp.pad(jnp.swapaxes(state_mlstm_n, 2, 3), pad_rows),
        jnp.pad(jnp.broadcast_to(jnp.swapaxes(state_mlstm_m, 2, 3)[..., None], (bs, depth, ML_HEADS, 2, LANES)),
                pad_rows),
    )

    xp = x_prompt.reshape(bp * sp, D_MODEL)
    xs = x_sample.reshape(bs * ss, D_MODEL)
    states = []
    for l in range(depth):
        final = l == depth - 1
        xp, st = _layer(xp, mod_all, l, sp, True, wts, None, None, gf, final)
        states.append(st)
        xs, _ = _layer(xs, mod_all, l, ss, False, wts, tabs, cache, gf, final)
    new_state = tuple(jnp.stack([st[k] for st in states], axis=1) for k in range(7))
    return (xp.reshape(bp, sp, D_MODEL), xs.reshape(bs, ss, D_MODEL)) + new_state
```

```python
import functools

import jax
import jax.numpy as jnp
import numpy as np
from jax import lax
from jax.experimental import pallas as pl
from jax.experimental.pallas import tpu as pltpu

F32 = jnp.float32
BF16 = jnp.bfloat16

D_MODEL = 1024
GRID_W = 64
ROPE_THETA = 10000.0
EPS = 1e-6
MLA_HEADS = 4
MLA_NOPE = 64
MLA_ROPE = 32
MLA_V = 64
MLA_Q_RANK = 256
MLA_KV_RANK = 128
ML_HEADS = 4
ML_DH = 128
ML_WIDTH = ML_HEADS * ML_DH
ML_CHUNK = 128
GQA_HEADS = 4
GQA_KV_HEADS = 2
GQA_DH = 64
D_FF = 2816
MLA_SCALE = (MLA_NOPE + MLA_ROPE) ** -0.5
GQA_SCALE = GQA_DH ** -0.5
ML_K_SCALE = ML_DH ** -0.5
LOG2E = 1.4426950408889634

LANES = 128
HALO = 16
VMEM_LIMIT = 52 * 1024 * 1024

C_CQ, C_CKV, C_X, C_U, C_V, C_O, C_QG, C_KG, C_VG, IN_COLS_P = (
    0, 256, 384, 512, 1024, 1536, 2048, 2304, 2432, 2560)
GATE_BLOCKS = 2
FF_CHUNK = 256
TM = 512
TQ = 512


def _cparams(sem):
    return pltpu.CompilerParams(dimension_semantics=sem, vmem_limit_bytes=VMEM_LIMIT)


def _lspec(arr, l):
    nd = arr.ndim - 1
    return pl.BlockSpec((None,) + arr.shape[1:], lambda *_: (l,) + (0,) * nd, pipeline_mode=pl.Buffered(1))


def _dot(a, b):
    return jnp.dot(a, b, preferred_element_type=F32)


def _dot_nt(a, b):
    return lax.dot_general(a, b, (((1,), (1,)), ((), ())), preferred_element_type=F32)


def _rms(x, g):
    return (x * lax.rsqrt(jnp.mean(x * x, axis=-1, keepdims=True) + EPS)) * g


def _silu(x):
    return x * jax.nn.sigmoid(x)


def _adaln(x, g, shift, scale):
    return _rms(x, g) * (1.0 + scale) + shift


def _split2(x):
    hi = x.astype(BF16)
    lo = (x - hi.astype(F32)).astype(BF16)
    return hi, lo


def _split3(x):
    h1 = x.astype(BF16)
    r1 = x - h1.astype(F32)
    h2 = r1.astype(BF16)
    h3 = (r1 - h2.astype(F32)).astype(BF16)
    return h1, h2, h3


def _rope(x, cos, sa, sb, half):
    w = x.shape[-1]
    return x * cos + pltpu.roll(x, w - half, 1) * sa + pltpu.roll(x, half, 1) * sb


def _group_rms(x, gsum, g):
    hi, lo = _split2(x * x)
    ss = _dot(hi, gsum) + _dot(lo, gsum)
    return (x * lax.rsqrt(ss * (1.0 / GQA_DH) + EPS)) * g


def _mod_kernel(c_ref, w_ref, b_ref, o_ref):
    a = _silu(c_ref[...]).astype(BF16)
    o_ref[...] = _dot(a, w_ref[...].astype(BF16)) + b_ref[...]


def _modulation(cvec, w_ada, b_ada):
    nl = w_ada.shape[0]
    tn = 1536
    return pl.pallas_call(
        _mod_kernel,
        grid=(nl, 6 * D_MODEL // tn),
        in_specs=[
            pl.BlockSpec((16, D_MODEL), lambda l, j: (0, 0)),
            pl.BlockSpec((None, D_MODEL, tn), lambda l, j: (l, 0, j)),
            pl.BlockSpec((None, 1, tn), lambda l, j: (l, 0, j)),
        ],
        out_specs=pl.BlockSpec((None, 16, tn), lambda l, j: (l, 0, j)),
        out_shape=jax.ShapeDtypeStruct((nl, 16, 6 * D_MODEL), F32),
        compiler_params=_cparams(("arbitrary", "arbitrary")),
        name="modulation",
    )(cvec, w_ada, b_ada.reshape(nl, 1, 6 * D_MODEL))


def _in_proj_kernel(is_ctx, x_ref, mod_ref, g1_ref, win_ref, gq_ref, wuq_ref, gkv_ref, wukv_ref,
                    bg_ref, ggq_ref, ggk_ref, gsum_ref, *rest):
    if is_ctx:
        (qm_ref, km_ref, vm_ref, u_ref, vml_ref, oml_ref, gt_ref, qg_ref, kg_ref, vg_ref,
         ckv_ref, kr_ref) = rest
    else:
        (cm_ref, sam_ref, sbm_ref, cg_ref, sag_ref, sbg_ref,
         qm_ref, km_ref, vm_ref, u_ref, vml_ref, oml_ref, gt_ref, qg_ref, kg_ref, vg_ref) = rest

    def modulated(s):
        rs = slice(s * TM, (s + 1) * TM)
        return _adaln(x_ref[rs, :], g1_ref[...], mod_ref[0:1, :], mod_ref[1:2, :]).astype(BF16)

    nsub = x_ref.shape[0] // TM
    lane = lax.broadcasted_iota(jnp.int32, (TM, LANES), 1)
    kr_lanes = jnp.logical_and(lane >= MLA_NOPE, lane < MLA_NOPE + MLA_ROPE)
    y_next = modulated(0)
    for s in range(nsub):
        rs = slice(s * TM, (s + 1) * TM)
        y = y_next

        def proj(c0, width, y=y):
            return _dot(y, win_ref[:, c0:c0 + width])

        if not is_ctx:
            rope_m = (cm_ref[rs, :], sam_ref[rs, :], sbm_ref[rs, :], 8)
            rope_g = (cg_ref[rs, :], sag_ref[rs, :], sbg_ref[rs, :], 16)

        z_cq = proj(C_CQ, MLA_Q_RANK)
        ckv_kr = proj(C_CKV, MLA_KV_RANK + LANES)
        if s + 1 < nsub:
            y_next = modulated(s + 1)

        cqn = _rms(z_cq, gq_ref[...]).astype(BF16)
        z_qg = proj(C_QG, GQA_HEADS * GQA_DH)
        qz = _dot(cqn, wuq_ref[...])

        ckvn = _rms(ckv_kr[:, 0:MLA_KV_RANK], gkv_ref[...])
        x_blk = ckv_kr[:, MLA_KV_RANK:]
        kr = jnp.where(kr_lanes, x_blk, 0.0)
        if is_ctx:
            ckv_ref[rs, :] = ckvn
            kr_ref[rs, :] = kr
        else:
            kr = _rope(kr, *rope_m)
        kv_g = proj(C_KG, 2 * LANES)
        kvz = _dot(ckvn.astype(BF16), wukv_ref[...])

        z_u = proj(C_U, ML_WIDTH)
        for h in range(MLA_HEADS):
            blk = qz[:, LANES * h:LANES * (h + 1)]
            if not is_ctx:
                blk = _rope(blk, *rope_m)
            qm_ref[rs, LANES * h:LANES * (h + 1)] = (blk * (MLA_SCALE * LOG2E)).astype(qm_ref.dtype)

        z_v = proj(C_V, ML_WIDTH)
        for h in range(MLA_HEADS):
            km_ref[rs, LANES * h:LANES * (h + 1)] = (kvz[:, LANES * h:LANES * (h + 1)] + kr).astype(km_ref.dtype)
        vm_ref[rs, :] = kvz[:, MLA_HEADS * LANES:].astype(vm_ref.dtype)
        u_ref[rs, :] = z_u.astype(u_ref.dtype)

        z_o = proj(C_O, ML_WIDTH)
        qg = _group_rms(z_qg, gsum_ref[...], ggq_ref[...])
        kg = _group_rms(kv_g[:, 0:LANES], gsum_ref[0:LANES, 0:LANES], ggk_ref[...])
        vml_ref[rs, :] = z_v.astype(vml_ref.dtype)
        if not is_ctx:
            qg = jnp.concatenate([_rope(qg[:, 0:LANES], *rope_g), _rope(qg[:, LANES:], *rope_g)], axis=1)
            kg = _rope(kg, *rope_g)
        qg_ref[rs, :] = (qg * (GQA_SCALE * LOG2E)).astype(qg_ref.dtype)
        kg_ref[rs, :] = kg.astype(kg_ref.dtype)
        vg_ref[rs, :] = kv_g[:, LANES:].astype(vg_ref.dtype)

        oml_ref[rs, :] = z_o.astype(oml_ref.dtype)
        gates_t = (x_blk + bg_ref[...]).T
        for pr in range(GATE_BLOCKS):
            gt_ref[pr, :, rs] = gates_t[8 * pr:8 * (pr + 1), :]


def _mod_spec(l, seq, tm, is_ctx):
    if is_ctx:
        return pl.BlockSpec((None, None, 6, D_MODEL), lambda i: (l, 0, 0, 0))
    return pl.BlockSpec((None, None, 6, D_MODEL), lambda i: (l, 1 + (i * tm) // seq, 0, 0))


def _in_proj(x2, mod_all, l, seq, is_ctx, wts, tabs):
    m = x2.shape[0]
    tm = 2 * TM
    assert m % tm == 0 and (is_ctx or seq % tm == 0)
    nt = m // tm
    row = lambda i: (i, 0)
    names = ["g_norm1", "w_in", "g_mla_q", "w_uq", "g_mla_kv", "w_ukv", "b_gates", "g_gqa_q", "g_gqa_k"]
    in_specs = ([pl.BlockSpec((tm, D_MODEL), row), _mod_spec(l, seq, tm, is_ctx)]
                + [_lspec(wts[n], l) for n in names]
                + [pl.BlockSpec((2 * LANES, 2 * LANES), lambda i: (0, 0))])
    args = [x2, mod_all] + [wts[n] for n in names] + [wts["gsum"]]
    if not is_ctx:
        tpos = seq // tm
        in_specs += [pl.BlockSpec((tm, LANES), lambda i: (i % tpos, 0))] * 6
        args += list(tabs)
    act = F32 if is_ctx else BF16
    outs = [
        (MLA_HEADS * LANES, BF16),
        (MLA_HEADS * LANES, BF16),
        (MLA_HEADS * MLA_V, BF16),
        (ML_WIDTH, F32),
        (ML_WIDTH, BF16),
        (ML_WIDTH, F32),
        (None, F32),
        (GQA_HEADS * GQA_DH, BF16),
        (LANES, act),
        (LANES, act),
    ]
    if is_ctx:
        outs += [(MLA_KV_RANK, F32), (LANES, F32)]
    return pl.pallas_call(
        functools.partial(_in_proj_kernel, is_ctx),
        grid=(nt,),
        in_specs=in_specs,
        out_specs=[pl.BlockSpec((GATE_BLOCKS, 8, tm), lambda i: (0, 0, i)) if w is None
                   else pl.BlockSpec((tm, w), row) for w, _ in outs],
        out_shape=[jax.ShapeDtypeStruct((GATE_BLOCKS, 8, m) if w is None else (m, w), dt)
                   for w, dt in outs],
        compiler_params=_cparams(("parallel",)),
        name="in_proj_ctx" if is_ctx else "in_proj_lat",
    )(*args)


def _attn_kernel(na, nc_, has_cache, qa_ref, qc_ref, *refs):
    kva, kvc = list(refs[:2 * na]), refs[2 * na:2 * (na + nc_)]
    rest = refs[2 * (na + nc_):]
    nb, tq = qa_ref.shape[0], qa_ref.shape[1]
    lo = lax.broadcasted_iota(jnp.int32, (tq, LANES), 1) < (LANES // 2)
    if has_cache:
        ckv_ref, krp_ref, wukv_ref, oa_ref, oc_ref, kc_s, vc_s = rest
        for bi in range(nb):
            kvz = _dot(ckv_ref[bi].astype(BF16), wukv_ref[...])
            kr = krp_ref[bi]
            for h in range(MLA_HEADS):
                kc_s[bi, :, LANES * h:LANES * (h + 1)] = (kvz[:, LANES * h:LANES * (h + 1)] + kr).astype(BF16)
            vc_s[bi] = kvz[:, MLA_HEADS * LANES:].astype(BF16)
        kva += [kc_s, vc_s]
    else:
        oa_ref, oc_ref = rest

    def qk(unit):
        bi, is_mla, h = unit
        kv = kva if is_mla else kvc
        if is_mla:
            qh = qa_ref[bi, :, LANES * h:LANES * (h + 1)]
            ksl = slice(LANES * h, LANES * (h + 1))
        else:
            blk = qc_ref[bi, :, LANES * (h % 2):LANES * (h % 2 + 1)].astype(F32)
            qh = jnp.where(lo if h < 2 else jnp.logical_not(lo), blk, 0.0).astype(BF16)
            ksl = slice(0, LANES)
        return [_dot_nt(qh, kv[2 * j][bi, :, ksl].astype(BF16)) for j in range(len(kv) // 2)]

    def softmax(ss):
        mx = jnp.max(ss[0], axis=-1, keepdims=True)
        for s in ss[1:]:
            mx = jnp.maximum(mx, jnp.max(s, axis=-1, keepdims=True))
        es = [jnp.exp2(s - mx) for s in ss]
        den = es[0].sum(axis=-1, keepdims=True)
        for e in es[1:]:
            den = den + e.sum(axis=-1, keepdims=True)
        return [e.astype(BF16) for e in es], den

    def pv(unit, es, den):
        bi, is_mla, h = unit
        kv = kva if is_mla else kvc
        vsl = slice(LANES * (h // 2), LANES * (h // 2 + 1)) if is_mla else slice(0, LANES)
        acc = None
        for j, e in enumerate(es):
            a = _dot(e, kv[2 * j + 1][bi, :, vsl].astype(BF16))
            acc = a if acc is None else acc + a
        return acc / den

    units = [(bi, is_mla, h) for bi in range(nb) for is_mla in (True, False) for h in range(4)]
    out = {}
    ahead = qk(units[0])
    for n, unit in enumerate(units):
        ss = ahead
        if n + 1 < len(units):
            ahead = qk(units[n + 1])
        out[unit] = pv(unit, *softmax(ss))
    half = LANES // 2
    for bi in range(nb):
        a = [out[(bi, True, h)] for h in range(4)]
        c = [out[(bi, False, h)] for h in range(4)]
        oa_ref[bi, :, 0:LANES] = jnp.where(lo, a[0], a[1]).astype(oa_ref.dtype)
        oa_ref[bi, :, LANES:] = jnp.where(lo, a[2], a[3]).astype(oa_ref.dtype)
        oc_ref[bi, :, 0:LANES] = jnp.where(lo, c[0], pltpu.roll(c[1], half, 1)).astype(oc_ref.dtype)
        oc_ref[bi, :, LANES:] = jnp.where(lo, pltpu.roll(c[2], half, 1), c[3]).astype(oc_ref.dtype)


def _attention(q_a, q_c, segs_a, segs_c, cache=None):
    b, tq_all, _ = q_a.shape
    tq = min(TQ, tq_all)
    nb = max(1, min(b, 2 * TQ // tq_all))
    in_specs = [pl.BlockSpec((nb, tq, q_a.shape[2]), lambda i, j: (i, j, 0)),
                pl.BlockSpec((nb, tq, q_c.shape[2]), lambda i, j: (i, j, 0))]
    args = [q_a, q_c]
    for segs in (segs_a, segs_c):
        for k, v, layer in segs:
            for a in (k, v):
                if layer is None:
                    in_specs.append(pl.BlockSpec((nb,) + a.shape[1:], lambda i, j: (i, 0, 0)))
                else:
                    in_specs.append(pl.BlockSpec((nb, None) + a.shape[2:],
                                                 lambda i, j, layer=layer: (i, layer, 0, 0)))
                args.append(a)
    scratch = []
    if cache is not None:
        ckv, kr_pad, layer, w_ukv = cache
        tc = ckv.shape[2]
        cache_blk = pl.BlockSpec((nb, None, tc, LANES), lambda i, j: (i, layer, 0, 0))
        in_specs += [cache_blk, cache_blk, _lspec(w_ukv, layer)]
        args += [ckv, kr_pad, w_ukv]
        scratch = [pltpu.VMEM((nb, tc, MLA_HEADS * LANES), BF16), pltpu.VMEM((nb, tc, MLA_HEADS * MLA_V), BF16)]
    out_spec = pl.BlockSpec((nb, tq, 2 * LANES), lambda i, j: (i, j, 0))
    return pl.pallas_call(
        functools.partial(_attn_kernel, len(segs_a), len(segs_c), cache is not None),
        grid=(b // nb, tq_all // tq),
        in_specs=in_specs,
        out_specs=[out_spec, out_spec],
        out_shape=[jax.ShapeDtypeStruct((b, tq_all, 2 * LANES), BF16)] * 2,
        scratch_shapes=scratch,
        compiler_params=_cparams(("parallel", "parallel")),
        name="attn_lat" if cache is not None else "attn_ctx",
    )(*args)


def _log_sigmoid(x):
    return jnp.minimum(x, 0.0) - jnp.log(1.0 + jnp.exp(-jnp.abs(x)))


GATE_HEADS = 2


def _mlstm_kernel(has_init, nc, hps, u_ref, v_ref, o_ref, gt_ref, cw_ref, cb_ref, wqt_ref, wk_ref, go_ref,
                  *rest):
    if has_init:
        c0_ref, n0_ref, m0_ref, out_ref, qt_s, k_s, vt_s, ht_f, ht_b, acol_s, row_s, st_s = rest
    else:
        out_ref, cfin_ref, sfin_ref, qt_s, k_s, vt_s, ht_f, ht_b, acol_s, row_s, st_s = rest
    t = nc * ML_CHUNK
    sq = (ML_CHUNK, ML_CHUNK)

    rows = lax.broadcasted_iota(jnp.int32, (t, LANES), 0)
    r_i = lax.broadcasted_iota(jnp.int32, sq, 0)
    c_i = lax.broadcasted_iota(jnp.int32, sq, 1)
    lower = c_i <= r_i
    upper = c_i >= r_i
    eye = jnp.where(c_i == r_i, 1.0, 0.0).astype(BF16)
    tri_lo = jnp.where(lower, 1.0, 0.0).astype(BF16)
    tri_up = jnp.where(upper, 1.0, 0.0).astype(BF16)
    row8 = lax.broadcasted_iota(jnp.int32, (8, ML_CHUNK), 0)
    is_f = lax.broadcasted_iota(jnp.int32, (8, t), 0) % 4 >= 2
    lg_all = []
    for pb in range(hps // GATE_HEADS):
        gates = gt_ref[pb]
        lg_all.append(jnp.where(is_f, _log_sigmoid(gates), gates))

    def gate_prep(pb, c):
        sl = slice(c * ML_CHUNK, (c + 1) * ML_CHUNK)
        lg = lg_all[pb][:, sl]
        h1, h2, h3 = _split3(lg)
        cs = jnp.where(row8 < 4, _dot(h1, tri_up) + _dot(h2, tri_up) + _dot(h3, tri_up),
                       _dot(h1, tri_lo) + _dot(h2, tri_lo) + _dot(h3, tri_lo)) * LOG2E
        row_s[pb, :, sl] = cs
        a = lg * LOG2E - pltpu.roll(cs, 6, 0)
        for jj in range(GATE_HEADS):
            for d in range(2):
                r = 4 * d + jj
                acol_s[2 * (GATE_HEADS * pb + jj) + d, sl, :] = jnp.broadcast_to(a[r:r + 1, :], sq).T

    gate_units = [(pb, c) for pb in range(hps // GATE_HEADS) for c in range(nc)]
    share = -(-len(gate_units) // hps)
    for j in range(hps):
        hs = slice(LANES * j, LANES * (j + 1))
        u = u_ref[:, hs]
        up = jnp.where(rows == 0, 0.0, pltpu.roll(u, 1, 0))
        un = jnp.where(rows == t - 1, 0.0, pltpu.roll(u, t - 1, 0))
        uc = _silu(cw_ref[0:1, hs] * up + cw_ref[1:2, hs] * u + cw_ref[2:3, hs] * un + cb_ref[:, hs]).astype(BF16)
        for pb, c in gate_units[j * share:(j + 1) * share]:
            gate_prep(pb, c)
        vt_s[hs, :] = _dot_nt(eye, v_ref[:, hs]).astype(BF16)
        qt_s[hs, :] = _dot_nt(wqt_ref[j], uc).astype(BF16)
        k_s[:, hs] = _dot(uc, wk_ref[j]) * ML_K_SCALE

    for c in range(nc):
        sl = slice(c * ML_CHUNK, (c + 1) * ML_CHUNK)
        for j in range(hps):
            hs = slice(LANES * j, LANES * (j + 1))
            st_s[j, sl, :] = _dot(k_s[sl, hs].astype(BF16), qt_s[hs, sl])

    def chunk(j, d, c0, ct, n, m2):
        hs = slice(LANES * j, LANES * (j + 1))
        cs_ = pl.ds(c0, ML_CHUNK)
        last = 0 if d else ML_CHUNK - 1
        qt = qt_s[hs, cs_]
        k = k_s[cs_, hs]
        vt = vt_s[hs, cs_]
        acol = acol_s[2 * j + d, cs_, :]
        r = 2 + 4 * d + j % GATE_HEADS
        bc = row_s[j // GATE_HEADS, r:r + 1, cs_]
        g = bc + m2
        dlog = jnp.where(lower if d else upper, acol + bc, -jnp.inf)
        m_t = jnp.maximum(g, jnp.max(dlog, axis=0, keepdims=True))
        w = jnp.exp2(dlog - m_t)
        inter = jnp.exp2(g - m_t)
        st = st_s[j, cs_, :] * w
        qn = _dot(jnp.broadcast_to(n, (8, LANES)).astype(BF16), qt)[0:1, :]
        den = jnp.sum(st, axis=0, keepdims=True) + inter * qn
        numt = _dot(vt, st.astype(BF16)) + inter * _dot(ct.astype(BF16), qt)
        ht = numt * (1.0 / jnp.maximum(jnp.abs(den), jnp.exp2(-m_t)))
        b_last = jnp.broadcast_to(bc[:, last:last + 1], (1, LANES))
        m_new = jnp.broadcast_to(m_t[:, last:last + 1], (1, LANES))
        kw = k * jnp.exp2(acol + (b_last - m_new))
        decay = jnp.exp2(m2 + (b_last - m_new))
        ct_new = decay * ct + _dot(vt, kw.astype(BF16))
        n_new = decay * n + jnp.sum(kw, axis=0, keepdims=True)
        return ht, ct_new, n_new, m_new

    def step(i, carry):
        cf = i * ML_CHUNK
        cb = (nc - 1 - i) * ML_CHUNK
        if not isinstance(i, int):
            cf = pl.multiple_of(cf, ML_CHUNK)
            cb = pl.multiple_of(cb, ML_CHUNK)
        new = []
        for j in range(hps):
            hs = slice(LANES * j, LANES * (j + 1))
            for d, (c0, ht_s) in enumerate(((cf, ht_f), (cb, ht_b))):
                ht, ct, n, m2 = chunk(j, d, c0, *carry[2 * j + d])
                ht_s[hs, pl.ds(c0, ML_CHUNK)] = ht
                new.append((ct, n, m2))
        return tuple(new)

    if has_init:
        init = tuple((c0_ref[d, j].T, n0_ref[j, d:d + 1, :], m0_ref[j, d:d + 1, :] * LOG2E)
                     for j in range(hps) for d in range(2))
    else:
        init = tuple((jnp.zeros(sq, F32), jnp.zeros((1, LANES), F32), jnp.zeros((1, LANES), F32))
                     for j in range(hps) for d in range(2))
    if nc <= 16:
        carry = init
        for i in range(nc):
            carry = step(i, carry)
    else:
        carry = lax.fori_loop(0, nc, step, init, unroll=4)

    for j in range(hps):
        hs = slice(LANES * j, LANES * (j + 1))
        hsum = (ht_f[hs, :] + ht_b[hs, :]).T
        hn = _rms(hsum, go_ref[:, hs])
        out_ref[:, hs] = (hn * jax.nn.sigmoid(o_ref[:, hs])).astype(out_ref.dtype)

    if not has_init:
        for j in range(hps):
            for d in range(2):
                ct, n, m2 = carry[2 * j + d]
                cfin_ref[d, j] = ct.T
            nf, nb = carry[2 * j][1], carry[2 * j + 1][1]
            mf, mb = carry[2 * j][2], carry[2 * j + 1][2]
            sfin_ref[j] = jnp.concatenate(
                [nf, nb, mf * (1.0 / LOG2E), mb * (1.0 / LOG2E), jnp.zeros((4, LANES), F32)], axis=0)


def _mlstm(u, v_ml, o_ml, gates, l, wts, init):
    b, t, _ = u.shape
    nc = t // ML_CHUNK
    has_init = init is not None
    hps = ML_HEADS if t <= 4 * ML_CHUNK else GATE_HEADS
    w2 = hps * LANES
    pair_blk = lambda i, p: (i, 0, p)
    in_specs = [
        pl.BlockSpec((None, t, w2), pair_blk),
        pl.BlockSpec((None, t, w2), pair_blk),
        pl.BlockSpec((None, t, w2), pair_blk),
        pl.BlockSpec((hps // GATE_HEADS, 8, t), lambda i, p: (p, 0, i)),
        pl.BlockSpec((None, 3, w2), lambda i, p: (l, 0, p)),
        pl.BlockSpec((None, 1, w2), lambda i, p: (l, 0, p)),
        pl.BlockSpec((None, hps, ML_DH, ML_DH), lambda i, p: (l, p, 0, 0)),
        pl.BlockSpec((None, hps, ML_DH, ML_DH), lambda i, p: (l, p, 0, 0)),
        pl.BlockSpec((None, 1, w2), lambda i, p: (l, 0, p)),
    ]
    args = [u, v_ml, o_ml, gates, wts["w_ml_conv"], wts["b_ml_conv"], wts["w_ml_qt"], wts["w_ml_k"],
            wts["g_ml_out"]]
    out_specs = [pl.BlockSpec((None, t, w2), pair_blk)]
    out_shape = [jax.ShapeDtypeStruct((b, t, ML_WIDTH), BF16)]
    state_c = pl.BlockSpec((None, 2, hps, ML_DH, ML_DH), lambda i, p: (i, 0, p, 0, 0))
    state_s = pl.BlockSpec((None, hps, 8, LANES), lambda i, p: (i, p, 0, 0))
    if has_init:
        in_specs += [pl.BlockSpec((None, None, 2, hps, ML_DH, ML_DH), lambda i, p: (i, l, 0, p, 0, 0)),
                     pl.BlockSpec((None, None, hps, 8, LANES), lambda i, p: (i, l, p, 0, 0)),
                     pl.BlockSpec((None, None, hps, 8, LANES), lambda i, p: (i, l, p, 0, 0))]
        args += list(init)
    else:
        out_specs += [state_c, state_s]
        out_shape += [
            jax.ShapeDtypeStruct((b, 2, ML_HEADS, ML_DH, ML_DH), F32),
            jax.ShapeDtypeStruct((b, ML_HEADS, 8, LANES), F32),
        ]
    return pl.pallas_call(
        functools.partial(_mlstm_kernel, has_init, nc, hps),
        grid=(b, ML_HEADS // hps),
        in_specs=in_specs,
        out_specs=out_specs,
        out_shape=out_shape,
        scratch_shapes=[
            pltpu.VMEM((w2, t), BF16),
            pltpu.VMEM((t, w2), F32),
            pltpu.VMEM((w2, t), BF16),
            pltpu.VMEM((w2, t), F32),
            pltpu.VMEM((w2, t), F32),
            pltpu.VMEM((2 * hps, t, LANES), F32),
            pltpu.VMEM((hps // GATE_HEADS, 8, t), F32),
            pltpu.VMEM((hps, t, LANES), F32),
        ],
        compiler_params=_cparams(("parallel", "parallel")),
        name="mlstm_lat" if has_init else "mlstm_ctx",
    )(*args)


def _ffn_kernel(seq, tm, final, *refs):
    x_refs, oa_refs, ob_refs, oc_refs = refs[0:3], refs[3:6], refs[6:9], refs[9:12]
    (mod_ref, g2_ref, wout_ref, wup_ref, cw_ref, cb_ref, wdn_ref, gf_ref, y_ref,
     oext_s, yext_s, gext_s, h_s) = refs[12:]
    i = pl.program_id(0)
    a_w = MLA_HEADS * MLA_V
    ext = ((0, HALO), (HALO, tm), (HALO + tm, HALO))
    for (r0, n), k in zip(ext, (1, 0, 2)):
        oext_s[r0:r0 + n, 0:a_w] = oa_refs[k][...]
        oext_s[r0:r0 + n, a_w:a_w + ML_WIDTH] = ob_refs[k][...]
        oext_s[r0:r0 + n, a_w + ML_WIDTH:] = oc_refs[k][...]
    mix = _dot(oext_s[...], wout_ref[...])
    gate1 = mod_ref[2:3, :]
    x = x_refs[0][...] + gate1 * mix[HALO:HALO + tm, :]
    xp = x_refs[1][...] + gate1 * mix[0:HALO, :]
    xn = x_refs[2][...] + gate1 * mix[HALO + tm:, :]

    g2 = g2_ref[...]
    shift = mod_ref[3:4, :]
    scale = mod_ref[4:5, :]
    spans = tm > seq
    if spans:
        keep_p = keep_n = 1.0
        tok = lax.broadcasted_iota(jnp.int32, (tm, FF_CHUNK), 0) % seq
        first, last = tok == 0, tok == seq - 1
    else:
        keep_p = jnp.where((i * tm) % seq == 0, 0.0, 1.0)
        keep_n = jnp.where(((i + 1) * tm) % seq == 0, 0.0, 1.0)
    yext_s[0:HALO, :] = (_adaln(xp, g2, shift, scale) * keep_p).astype(BF16)
    yext_s[HALO:HALO + tm, :] = _adaln(x, g2, shift, scale).astype(BF16)
    yext_s[HALO + tm:, :] = (_adaln(xn, g2, shift, scale) * keep_n).astype(BF16)

    def up(c):
        c0 = c * FF_CHUNK
        gext_s[c % 2] = _dot(yext_s[...], wup_ref[:, D_FF + c0:D_FF + c0 + FF_CHUNK])
        return _dot(yext_s[HALO:HALO + tm, :], wup_ref[:, c0:c0 + FF_CHUNK])

    nchunk = D_FF // FF_CHUNK
    a_next = up(0)
    for c in range(nchunk):
        c0 = c * FF_CHUNK
        a = a_next
        if c + 1 < nchunk:
            a_next = up(c + 1)
        gs = gext_s.at[c % 2]
        g_prev = gs[HALO - 1:HALO - 1 + tm, :]
        g_next = gs[HALO + 1:HALO + 1 + tm, :]
        if spans:
            g_prev = jnp.where(first, 0.0, g_prev)
            g_next = jnp.where(last, 0.0, g_next)
        g = (cw_ref[0:1, c0:c0 + FF_CHUNK] * g_prev
             + cw_ref[1:2, c0:c0 + FF_CHUNK] * gs[HALO:HALO + tm, :]
             + cw_ref[2:3, c0:c0 + FF_CHUNK] * g_next
             + cb_ref[:, c0:c0 + FF_CHUNK])
        h_s[:, c0:c0 + FF_CHUNK] = (_silu(g) * a).astype(BF16)
    y = x + mod_ref[5:6, :] * _dot(h_s[...], wdn_ref[...])
    if final:
        y = _rms(y, gf_ref[...])
    y_ref[...] = y


def _ffn(x2, o_a, o_b, o_c, mod_all, l, seq, is_ctx, wts, g_final, final):
    m = x2.shape[0]
    tm = TM
    assert tm % seq == 0 or seq % tm == 0
    nb = tm // HALO
    last_blk = m // HALO - 1
    row = lambda i: (i, 0)
    prev = lambda i: (jnp.maximum(i * nb - 1, 0), 0)
    nxt = lambda i: (jnp.minimum((i + 1) * nb, last_blk), 0)
    in_specs, args = [], []
    for a in (x2, o_a, o_b, o_c):
        w = a.shape[1]
        in_specs += [pl.BlockSpec((tm, w), row), pl.BlockSpec((HALO, w), prev), pl.BlockSpec((HALO, w), nxt)]
        args += [a, a, a]
    names = ["g_norm2", "w_out", "w_ff_up", "w_ff_conv", "b_ff_conv", "w_ff_down"]
    in_specs += ([_mod_spec(l, seq, tm, is_ctx)] + [_lspec(wts[n], l) for n in names]
                 + [pl.BlockSpec((1, D_MODEL), lambda i: (0, 0))])
    args += [mod_all] + [wts[n] for n in names] + [g_final]
    return pl.pallas_call(
        functools.partial(_ffn_kernel, seq, tm, final),
        grid=(m // tm,),
        in_specs=in_specs,
        out_specs=pl.BlockSpec((tm, D_MODEL), row),
        out_shape=jax.ShapeDtypeStruct((m, D_MODEL), F32),
        scratch_shapes=[pltpu.VMEM((tm + 2 * HALO, D_MODEL), BF16),
                        pltpu.VMEM((tm + 2 * HALO, D_MODEL), BF16),
                        pltpu.VMEM((2, tm + 2 * HALO, FF_CHUNK), F32),
                        pltpu.VMEM((tm, D_FF), BF16)],
        compiler_params=_cparams(("parallel",)),
        name="ffn_ctx" if is_ctx else "ffn_lat",
    )(*args)


def _rope_tables(t):
    pos = np.arange(t)
    rows = (pos // GRID_W).astype(np.float32)
    cols = (pos % GRID_W).astype(np.float32)

    def group(p, d):
        inv = (np.float32(ROPE_THETA) ** (-np.arange(0, d, 2, dtype=np.float32) / np.float32(d))).astype(np.float32)
        ang = (p[:, None] * inv[None, :]).astype(np.float32)
        cs, sn, z = np.cos(ang), np.sin(ang), np.zeros_like(ang)
        return (np.concatenate([cs, cs], 1), np.concatenate([-sn, z], 1), np.concatenate([z, sn], 1))

    gr, gc = group(rows, MLA_ROPE // 2), group(cols, MLA_ROPE // 2)
    ones = np.ones((t, MLA_NOPE), np.float32)
    zeros = np.zeros((t, MLA_NOPE), np.float32)
    pad1 = np.ones((t, LANES - MLA_NOPE - MLA_ROPE), np.float32)
    pad0 = np.zeros((t, LANES - MLA_NOPE - MLA_ROPE), np.float32)
    mla = (np.concatenate([ones, gr[0], gc[0], pad1], 1),
           np.concatenate([zeros, gr[1], gc[1], pad0], 1),
           np.concatenate([zeros, gr[2], gc[2], pad0], 1))
    gr, gc = group(rows, GQA_DH // 2), group(cols, GQA_DH // 2)
    gqa = tuple(np.concatenate([gr[k], gc[k], gr[k], gc[k]], 1) for k in range(3))
    return tuple(jnp.asarray(a, F32) for a in mla + gqa)


def _prep_weights(p):
    w_in = p["w_in"]
    nl = w_in.shape[0]
    qg = [w_in[:, :, 1968 + GQA_DH * h:1968 + GQA_DH * (h + 1)] for h in (0, 2, 1, 3)]
    gh = GATE_HEADS

    def regroup(g16):
        return [g16[..., kind * ML_HEADS + gh * pr:kind * ML_HEADS + gh * (pr + 1)]
                for pr in range(GATE_BLOCKS) for kind in range(4)]

    zeros = lambda a, n: jnp.zeros(a.shape[:-1] + (n,), F32)
    b16 = p["b_ml_gates"]
    b_gates = jnp.concatenate(regroup(b16) + [zeros(b16, LANES - 4 * ML_HEADS)], axis=-1)
    w_in_p = jnp.concatenate([
        w_in[:, :, 0:384],
        *regroup(w_in[:, :, 1952:1968]), zeros(w_in, MLA_NOPE - 4 * ML_HEADS),
        w_in[:, :, 384:416], zeros(w_in, LANES - MLA_NOPE - MLA_ROPE),
        w_in[:, :, 416:1952],
        *qg,
        w_in[:, :, 2224:2480],
    ], axis=2).astype(BF16)
    w_uq = p["w_mla_uq"].reshape(nl, MLA_Q_RANK, MLA_HEADS, MLA_NOPE + MLA_ROPE)
    w_uq = jnp.pad(w_uq, ((0, 0), (0, 0), (0, 0), (0, LANES - MLA_NOPE - MLA_ROPE)))
    w_ukv = p["w_mla_ukv"].reshape(nl, MLA_KV_RANK, MLA_HEADS, MLA_NOPE + MLA_V)
    w_uk = jnp.pad(w_ukv[..., :MLA_NOPE], ((0, 0), (0, 0), (0, 0), (0, LANES - MLA_NOPE)))
    w_ukv_p = jnp.concatenate([w_uk.reshape(nl, MLA_KV_RANK, -1),
                               w_ukv[..., MLA_NOPE:].reshape(nl, MLA_KV_RANK, -1)], axis=2)
    grp = np.arange(2 * LANES) // GQA_DH
    row = lambda a: a[:, None, :]
    return {
        "g_norm1": row(p["g_norm1"]),
        "g_norm2": row(p["g_norm2"]),
        "w_in": w_in_p,
        "g_mla_q": row(p["g_mla_q"]),
        "w_uq": w_uq.reshape(nl, MLA_Q_RANK, -1).astype(BF16),
        "g_mla_kv": row(p["g_mla_kv"]),
        "w_ukv": w_ukv_p.astype(BF16),
        "b_gates": row(b_gates),
        "g_gqa_q": row(jnp.tile(p["g_gqa_q"], (1, GQA_HEADS))),
        "g_gqa_k": row(jnp.tile(p["g_gqa_k"], (1, GQA_KV_HEADS))),
        "gsum": jnp.asarray(grp[:, None] == grp[None, :], BF16),
        "w_ml_conv": p["w_ml_conv"],
        "b_ml_conv": row(p["b_ml_conv"]),
        "w_ml_qt": jnp.swapaxes(p["w_ml_q"], 2, 3).astype(BF16),
        "w_ml_k": p["w_ml_k"].astype(BF16),
        "g_ml_out": row(p["g_ml_out"]),
        "w_out": p["w_out"].astype(BF16),
        "w_ff_up": p["w_ff_up"].astype(BF16),
        "w_ff_conv": p["w_ff_conv"],
        "b_ff_conv": row(p["b_ff_conv"]),
        "w_ff_down": p["w_ff_down"].astype(BF16),
    }


def _layer(x2, mod_all, l, seq, is_ctx, wts, tabs, cache, g_final, final):
    m = x2.shape[0]
    b = m // seq
    r3 = lambda a: a.reshape(b, seq, a.shape[-1])
    outs = _in_proj(x2, mod_all, l, seq, is_ctx, wts, tabs)
    q_m, k_m, v_m, u, v_ml, o_ml, gates, q_g, k_g, v_g = outs[:10]
    segs_a = [(r3(k_m), r3(v_m), None)]
    segs_c = [(r3(k_g), r3(v_g), None)]
    init = None
    mla_cache = None
    if not is_ctx:
        ckv_c, kr_pad, kg_c, vg_c, ct0, n0p, m0p = cache
        mla_cache = (ckv_c, kr_pad, l, wts["w_ukv"])
        segs_c.append((kg_c, vg_c, l))
        init = (ct0, n0p, m0p)
    o_a, o_c = _attention(r3(q_m), r3(q_g), segs_a, segs_c, mla_cache)
    ml = _mlstm(r3(u), r3(v_ml), r3(o_ml), gates, l, wts, init)
    x2 = _ffn(x2, o_a.reshape(m, -1), ml[0].reshape(m, -1), o_c.reshape(m, -1), mod_all, l, seq, is_ctx, wts,
              g_final, final)
    state = None
    if is_ctx:
        ckv_n, kr_raw = outs[10:12]
        sfin = ml[2]
        state = (
            r3(ckv_n),
            r3(kr_raw)[:, :, MLA_NOPE:MLA_NOPE + MLA_ROPE],
            k_g.reshape(b, seq, GQA_KV_HEADS, GQA_DH),
            v_g.reshape(b, seq, GQA_KV_HEADS, GQA_DH),
            ml[1],
            jnp.swapaxes(sfin[:, :, 0:2, :], 1, 2),
            jnp.swapaxes(sfin[:, :, 2:4, 0], 1, 2),
        )
    return x2, state


def kernel(x_prompt, x_sample, cache_mla_ckv, cache_mla_krope, cache_gqa_k, cache_gqa_v, state_mlstm_C, state_mlstm_n, state_mlstm_m, c, c_ctx, w_ada, b_ada, g_norm1, g_norm2, w_in, g_mla_q, w_mla_uq, g_mla_kv, w_mla_ukv, w_ml_conv, b_ml_conv, w_ml_q, w_ml_k, b_ml_gates, g_ml_out, g_gqa_q, g_gqa_k, w_out, w_ff_up, w_ff_conv, b_ff_conv, w_ff_down, g_final):
    params = {
        "g_norm1": g_norm1, "g_norm2": g_norm2, "w_in": w_in, "g_mla_q": g_mla_q, "w_mla_uq": w_mla_uq,
        "g_mla_kv": g_mla_kv, "w_mla_ukv": w_mla_ukv, "w_ml_conv": w_ml_conv, "b_ml_conv": b_ml_conv,
        "w_ml_q": w_ml_q, "w_ml_k": w_ml_k, "b_ml_gates": b_ml_gates, "g_ml_out": g_ml_out,
        "g_gqa_q": g_gqa_q, "g_gqa_k": g_gqa_k, "w_out": w_out, "w_ff_up": w_ff_up, "w_ff_conv": w_ff_conv,
        "b_ff_conv": b_ff_conv, "w_ff_down": w_ff_down,
    }
    depth = w_in.shape[0]
    bp, sp, _ = x_prompt.shape
    bs, ss, _ = x_sample.shape
    assert bs + 1 <= 16 and sp % ML_CHUNK == 0 and ss % TM == 0 and (bp * sp) % TM == 0

    cvec = jnp.concatenate([c_ctx[None, :], c, jnp.zeros((16 - 1 - bs, D_MODEL), F32)], axis=0)
    mod_all = _modulation(cvec, w_ada, b_ada).reshape(depth, 16, 6, D_MODEL)
    tabs = _rope_tables(ss)
    gf = g_final[None, :]
    wts = _prep_weights(params)

    tc = cache_mla_ckv.shape[2]
    pad_rows = ((0, 0), (0, 0), (0, 0), (0, 8 - 2), (0, 0))
    cache = (
        cache_mla_ckv,
        jnp.pad(cache_mla_krope, ((0, 0), (0, 0), (0, 0), (MLA_NOPE, LANES - MLA_NOPE - MLA_ROPE))),
        cache_gqa_k.reshape(bs, depth, tc, GQA_KV_HEADS * GQA_DH),
        cache_gqa_v.reshape(bs, depth, tc, GQA_KV_HEADS * GQA_DH),
        state_mlstm_C,
        jnp.pad(jnp.swapaxes(state_mlstm_n, 2, 3), pad_rows),
        jnp.pad(jnp.broadcast_to(jnp.swapaxes(state_mlstm_m, 2, 3)[..., None], (bs, depth, ML_HEADS, 2, LANES)),
                pad_rows),
    )

    xp = x_prompt.reshape(bp * sp, D_MODEL)
    xs = x_sample.reshape(bs * ss, D_MODEL)
    states = []
    for l in range(depth):
        final = l == depth - 1
        xp, st = _layer(xp, mod_all, l, sp, True, wts, None, None, gf, final)
        states.append(st)
        xs, _ = _layer(xs, mod_all, l, ss, False, wts, tabs, cache, gf, final)
    new_state = tuple(jnp.stack([st[k] for st in states], axis=1) for k in range(7))
    return (xp.reshape(bp, sp, D_MODEL), xs.reshape(bs, ss, D_MODEL)) + new_state
```
